```python
import math
import jax, jax.numpy as jnp
from jax import lax
import numpy as np

D_MODEL = 1024
BATCH = 32
SEQ = 2048
DEPTH = 4
DEC_BATCH = 16
DEC_SEQ = 64
PAST_LEN = 2048

CHUNK = 64
N_MIXERS = 2
N_HGRN = (DEPTH + 1) // 2
N_ATTN = DEPTH // 2

HG_EXPAND = 128
HG_HEADS = D_MODEL // HG_EXPAND
HG_DK = HG_EXPAND
HG_DV = D_MODEL // HG_HEADS
GLA_BLOCK = 16

DA_HEAD_DIM = 64
DA_HEADS = D_MODEL // (2 * DA_HEAD_DIM)
Q_BLOCK = 128

EPS = 1e-6
NEG_INF = -1e30
F32 = jnp.float32

kernel_name = "hgrn2_diffattn_streaming_step"


def rms_norm(x, w):
    x32 = x.astype(F32)
    y = x32 * lax.rsqrt(jnp.mean(x32 * x32, axis=-1, keepdims=True) + EPS)
    return y * w.astype(F32)


def hgrn_lower_bounds(logits):
    p = jax.nn.softmax(logits.astype(F32), axis=0)
    return jnp.cumsum(p, axis=0) - p[0]


def _gla_block(S, blk):
    q, k, v, g = blk
    L = q.shape[1]
    b = jnp.cumsum(g, axis=1)
    causal = jnp.tril(jnp.ones((L, L), dtype=bool))[None, :, :, None, None]
    diff = b[:, :, None] - b[:, None, :]
    decay = jnp.exp(jnp.where(causal, diff, -jnp.inf))
    att = jnp.einsum('bthk,bshk,btshk->bhts', q, k, decay)
    o = (jnp.einsum('bthk,bhkv->bthv', q * jnp.exp(b), S)
         + jnp.einsum('bhts,bshv->bthv', att, v))
    b_last = b[:, -1]
    S_new = (jnp.exp(b_last)[..., None] * S
             + jnp.einsum('bshk,bshv->bhkv', k * jnp.exp(b_last[:, None] - b), v))
    return S_new, o


def gla(q, k, v, g, S0, block_len):
    B, T = q.shape[:2]
    nb = T // block_len

    def to_blocks(a):
        return jnp.moveaxis(a.reshape(B, nb, block_len, *a.shape[2:]), 1, 0)

    S, o = lax.scan(_gla_block, S0, (to_blocks(q), to_blocks(k), to_blocks(v), to_blocks(g)))
    o = jnp.moveaxis(o, 0, 1).reshape(B, T, HG_HEADS, HG_DV)
    return o, S


def hgrn_layer(x, S0, norm_w, w_in, lb, onorm_w, w_out, block_len):
    B, T, _ = x.shape
    h = rms_norm(x, norm_w).astype(x.dtype)
    q, fx, i, z = jnp.split(h @ w_in, 4, axis=-1)
    f = lb + (1.0 - lb) * jax.nn.sigmoid(fx.astype(F32))
    g = jnp.log(f)
    k = 1.0 - f
    shp = (B, T, HG_HEADS, HG_DK)
    q32 = q.astype(F32).reshape(shp) * (HG_DK ** -0.5)
    v32 = i.astype(F32).reshape(B, T, HG_HEADS, HG_DV)
    o, S = gla(q32, k.reshape(shp), v32, g.reshape(shp), S0.astype(F32), block_len)
    o = rms_norm(o.reshape(B, T, D_MODEL), onorm_w).astype(x.dtype) * jax.nn.silu(z)
    return x + o @ w_out, S.astype(x.dtype)


def diff_attend(q, k, v, q_pos, k_pos, slopes, lam):
    s = jnp.einsum('bqmhd,bkmhd->bmhqk', q.astype(F32), k.astype(F32)) * (DA_HEAD_DIM ** -0.5)
    dist = jnp.abs(q_pos[:, None] - k_pos[None, :]).astype(F32)
    visible = (k_pos[None, :] // CHUNK) <= (q_pos[:, None] // CHUNK)
    bias = jnp.where(visible[None], -slopes[:, None, None] * dist[None], NEG_INF)
    p = jax.nn.softmax(s + bias, axis=-1)
    a = p[:, 0] - lam * p[:, 1]
    return jnp.einsum('bhqk,bkhe->bqhe', a, v.astype(F32))


def diff_attn_layer(x, k_past, v_past, q_pos, k_pos, norm_w, w_in, qn_w, kn_w, lam_vec, subln_w,
                    w_out, layer_idx, q_block):
    B, T, _ = x.shape
    h = rms_norm(x, norm_w).astype(x.dtype)
    q, k, v, z = jnp.split(h @ w_in, 4, axis=-1)
    q = rms_norm(q.reshape(B, T, 2, DA_HEADS, DA_HEAD_DIM), qn_w).astype(x.dtype)
    k = rms_norm(k.reshape(B, T, 2, DA_HEADS, DA_HEAD_DIM), kn_w).astype(x.dtype)
    v = v.reshape(B, T, DA_HEADS, 2 * DA_HEAD_DIM)
    if k_past is None:
        k_all, v_all = k, v
    else:
        k_all = jnp.concatenate([k_past.astype(k.dtype), k], axis=1)
        v_all = jnp.concatenate([v_past.astype(v.dtype), v], axis=1)
    lam_init = 0.8 - 0.6 * math.exp(-0.3 * layer_idx)
    lv = lam_vec.astype(F32)
    lam = jnp.exp(jnp.sum(lv[0] * lv[1])) - jnp.exp(jnp.sum(lv[2] * lv[3])) + lam_init
    slopes = jnp.exp2(-8.0 * jnp.arange(1, DA_HEADS + 1, dtype=F32) / DA_HEADS)
    nb = T // q_block
    qb = jnp.moveaxis(q.reshape(B, nb, q_block, 2, DA_HEADS, DA_HEAD_DIM), 1, 0)
    pb = q_pos.reshape(nb, q_block)
    o = lax.map(lambda a: diff_attend(a[0], k_all, v_all, a[1], k_pos, slopes, lam), (qb, pb))
    o = jnp.moveaxis(o, 0, 1).reshape(B, T, DA_HEADS, 2 * DA_HEAD_DIM)
    o = rms_norm(o, subln_w) * (1.0 - lam_init)
    o = o.reshape(B, T, D_MODEL).astype(x.dtype) * jax.nn.silu(z)
    return x + o @ w_out, k, v


def setup_inputs(seed: int = 0) -> dict:
    key = jax.random.key(seed)
    ks = jax.random.split(key, 20)
    D = D_MODEL
    sc = D ** -0.5
    nrm = jax.random.normal
    return {
        "x_prompt": nrm(ks[0], (BATCH, SEQ, D), F32),
        "x_sample": nrm(ks[1], (DEC_BATCH, DEC_SEQ, D), F32),
        "cache_k": nrm(ks[2], (N_ATTN, DEC_BATCH, PAST_LEN, 2, DA_HEADS, DA_HEAD_DIM), F32),
        "cache_v": nrm(ks[3], (N_ATTN, DEC_BATCH, PAST_LEN, DA_HEADS, 2 * DA_HEAD_DIM), F32),
        "state_hgrn": 0.5 * nrm(ks[4], (N_HGRN, DEC_BATCH, HG_HEADS, HG_DK, HG_DV), F32),
        "norm_w": 1.0 + 0.02 * nrm(ks[5], (DEPTH, D), F32),
        "hgrn_w_in": sc * nrm(ks[6], (N_HGRN, D, 4 * D), F32),
        "hgrn_lb_logits": 0.1 * nrm(ks[7], (N_HGRN, D), F32),
        "hgrn_onorm_w": 1.0 + 0.02 * nrm(ks[8], (N_HGRN, D), F32),
        "hgrn_w_out": 0.5 * sc * nrm(ks[9], (N_HGRN, D, D), F32),
        "attn_w_in": sc * nrm(ks[10], (N_ATTN, D, 4 * D), F32),
        "attn_q_norm": 1.0 + 0.02 * nrm(ks[11], (N_ATTN, DA_HEAD_DIM), F32),
        "attn_k_norm": 1.0 + 0.02 * nrm(ks[12], (N_ATTN, DA_HEAD_DIM), F32),
        "attn_lambda": 0.1 * nrm(ks[13], (N_ATTN, 4, DA_HEAD_DIM), F32),
        "attn_subln": 1.0 + 0.02 * nrm(ks[14], (N_ATTN, 2 * DA_HEAD_DIM), F32),
        "attn_w_out": 0.5 * sc * nrm(ks[15], (N_ATTN, D, D), F32),
    }


def reference(x_prompt, x_sample, cache_k, cache_v, state_hgrn, norm_w, hgrn_w_in, hgrn_lb_logits,
              hgrn_onorm_w, hgrn_w_out, attn_w_in, attn_q_norm, attn_k_norm, attn_lambda, attn_subln,
              attn_w_out):
    yp, ys = x_prompt, x_sample
    Bp, Tp = yp.shape[0], yp.shape[1]
    Ts = ys.shape[1]
    P = cache_k.shape[2]
    pos_p = jnp.arange(Tp, dtype=jnp.int32)
    pos_sq = P + jnp.arange(Ts, dtype=jnp.int32)
    pos_sk = jnp.arange(P + Ts, dtype=jnp.int32)
    lbs = hgrn_lower_bounds(hgrn_lb_logits)
    kp, vp, ks_, vs_, sp, ss = [], [], [], [], [], []
    for l in range(DEPTH):
        j = l // N_MIXERS
        if l % N_MIXERS == 0:
            S0 = jnp.zeros((Bp, HG_HEADS, HG_DK, HG_DV), yp.dtype)
            yp, s_p = hgrn_layer(yp, S0, norm_w[l], hgrn_w_in[j], lbs[j], hgrn_onorm_w[j],
                                 hgrn_w_out[j], GLA_BLOCK)
            ys, s_s = hgrn_layer(ys, state_hgrn[j], norm_w[l], hgrn_w_in[j], lbs[j], hgrn_onorm_w[j],
                                 hgrn_w_out[j], Ts)
            sp.append(s_p)
            ss.append(s_s)
        else:
            yp, k_p, v_p = diff_attn_layer(yp, None, None, pos_p, pos_p, norm_w[l], attn_w_in[j],
                                           attn_q_norm[j], attn_k_norm[j], attn_lambda[j],
                                           attn_subln[j], attn_w_out[j], l, min(Q_BLOCK, Tp))
            ys, k_s, v_s = diff_attn_layer(ys, cache_k[j], cache_v[j], pos_sq, pos_sk, norm_w[l],
                                           attn_w_in[j], attn_q_norm[j], attn_k_norm[j],
                                           attn_lambda[j], attn_subln[j], attn_w_out[j], l, Ts)
            kp.append(k_p)
            vp.append(v_p)
            ks_.append(k_s)
            vs_.append(v_s)
    return (yp, ys, jnp.stack(kp), jnp.stack(vp), jnp.stack(ks_), jnp.stack(vs_), jnp.stack(sp), jnp.stack(ss))
```

```python
import functools
import math

import jax
import jax.numpy as jnp
from jax import lax
from jax.experimental import pallas as pl
from jax.experimental.pallas import tpu as pltpu

F32 = jnp.float32
BF16 = jnp.bfloat16

EPS = 1e-6
NEG_INF = -1e30
LANES = 128
HG_HEADS = 8
HG_DK = 128
DA_HEADS = 8
DA_HEAD_DIM = 64
MASK_CHUNK = 64
N_MIXERS = 2

REC_CHUNK = 64
REC_LEVELS = (1, 2, 4, 8, 16, 32)
ATTN_TILE = 256
VMEM_LIMIT = 56 * 1024 * 1024


def _dot(a, b):
    return jnp.dot(a, b, preferred_element_type=F32)


def _dot_nt(a, b):
    return lax.dot_general(a, b, (((1,), (1,)), ((), ())), preferred_element_type=F32)


def _dot_tn(a, b):
    return lax.dot_general(a, b, (((0,), (0,)), ((), ())), preferred_element_type=F32)


def _rms_rows(x, w):
    ms = jnp.mean(x * x, axis=-1, keepdims=True)
    return x * lax.rsqrt(ms + EPS) * w


def _resident(shape):
    nd = len(shape)
    return pl.BlockSpec(shape, lambda *_: (0,) * nd, pipeline_mode=pl.Buffered(1))


def _inproj_hgrn_kernel(x_ref, nw_ref, w_ref, q_ref, fx_ref, i_ref, z_ref):
    d = x_ref.shape[1]
    h = _rms_rows(x_ref[...], nw_ref[...]).astype(BF16)
    for c, o_ref in enumerate((q_ref, fx_ref, i_ref, z_ref)):
        o_ref[...] = _dot(h, w_ref[:, c * d:(c + 1) * d]).astype(o_ref.dtype)


def _inproj_attn_kernel(x_ref, nw_ref, w_ref, qnw_ref, knw_ref, g_ref,
                        q_ref, kf_ref, kb_ref, vf_ref, vb_ref, z_ref):
    d = x_ref.shape[1]
    h = _rms_rows(x_ref[...], nw_ref[...]).astype(BF16)
    gmat = g_ref[...]

    def group_norm(y, w):
        cols = []
        for t in range(d // LANES):
            yt = y[:, t * LANES:(t + 1) * LANES]
            ms = _dot((yt * yt).astype(BF16), gmat)
            cols.append(yt * lax.rsqrt(ms + EPS))
        return jnp.concatenate(cols, axis=1) * w

    q = group_norm(_dot(h, w_ref[:, 0:d]), qnw_ref[...])
    q_ref[...] = (q * (DA_HEAD_DIM ** -0.5)).astype(BF16)
    k = group_norm(_dot(h, w_ref[:, d:2 * d]), knw_ref[...])
    kf_ref[...] = k
    kb_ref[...] = k.astype(BF16)
    v = _dot(h, w_ref[:, 2 * d:3 * d])
    vf_ref[...] = v
    vb_ref[...] = v.astype(BF16)
    z_ref[...] = _dot(h, w_ref[:, 3 * d:4 * d]).astype(BF16)


def _inproj_hgrn(x2, nw, w_bf, tm):
    m, d = x2.shape
    row = pl.BlockSpec((tm, d), lambda i: (i, 0))
    return pl.pallas_call(
        _inproj_hgrn_kernel,
        grid=(m // tm,),
        in_specs=[row, _resident((1, d)), _resident((d, 4 * d))],
        out_specs=[row, row, row, row],
        out_shape=[jax.ShapeDtypeStruct((m, d), BF16), jax.ShapeDtypeStruct((m, d), F32),
                   jax.ShapeDtypeStruct((m, d), BF16), jax.ShapeDtypeStruct((m, d), BF16)],
        compiler_params=pltpu.CompilerParams(dimension_semantics=("parallel",),
                                             vmem_limit_bytes=VMEM_LIMIT),
        name="inproj_hgrn",
    )(x2, nw, w_bf)


def _inproj_attn(x2, nw, w_bf, qnw, knw, gmat, tm):
    m, d = x2.shape
    row = pl.BlockSpec((tm, d), lambda i: (i, 0))
    shp = lambda dt: jax.ShapeDtypeStruct((m, d), dt)
    return pl.pallas_call(
        _inproj_attn_kernel,
        grid=(m // tm,),
        in_specs=[row, _resident((1, d)), _resident((d, 4 * d)), _resident((1, d)),
                  _resident((1, d)), _resident((LANES, LANES))],
        out_specs=[row] * 6,
        out_shape=[shp(BF16), shp(F32), shp(BF16), shp(F32), shp(BF16), shp(BF16)],
        compiler_params=pltpu.CompilerParams(dimension_semantics=("parallel",),
                                             vmem_limit_bytes=VMEM_LIMIT),
        name="inproj_attn",
    )(x2, nw, w_bf, qnw, knw, gmat)


def _outproj_kernel(full_norm, o_ref, z_ref, x_ref, w_ref, ow_ref, y_ref):
    o = o_ref[...].astype(F32)
    if full_norm:
        o = _rms_rows(o, ow_ref[...])
    z = z_ref[...].astype(F32)
    gated = (o * (z * jax.nn.sigmoid(z))).astype(BF16)
    y_ref[...] = x_ref[...] + _dot(gated, w_ref[...])


def _outproj(o2, z2, x2, w_bf, ow, full_norm, tm):
    m, d = x2.shape
    row = pl.BlockSpec((tm, d), lambda i: (i, 0))
    return pl.pallas_call(
        functools.partial(_outproj_kernel, full_norm),
        grid=(m // tm,),
        in_specs=[row, row, row, _resident((d, d)), _resident((1, d))],
        out_specs=row,
        out_shape=jax.ShapeDtypeStruct((m, d), F32),
        compiler_params=pltpu.CompilerParams(dimension_semantics=("parallel",),
                                             vmem_limit_bytes=VMEM_LIMIT),
        name="outproj",
    )(o2, z2, x2, w_bf, ow)


def _range_matrices(c):
    t = jnp.arange(c)[:, None]
    s = jnp.arange(c)[None, :]
    blocks = [s <= t]
    for w in REC_LEVELS:
        ref = (t // (2 * w)) * (2 * w) + w - 1
        upper = (t & w) != 0
        blocks.append(jnp.where(upper, (s > ref) & (s <= t), (s > t) & (s <= ref)))
    blocks.append(s > t)
    e = jnp.concatenate(blocks, axis=0).astype(BF16)
    return jnp.concatenate([e, e, e], axis=1)


def _hgrn_rec_kernel(layer_j, has_s0, q_ref, fx_ref, v_ref, lbl_ref, emat_ref, *rest):
    if has_s0:
        s0_ref, o_ref, sfin_ref, st_ref = rest
    else:
        o_ref, sfin_ref, st_ref = rest
    c_idx = pl.program_id(1)
    c = q_ref.shape[1]
    d = q_ref.shape[2]

    @pl.when(c_idx == 0)
    def _():
        for h in range(HG_HEADS):
            if has_s0:
                st_ref[h] = s0_ref[0, h].T
            else:
                st_ref[h] = jnp.zeros((HG_DK, HG_DK), F32)

    lg = lbl_ref[...]
    ex = jnp.exp(lg - jnp.max(lg, axis=0, keepdims=True))
    p = ex / jnp.sum(ex, axis=0, keepdims=True)
    cs = p[0:1]
    for r in range(1, layer_j + 1):
        cs = cs + p[r:r + 1]
    lb = cs - p[0:1]

    f = lb + (1.0 - lb) * jax.nn.sigmoid(fx_ref[0])
    g = jnp.log(f)
    kk = 1.0 - f
    g1 = g.astype(BF16)
    r1 = g - g1.astype(F32)
    g2 = r1.astype(BF16)
    g3 = (r1 - g2.astype(F32)).astype(BF16)
    decay = jnp.exp(_dot(emat_ref[...], jnp.concatenate([g1, g2, g3], axis=0)))

    q = q_ref[0].astype(F32) * (HG_DK ** -0.5)
    row = lax.broadcasted_iota(jnp.int32, (c, d), 0)
    q_state = (q * decay[0:c]).astype(BF16)
    k_state = (kk * decay[(len(REC_LEVELS) + 1) * c:]).astype(BF16)
    state_decay = decay[c - 1:c]
    q_lv, k_lv = [q.astype(BF16)], [kk.astype(BF16)]
    for li, w in enumerate(REC_LEVELS):
        x = decay[(li + 1) * c:(li + 2) * c]
        upper = (row & w) != 0
        q_lv.append(jnp.where(upper, q * x, 0.0).astype(BF16))
        k_lv.append(jnp.where(upper, 0.0, kk * x).astype(BF16))

    tt = lax.broadcasted_iota(jnp.int32, (c, c), 0)
    ss = lax.broadcasted_iota(jnp.int32, (c, c), 1)
    masks = [tt == ss] + [(tt // (2 * w)) == (ss // (2 * w)) for w in REC_LEVELS]

    for h in range(HG_HEADS):
        sl = slice(h * HG_DK, (h + 1) * HG_DK)
        att = jnp.zeros((c, c), F32)
        for lv in range(len(masks)):
            att = att + jnp.where(masks[lv], _dot_nt(q_lv[lv][:, sl], k_lv[lv][:, sl]), 0.0)
        vh = v_ref[0, :, sl]
        st = st_ref[h]
        o = _dot(att.astype(BF16), vh) + _dot_nt(q_state[:, sl], st.astype(BF16))
        o_ref[0, :, sl] = o.astype(o_ref.dtype)
        st_ref[h] = st * state_decay[:, sl] + _dot_tn(vh, k_state[:, sl])

    @pl.when(c_idx == pl.num_programs(1) - 1)
    def _():
        for h in range(HG_HEADS):
            sfin_ref[0, h] = st_ref[h].T


def _hgrn_rec(q, fx, v, lb_logits, s0, layer_j):
    b, t, d = q.shape
    c = REC_CHUNK
    blk = pl.BlockSpec((1, c, d), lambda i, j: (i, j, 0))
    st_blk = pl.BlockSpec((1, HG_HEADS, HG_DK, HG_DK), lambda i, j: (i, 0, 0, 0))
    emat = _range_matrices(c)
    in_specs = [blk, blk, blk, _resident(lb_logits.shape), _resident(emat.shape)]
    args = [q, fx, v, lb_logits, emat]
    if s0 is not None:
        in_specs.append(st_blk)
        args.append(s0)
    return pl.pallas_call(
        functools.partial(_hgrn_rec_kernel, layer_j, s0 is not None),
        grid=(b, t // c),
        in_specs=in_specs,
        out_specs=[blk, st_blk],
        out_shape=[jax.ShapeDtypeStruct((b, t, d), BF16),
                   jax.ShapeDtypeStruct((b, HG_HEADS, HG_DK, HG_DK), F32)],
        scratch_shapes=[pltpu.VMEM((HG_HEADS, HG_DK, HG_DK), F32)],
        compiler_params=pltpu.CompilerParams(dimension_semantics=("parallel", "arbitrary"),
                                             vmem_limit_bytes=VMEM_LIMIT),
        name="hgrn_rec",
    )(*args)


def _attn_kernel(q_off, causal, n_full, tk, tk_rem, lam_init,
                 slopes_ref, lam_ref, subw_ref, q0_ref, q1_ref, k0_ref, k1_ref, v_ref,
                 o_ref, m_sc, l_sc, acc_sc):
    j = pl.program_id(1)
    i = pl.program_id(2)
    tq = q0_ref.shape[1]
    q_start = q_off + i * tq

    lane = lax.broadcasted_iota(jnp.int32, (tq, LANES), 1)
    zero = jnp.zeros((tq, LANES), BF16)
    qm = [[jnp.where(lane < DA_HEAD_DIM, qr[0], zero), jnp.where(lane >= DA_HEAD_DIM, qr[0], zero)]
          for qr in (q0_ref, q1_ref)]
    k_refs = (k0_ref, k1_ref)

    m_sc[...] = jnp.full(m_sc.shape, -jnp.inf, F32)
    l_sc[...] = jnp.zeros(l_sc.shape, F32)
    acc_sc[...] = jnp.zeros(acc_sc.shape, F32)

    def tile(k_start, tkk, masked):
        row = lax.broadcasted_iota(jnp.int32, (tkk, tq), 0)
        col = lax.broadcasted_iota(jnp.int32, (tkk, tq), 1)
        dist = jnp.abs((q_start - k_start) + (col - row)).astype(F32)
        if masked:
            visible = ((k_start + row) // MASK_CHUNK) <= ((q_start + col) // MASK_CHUNK)
        for r in range(2):
            bias = -slopes_ref[2 * j + r] * dist
            vt = v_ref[0, pl.ds(k_start, tkk), r * LANES:(r + 1) * LANES]
            for mp in range(2):
                idx = 2 * r + mp
                kt = k_refs[mp][0, pl.ds(k_start, tkk), :]
                s = _dot_nt(kt, qm[mp][r]) + bias
                if masked:
                    s = jnp.where(visible, s, NEG_INF)
                m_prev = m_sc[idx]
                m_new = jnp.maximum(m_prev, jnp.max(s, axis=0, keepdims=True))
                alpha = jnp.exp(m_prev - m_new)
                p = jnp.exp(s - m_new)
                l_sc[idx] = alpha * l_sc[idx] + jnp.sum(p, axis=0, keepdims=True)
                acc_sc[idx] = alpha * acc_sc[idx] + _dot_tn(vt, p.astype(BF16))
                m_sc[idx] = m_new

    def full_tile(kt_idx, carry):
        tile(pl.multiple_of(kt_idx * tk, tk), tk, False)
        return carry

    if causal:
        lax.fori_loop(0, i, full_tile, 0)
        tile(pl.multiple_of(i * tk, tk), tk, True)
    else:
        lax.fori_loop(0, n_full, full_tile, 0)
        if tk_rem:
            tile(n_full * tk, tk_rem, False)

    lv = lam_ref[...]
    lam = (jnp.exp(jnp.sum(lv[0:1] * lv[1:2], axis=1, keepdims=True))
           - jnp.exp(jnp.sum(lv[2:3] * lv[3:4], axis=1, keepdims=True)) + lam_init)
    for r in range(2):
        o = acc_sc[2 * r] / l_sc[2 * r] - lam * (acc_sc[2 * r + 1] / l_sc[2 * r + 1])
        ms = jnp.mean(o * o, axis=0, keepdims=True)
        o = o * lax.rsqrt(ms + EPS) * subw_ref[...] * (1.0 - lam_init)
        o_ref[0, :, r * LANES:(r + 1) * LANES] = o.T.astype(o_ref.dtype)


def _attention(q, k, v, lam_vec, subln_w, slopes, q_off, causal, lam_init):
    b, t_q, d = q.shape
    t_k = k.shape[1]
    half = d // (2 * LANES)
    if causal:
        tq = tk = ATTN_TILE
        n_full, tk_rem = 0, 0
    else:
        tq = t_q
        tk = ATTN_TILE
        n_full, tk_rem = t_k // tk, t_k % tk
    q_spec = lambda off: pl.BlockSpec((1, tq, LANES), lambda bi, j, i: (bi, i, j + off))
    k_spec = lambda off: pl.BlockSpec((1, t_k, LANES), lambda bi, j, i: (bi, 0, j + off))
    v_spec = pl.BlockSpec((1, t_k, 2 * LANES), lambda bi, j, i: (bi, 0, j))
    o_spec = pl.BlockSpec((1, tq, 2 * LANES), lambda bi, j, i: (bi, i, j))
    return pl.pallas_call(
        functools.partial(_attn_kernel, q_off, causal, n_full, tk, tk_rem, lam_init),
        grid=(b, half, t_q // tq),
        in_specs=[pl.BlockSpec(memory_space=pltpu.SMEM),
                  pl.BlockSpec(lam_vec.shape, lambda bi, j, i: (0, 0)),
                  pl.BlockSpec((2 * DA_HEAD_DIM, 1), lambda bi, j, i: (0, 0)),
                  q_spec(0), q_spec(half), k_spec(0), k_spec(half), v_spec],
        out_specs=o_spec,
        out_shape=jax.ShapeDtypeStruct((b, t_q, d), BF16),
        scratch_shapes=[pltpu.VMEM((4, 1, tq), F32), pltpu.VMEM((4, 1, tq), F32),
                        pltpu.VMEM((4, 2 * DA_HEAD_DIM, tq), F32)],
        compiler_params=pltpu.CompilerParams(
            dimension_semantics=("parallel", "parallel", "arbitrary"),
            vmem_limit_bytes=VMEM_LIMIT),
        name="diff_attn",
    )(slopes, lam_vec, subln_w.reshape(2 * DA_HEAD_DIM, 1), q, q, k, k, v)


def _row_tile(m):
    return 512 if m % 512 == 0 else 256


def _hgrn_layer(x, s0, nw, w_in_bf, lb_logits, onw, w_out_bf, layer_j):
    b, t, d = x.shape
    x2 = x.reshape(b * t, d)
    tm = _row_tile(b * t)
    q, fx, iv, z = _inproj_hgrn(x2, nw.reshape(1, d), w_in_bf, tm)
    o, s_new = _hgrn_rec(q.reshape(b, t, d), fx.reshape(b, t, d), iv.reshape(b, t, d),
                         lb_logits, s0, layer_j)
    y = _outproj(o.reshape(b * t, d), z, x2, w_out_bf, onw.reshape(1, d), True, tm)
    return y.reshape(b, t, d), s_new


def _attn_layer(x, k_past, v_past, nw, w_in_bf, qn_w, kn_w, lam_vec, subln_w, w_out_bf, layer_idx):
    b, t, d = x.shape
    x2 = x.reshape(b * t, d)
    groups = d // DA_HEAD_DIM
    gmat = jnp.kron(jnp.eye(LANES // DA_HEAD_DIM, dtype=F32),
                    jnp.full((DA_HEAD_DIM, DA_HEAD_DIM), 1.0 / DA_HEAD_DIM, F32)).astype(BF16)
    q, kf, kb, vf, vb, z = _inproj_attn(x2, nw.reshape(1, d), w_in_bf,
                                        jnp.tile(qn_w, groups).reshape(1, d),
                                        jnp.tile(kn_w, groups).reshape(1, d), gmat, 256)
    lam_init = 0.8 - 0.6 * math.exp(-0.3 * layer_idx)
    slopes = jnp.exp2(-8.0 * jnp.arange(1, DA_HEADS + 1, dtype=F32) / DA_HEADS)
    kb = kb.reshape(b, t, d)
    vb = vb.reshape(b, t, d)
    if k_past is None:
        k_all, v_all, q_off, causal = kb, vb, 0, True
    else:
        past = k_past.shape[1]
        k_all = jnp.concatenate([k_past.reshape(b, past, d).astype(BF16), kb], axis=1)
        v_all = jnp.concatenate([v_past.reshape(b, past, d).astype(BF16), vb], axis=1)
        q_off, causal = past, False
    o = _attention(q.reshape(b, t, d), k_all, v_all, lam_vec, subln_w, slopes, q_off, causal,
                   lam_init)
    y = _outproj(o.reshape(b * t, d), z, x2, w_out_bf, jnp.ones((1, d), F32), False, _row_tile(b * t))
    k_new = kf.reshape(b, t, 2, DA_HEADS, DA_HEAD_DIM)
    v_new = vf.reshape(b, t, DA_HEADS, 2 * DA_HEAD_DIM)
    return y.reshape(b, t, d), k_new, v_new


def kernel(x_prompt, x_sample, cache_k, cache_v, state_hgrn, norm_w, hgrn_w_in, hgrn_lb_logits,
           hgrn_onorm_w, hgrn_w_out, attn_w_in, attn_q_norm, attn_k_norm, attn_lambda, attn_subln,
           attn_w_out):
    depth = norm_w.shape[0]
    yp, ys = x_prompt, x_sample
    kp, vp, ks_, vs_, sp, ss = [], [], [], [], [], []
    for l in range(depth):
        j = l // N_MIXERS
        if l % N_MIXERS == 0:
            w_in = hgrn_w_in[j].astype(BF16)
            w_out = hgrn_w_out[j].astype(BF16)
            yp, s_p = _hgrn_layer(yp, None, norm_w[l], w_in, hgrn_lb_logits, hgrn_onorm_w[j],
                                  w_out, j)
            ys, s_s = _hgrn_layer(ys, state_hgrn[j], norm_w[l], w_in, hgrn_lb_logits,
                                  hgrn_onorm_w[j], w_out, j)
            sp.append(s_p)
            ss.append(s_s)
        else:
            w_in = attn_w_in[j].astype(BF16)
            w_out = attn_w_out[j].astype(BF16)
            yp, k_p, v_p = _attn_layer(yp, None, None, norm_w[l], w_in, attn_q_norm[j],
                                       attn_k_norm[j], attn_lambda[j], attn_subln[j], w_out, l)
            ys, k_s, v_s = _attn_layer(ys, cache_k[j], cache_v[j], norm_w[l], w_in, attn_q_norm[j],
                                       attn_k_norm[j], attn_lambda[j], attn_subln[j], w_out, l)
            kp.append(k_p)
            vp.append(v_p)
            ks_.append(k_s)
            vs_.append(v_s)
    return (yp, ys, jnp.stack(kp), jnp.stack(vp), jnp.stack(ks_), jnp.stack(vs_), jnp.stack(sp),
            jnp.stack(ss))
```

```python
import functools
import math

import jax
import jax.numpy as jnp
from jax import lax
from jax.experimental import pallas as pl
from jax.experimental.pallas import tpu as pltpu

F32 = jnp.float32
BF16 = jnp.bfloat16

EPS = 1e-6
NEG_INF = -1e30
LOG2E = 1.4426950408889634
LANES = 128
HG_HEADS = 8
HG_DK = 128
DA_HEADS = 8
DA_HEAD_DIM = 64
MASK_CHUNK = 64
N_MIXERS = 2

REC_CHUNK = 64
REC_LEVELS = (1, 2, 4, 8, 16, 32)
ATTN_TILE = 256
FIXED_SHIFT_MAX_SCORE = 40.0
VMEM_LIMIT = 56 * 1024 * 1024


def _dot(a, b):
    return jnp.dot(a, b, preferred_element_type=F32)


def _dot_nt(a, b):
    return lax.dot_general(a, b, (((1,), (1,)), ((), ())), preferred_element_type=F32)


def _dot_tn(a, b):
    return lax.dot_general(a, b, (((0,), (0,)), ((), ())), preferred_element_type=F32)


def _rms_rows(x, w):
    ms = jnp.mean(x * x, axis=-1, keepdims=True)
    return x * lax.rsqrt(ms + EPS) * w


def _resident(shape):
    nd = len(shape)
    return pl.BlockSpec(shape, lambda *_: (0,) * nd, pipeline_mode=pl.Buffered(1))


def _params(*semantics):
    return pltpu.CompilerParams(dimension_semantics=semantics, vmem_limit_bytes=VMEM_LIMIT)


def _inproj_hgrn_kernel(x_ref, nw_ref, w_ref, q_ref, fx_ref, i_ref, z_ref):
    d = x_ref.shape[1]
    h = _rms_rows(x_ref[...], nw_ref[...]).astype(BF16)
    for c, o_ref in enumerate((q_ref, fx_ref, i_ref, z_ref)):
        o_ref[...] = _dot(h, w_ref[:, c * d:(c + 1) * d]).astype(o_ref.dtype)


def _inproj_hgrn(x2, nw, w_bf, tm):
    m, d = x2.shape
    row = pl.BlockSpec((tm, d), lambda i: (i, 0))
    return pl.pallas_call(
        _inproj_hgrn_kernel,
        grid=(m // tm,),
        in_specs=[row, _resident((1, d)), _resident((d, 4 * d))],
        out_specs=[row, row, row, row],
        out_shape=[jax.ShapeDtypeStruct((m, d), BF16), jax.ShapeDtypeStruct((m, d), F32),
                   jax.ShapeDtypeStruct((m, d), BF16), jax.ShapeDtypeStruct((m, d), BF16)],
        compiler_params=_params("parallel"),
        name="inproj_hgrn",
    )(x2, nw, w_bf)


def _inproj_attn_kernel(k_transposed, n_alias, x_ref, nw_ref, w_ref, qnw_ref, knw_ref, g_ref, *rest):
    q_ref, kb_ref, vb_ref, z_ref, ks_ref, vs_ref = rest[n_alias:]
    d = x_ref.shape[1]
    h = _rms_rows(x_ref[...], nw_ref[...]).astype(BF16)
    gmat = g_ref[...]

    def group_norm(y, w):
        cols = []
        for t in range(d // LANES):
            yt = y[:, t * LANES:(t + 1) * LANES]
            ms = _dot((yt * yt).astype(BF16), gmat)
            cols.append(yt * lax.rsqrt(ms + EPS))
        return jnp.concatenate(cols, axis=1) * w

    q = group_norm(_dot(h, w_ref[:, 0:d]), qnw_ref[...])
    q_ref[...] = (q * (DA_HEAD_DIM ** -0.5 * LOG2E)).astype(BF16)
    k = group_norm(_dot(h, w_ref[:, d:2 * d]), knw_ref[...])
    kb_ref[...] = k.astype(BF16)
    ks_ref[0, 0] = k.T if k_transposed else k
    v = _dot(h, w_ref[:, 2 * d:3 * d])
    vs_ref[0] = v
    vb_ref[...] = v.astype(BF16)
    z_ref[...] = _dot(h, w_ref[:, 3 * d:4 * d]).astype(BF16)


def _inproj_attn(x2, seq, nw, w_bf, qnw, knw, gmat, layer_j, n_layers, k_stack, v_stack,
                 k_transposed, tm):
    m, d = x2.shape
    batch = m // seq
    tiles_per_seq = seq // tm
    row = pl.BlockSpec((tm, d), lambda i: (i, 0))
    if k_transposed:
        ks_shape = (n_layers, batch, d, seq)
        ks_spec = pl.BlockSpec((1, 1, d, tm),
                               lambda i: (layer_j, i // tiles_per_seq, 0, i % tiles_per_seq))
    else:
        ks_shape = (n_layers, batch, seq, d)
        ks_spec = pl.BlockSpec((1, 1, tm, d),
                               lambda i: (layer_j, i // tiles_per_seq, i % tiles_per_seq, 0))
    vs_spec = pl.BlockSpec((1, tm, d), lambda i: (layer_j, i, 0))
    in_specs = [row, _resident((1, d)), _resident((d, 4 * d)), _resident((1, d)),
                _resident((1, d)), _resident((LANES, LANES))]
    args = [x2, nw, w_bf, qnw, knw, gmat]
    aliases = {}
    if k_stack is not None:
        in_specs += [pl.BlockSpec(memory_space=pl.ANY)] * 2
        aliases = {len(args): 4, len(args) + 1: 5}
        args += [k_stack, v_stack]
    shp = lambda dt: jax.ShapeDtypeStruct((m, d), dt)
    return pl.pallas_call(
        functools.partial(_inproj_attn_kernel, k_transposed, len(aliases)),
        grid=(m // tm,),
        in_specs=in_specs,
        out_specs=[row, row, row, row, ks_spec, vs_spec],
        out_shape=[shp(BF16), shp(BF16), shp(BF16), shp(BF16),
                   jax.ShapeDtypeStruct(ks_shape, F32),
                   jax.ShapeDtypeStruct((n_layers, m, d), F32)],
        input_output_aliases=aliases,
        compiler_params=_params("parallel"),
        name="inproj_attn",
    )(*args)


def _outproj_kernel(full_norm, o_ref, z_ref, x_ref, w_ref, ow_ref, y_ref):
    o = o_ref[...].astype(F32)
    if full_norm:
        o = _rms_rows(o, ow_ref[...])
    z = z_ref[...].astype(F32)
    gated = (o * (z * jax.nn.sigmoid(z))).astype(BF16)
    y_ref[...] = x_ref[...] + _dot(gated, w_ref[...])


def _outproj(o2, z2, x2, w_bf, ow, full_norm, tm):
    m, d = x2.shape
    row = pl.BlockSpec((tm, d), lambda i: (i, 0))
    return pl.pallas_call(
        functools.partial(_outproj_kernel, full_norm),
        grid=(m // tm,),
        in_specs=[row, row, row, _resident((d, d)), _resident((1, d))],
        out_specs=row,
        out_shape=jax.ShapeDtypeStruct((m, d), F32),
        compiler_params=_params("parallel"),
        name="outproj",
    )(o2, z2, x2, w_bf, ow)


def _range_matrices(c):
    t = jnp.arange(c)[:, None]
    s = jnp.arange(c)[None, :]
    blocks = [s <= t]
    for w in REC_LEVELS:
        ref = (t // (2 * w)) * (2 * w) + w - 1
        upper = (t & w) != 0
        blocks.append(jnp.where(upper, (s > ref) & (s <= t), (s > t) & (s <= ref)))
    blocks.append(s > t)
    e = jnp.concatenate(blocks, axis=0).astype(BF16)
    return jnp.concatenate([e, e, e], axis=1)


def _hgrn_rec_kernel(layer_j, has_s0, q_ref, fx_ref, v_ref, lbl_ref, emat_ref, *rest):
    if has_s0:
        s0_ref, o_ref, sfin_ref, st_ref = rest
    else:
        o_ref, sfin_ref, st_ref = rest
    c_idx = pl.program_id(1)
    c = q_ref.shape[1]
    d = q_ref.shape[2]

    @pl.when(c_idx == 0)
    def _():
        for h in range(HG_HEADS):
            if has_s0:
                st_ref[h] = s0_ref[0, h].T
            else:
                st_ref[h] = jnp.zeros((HG_DK, HG_DK), F32)

    lg = lbl_ref[...]
    ex = jnp.exp(lg - jnp.max(lg, axis=0, keepdims=True))
    p = ex / jnp.sum(ex, axis=0, keepdims=True)
    cs = p[0:1]
    for r in range(1, layer_j + 1):
        cs = cs + p[r:r + 1]
    lb = cs - p[0:1]

    f = lb + (1.0 - lb) * jax.nn.sigmoid(fx_ref[0])
    g = jnp.log(f)
    kk = 1.0 - f
    g1 = g.astype(BF16)
    r1 = g - g1.astype(F32)
    g2 = r1.astype(BF16)
    g3 = (r1 - g2.astype(F32)).astype(BF16)
    decay = jnp.exp(_dot(emat_ref[...], jnp.concatenate([g1, g2, g3], axis=0)))

    q = q_ref[0].astype(F32) * (HG_DK ** -0.5)
    row = lax.broadcasted_iota(jnp.int32, (c, d), 0)
    q_state = (q * decay[0:c]).astype(BF16)
    k_state = (kk * decay[(len(REC_LEVELS) + 1) * c:]).astype(BF16)
    state_decay = decay[c - 1:c]
    q_lv, k_lv = [q.astype(BF16)], [kk.astype(BF16)]
    for li, w in enumerate(REC_LEVELS):
        x = decay[(li + 1) * c:(li + 2) * c]
        upper = (row & w) != 0
        q_lv.append(jnp.where(upper, q * x, 0.0).astype(BF16))
        k_lv.append(jnp.where(upper, 0.0, kk * x).astype(BF16))

    tt = lax.broadcasted_iota(jnp.int32, (c, c), 0)
    ss = lax.broadcasted_iota(jnp.int32, (c, c), 1)
    masks = [tt == ss] + [(tt // (2 * w)) == (ss // (2 * w)) for w in REC_LEVELS]

    for h in range(HG_HEADS):
        sl = slice(h * HG_DK, (h + 1) * HG_DK)
        att = jnp.zeros((c, c), F32)
        for lv in range(len(masks)):
            att = att + jnp.where(masks[lv], _dot_nt(q_lv[lv][:, sl], k_lv[lv][:, sl]), 0.0)
        vh = v_ref[0, :, sl]
        st = st_ref[h]
        o = _dot(att.astype(BF16), vh) + _dot_nt(q_state[:, sl], st.astype(BF16))
        o_ref[0, :, sl] = o.astype(o_ref.dtype)
        st_ref[h] = st * state_decay[:, sl] + _dot_tn(vh, k_state[:, sl])

    @pl.when(c_idx == pl.num_programs(1) - 1)
    def _():
        for h in range(HG_HEADS):
            sfin_ref[0, h] = st_ref[h].T


def _hgrn_rec(q, fx, v, lb_logits, s0, layer_j):
    b, t, d = q.shape
    c = REC_CHUNK
    blk = pl.BlockSpec((1, c, d), lambda i, j: (i, j, 0))
    st_blk = pl.BlockSpec((1, HG_HEADS, HG_DK, HG_DK), lambda i, j: (i, 0, 0, 0))
    emat = _range_matrices(c)
    in_specs = [blk, blk, blk, _resident(lb_logits.shape), _resident(emat.shape)]
    args = [q, fx, v, lb_logits, emat]
    if s0 is not None:
        in_specs.append(st_blk)
        args.append(s0)
    return pl.pallas_call(
        functools.partial(_hgrn_rec_kernel, layer_j, s0 is not None),
        grid=(b, t // c),
        in_specs=in_specs,
        out_specs=[blk, st_blk],
        out_shape=[jax.ShapeDtypeStruct((b, t, d), BF16),
                   jax.ShapeDtypeStruct((b, HG_HEADS, HG_DK, HG_DK), F32)],
        scratch_shapes=[pltpu.VMEM((HG_HEADS, HG_DK, HG_DK), F32)],
        compiler_params=_params("parallel", "arbitrary"),
        name="hgrn_rec",
    )(*args)


def _masked_queries(q_refs):
    tq = q_refs[0].shape[1]
    lane = lax.broadcasted_iota(jnp.int32, (tq, LANES), 1)
    zero = jnp.zeros((tq, LANES), BF16)
    return [[jnp.where(lane < DA_HEAD_DIM, qr[0], zero), jnp.where(lane >= DA_HEAD_DIM, qr[0], zero)]
            for qr in q_refs]


def _lambda_scalar(lam_ref, lam_init):
    lv = lam_ref[...]
    return (jnp.exp(jnp.sum(lv[0:1] * lv[1:2], axis=1, keepdims=True))
            - jnp.exp(jnp.sum(lv[2:3] * lv[3:4], axis=1, keepdims=True)) + lam_init)


def _finish_transposed(lam_ref, subw_ref, lam_init, l_sc, acc_sc, o_ref):
    lam = _lambda_scalar(lam_ref, lam_init)
    for r in range(2):
        o = acc_sc[2 * r] / l_sc[2 * r] - lam * (acc_sc[2 * r + 1] / l_sc[2 * r + 1])
        ms = jnp.mean(o * o, axis=0, keepdims=True)
        o = o * lax.rsqrt(ms + EPS) * subw_ref[...] * (1.0 - lam_init)
        o_ref[0, :, r * LANES:(r + 1) * LANES] = o.T.astype(o_ref.dtype)


def _attn_fixed_kernel(lam_init, slopes_ref, shift_ref, lam_ref, subw_ref, q0_ref, q1_ref,
                       k0_ref, k1_ref, v_ref, o_ref, l_sc, acc_sc, bias_sc):
    j = pl.program_id(1)
    i = pl.program_id(2)
    t = q0_ref.shape[1]

    @pl.when(i == 0)
    def _():
        row = lax.broadcasted_iota(jnp.int32, (t, t), 0)
        col = lax.broadcasted_iota(jnp.int32, (t, t), 1)
        rel = col - row
        visible = (row // MASK_CHUNK) <= (col // MASK_CHUNK)
        for r in range(2):
            slope = slopes_ref[2 * j + r]
            bias_sc[0, :, r * t:(r + 1) * t] = -slope * (t + rel).astype(F32) - shift_ref[0]
            bias_sc[1, :, r * t:(r + 1) * t] = jnp.where(
                visible, -slope * jnp.abs(rel).astype(F32) - shift_ref[0], NEG_INF)

    qm = _masked_queries((q0_ref, q1_ref))
    q_pair = [jnp.concatenate(qm[mp], axis=0) for mp in range(2)]
    k_refs = (k0_ref, k1_ref)
    l_sc[...] = jnp.zeros(l_sc.shape, F32)
    acc_sc[...] = jnp.zeros(acc_sc.shape, F32)

    def tile(kt_idx, diagonal):
        k_start = pl.multiple_of(kt_idx * t, t)
        if not diagonal:
            far = ((i - kt_idx - 1) * t).astype(F32)
            weight = [jnp.exp2(jnp.full((1, t), -slopes_ref[2 * j + r] * far, F32)) for r in range(2)]
        p = []
        for mp in range(2):
            s = _dot_nt(k_refs[mp][0, pl.ds(k_start, t), :], q_pair[mp])
            pm = jnp.exp2(s + bias_sc[int(diagonal)])
            l_new = jnp.sum(pm, axis=0, keepdims=True)
            if not diagonal:
                l_new = l_new * jnp.concatenate(weight, axis=1)
            l_sc[mp] = l_sc[mp] + l_new
            p.append(pm.astype(BF16))
        for r in range(2):
            vt = v_ref[0, pl.ds(k_start, t), r * LANES:(r + 1) * LANES]
            pr = jnp.concatenate([p[0][:, r * t:(r + 1) * t], p[1][:, r * t:(r + 1) * t]], axis=1)
            acc_new = _dot_tn(vt, pr)
            if not diagonal:
                acc_new = acc_new * jnp.concatenate([weight[r], weight[r]], axis=1)
            acc_sc[r] = acc_sc[r] + acc_new

    def below(kt_idx, carry):
        tile(kt_idx, False)
        return carry

    lax.fori_loop(0, i, below, 0)
    tile(i, True)

    lam = _lambda_scalar(lam_ref, lam_init)
    for r in range(2):
        num = acc_sc[r]
        o = (num[:, :t] / l_sc[0][:, r * t:(r + 1) * t]
             - lam * (num[:, t:] / l_sc[1][:, r * t:(r + 1) * t]))
        ms = jnp.mean(o * o, axis=0, keepdims=True)
        o = o * lax.rsqrt(ms + EPS) * subw_ref[...] * (1.0 - lam_init)
        o_ref[0, :, r * LANES:(r + 1) * LANES] = o.T.astype(o_ref.dtype)


def _attn_online_kernel(lam_init, slopes_ref, lam_ref, subw_ref, q0_ref, q1_ref, k0_ref, k1_ref,
                        v_ref, o_ref, m_sc, l_sc, acc_sc):
    j = pl.program_id(1)
    i = pl.program_id(2)
    t = q0_ref.shape[1]
    qm = _masked_queries((q0_ref, q1_ref))
    k_refs = (k0_ref, k1_ref)
    m_sc[...] = jnp.full(m_sc.shape, -jnp.inf, F32)
    l_sc[...] = jnp.zeros(l_sc.shape, F32)
    acc_sc[...] = jnp.zeros(acc_sc.shape, F32)

    def tile(kt_idx, diagonal):
        k_start = pl.multiple_of(kt_idx * t, t)
        row = lax.broadcasted_iota(jnp.int32, (t, t), 0)
        col = lax.broadcasted_iota(jnp.int32, (t, t), 1)
        dist = jnp.abs((i - kt_idx) * t + (col - row)).astype(F32)
        if diagonal:
            visible = (row // MASK_CHUNK) <= (col // MASK_CHUNK)
        for r in range(2):
            bias = -slopes_ref[2 * j + r] * dist
            vt = v_ref[0, pl.ds(k_start, t), r * LANES:(r + 1) * LANES]
            for mp in range(2):
                idx = 2 * r + mp
                kt = k_refs[mp][0, pl.ds(k_start, t), :]
                s = _dot_nt(kt, qm[mp][r]) + bias
                if diagonal:
                    s = jnp.where(visible, s, NEG_INF)
                m_prev = m_sc[idx]
                m_new = jnp.maximum(m_prev, jnp.max(s, axis=0, keepdims=True))
                alpha = jnp.exp2(m_prev - m_new)
                p = jnp.exp2(s - m_new)
                l_sc[idx] = alpha * l_sc[idx] + jnp.sum(p, axis=0, keepdims=True)
                acc_sc[idx] = alpha * acc_sc[idx] + _dot_tn(vt, p.astype(BF16))
                m_sc[idx] = m_new

    def below(kt_idx, carry):
        tile(kt_idx, False)
        return carry

    lax.fori_loop(0, i, below, 0)
    tile(i, True)
    _finish_transposed(lam_ref, subw_ref, lam_init, l_sc, acc_sc, o_ref)


def _prompt_attention(q, k, v, lam_vec, subln_w, slopes2, score_bound2, lam_init):
    b, t, d = q.shape
    tile = ATTN_TILE
    half = d // (2 * LANES)
    q_spec = lambda off: pl.BlockSpec((1, tile, LANES), lambda bi, j, i: (bi, i, j + off))
    k_spec = lambda off: pl.BlockSpec((1, t, LANES), lambda bi, j, i: (bi, 0, j + off))
    v_spec = pl.BlockSpec((1, t, 2 * LANES), lambda bi, j, i: (bi, 0, j))
    o_spec = pl.BlockSpec((1, tile, 2 * LANES), lambda bi, j, i: (bi, i, j))
    smem = pl.BlockSpec(memory_space=pltpu.SMEM)
    lam_spec = pl.BlockSpec(lam_vec.shape, lambda bi, j, i: (0, 0))
    subw_spec = pl.BlockSpec((2 * DA_HEAD_DIM, 1), lambda bi, j, i: (0, 0))
    data_specs = [q_spec(0), q_spec(half), k_spec(0), k_spec(half), v_spec]
    subw = subln_w.reshape(2 * DA_HEAD_DIM, 1)
    stat = pltpu.VMEM((4, 1, tile), F32)
    acc = pltpu.VMEM((4, 2 * DA_HEAD_DIM, tile), F32)
    common = dict(grid=(b, half, t // tile), out_specs=o_spec,
                  out_shape=jax.ShapeDtypeStruct((b, t, d), BF16),
                  compiler_params=_params("parallel", "parallel", "arbitrary"))

    def fixed(shift):
        return pl.pallas_call(
            functools.partial(_attn_fixed_kernel, lam_init),
            in_specs=[smem, smem, lam_spec, subw_spec] + data_specs,
            scratch_shapes=[pltpu.VMEM((2, 1, 2 * tile), F32),
                            pltpu.VMEM((2, 2 * DA_HEAD_DIM, 2 * tile), F32),
                            pltpu.VMEM((2, tile, 2 * tile), F32)],
            name="diff_attn_fixed", **common,
        )(slopes2, shift, lam_vec, subw, q, q, k, k, v)

    def online(_):
        return pl.pallas_call(
            functools.partial(_attn_online_kernel, lam_init),
            in_specs=[smem, lam_spec, subw_spec] + data_specs,
            scratch_shapes=[stat, stat, acc],
            name="diff_attn_online", **common,
        )(slopes2, lam_vec, subw, q, q, k, k, v)

    return lax.cond(score_bound2[0] <= FIXED_SHIFT_MAX_SCORE * LOG2E, fixed, online, score_bound2)


def _sample_attn_kernel(lam_init, past_visible, slopes_ref, lam_ref, subw_ref, q0_ref, q1_ref,
                        kc0_ref, kc1_ref, kn0_ref, kn1_ref, vc_ref, vn_ref, o_ref):
    j = pl.program_id(1)
    tq = q0_ref.shape[1]
    past = kc0_ref.shape[2]
    qm = _masked_queries((q0_ref, q1_ref))
    kc = [ref[0].astype(BF16) for ref in (kc0_ref, kc1_ref)]
    kn = [ref[0] for ref in (kn0_ref, kn1_ref)]
    lam = _lambda_scalar(lam_ref, lam_init)

    row_c = lax.broadcasted_iota(jnp.int32, (tq, past), 0)
    col_c = lax.broadcasted_iota(jnp.int32, (tq, past), 1)
    dist_c = (past + row_c - col_c).astype(F32)
    row_n = lax.broadcasted_iota(jnp.int32, (tq, tq), 0)
    col_n = lax.broadcasted_iota(jnp.int32, (tq, tq), 1)
    dist_n = jnp.abs(row_n - col_n).astype(F32)
    if not past_visible:
        vis_c = (col_c // MASK_CHUNK) <= ((past + row_c) // MASK_CHUNK)
        vis_n = ((past + col_n) // MASK_CHUNK) <= ((past + row_n) // MASK_CHUNK)

    for r in range(2):
        slope = slopes_ref[2 * j + r]
        vc = vc_ref[0, :, r * LANES:(r + 1) * LANES].astype(BF16)
        vn = vn_ref[0, :, r * LANES:(r + 1) * LANES]
        outs = []
        for mp in range(2):
            s_c = _dot(qm[mp][r], kc[mp]) - slope * dist_c
            s_n = _dot_nt(qm[mp][r], kn[mp]) - slope * dist_n
            if not past_visible:
                s_c = jnp.where(vis_c, s_c, NEG_INF)
                s_n = jnp.where(vis_n, s_n, NEG_INF)
            m = jnp.maximum(jnp.max(s_c, axis=1, keepdims=True), jnp.max(s_n, axis=1, keepdims=True))
            p_c = jnp.exp2(s_c - m)
            p_n = jnp.exp2(s_n - m)
            l = jnp.sum(p_c, axis=1, keepdims=True) + jnp.sum(p_n, axis=1, keepdims=True)
            acc = _dot(p_c.astype(BF16), vc) + _dot(p_n.astype(BF16), vn)
            outs.append(acc / l)
        o = outs[0] - lam * outs[1]
        ms = jnp.mean(o * o, axis=1, keepdims=True)
        o = o * lax.rsqrt(ms + EPS) * subw_ref[...] * (1.0 - lam_init)
        o_ref[0, :, r * LANES:(r + 1) * LANES] = o.astype(o_ref.dtype)


def _sample_attention(q, k_new, v_new, k_cache_t, v_cache, lam_vec, subln_w, slopes2, lam_init):
    b, t_q, d = q.shape
    past = k_cache_t.shape[2]
    half = d // (2 * LANES)
    past_visible = past % MASK_CHUNK == 0 and t_q <= MASK_CHUNK
    qk_spec = lambda off: pl.BlockSpec((1, t_q, LANES), lambda bi, j: (bi, 0, j + off))
    kc_spec = lambda off: pl.BlockSpec((1, LANES, past), lambda bi, j: (bi, j + off, 0))
    return pl.pallas_call(
        functools.partial(_sample_attn_kernel, lam_init, past_visible),
        grid=(b, half),
        in_specs=[pl.BlockSpec(memory_space=pltpu.SMEM),
                  pl.BlockSpec(lam_vec.shape, lambda bi, j: (0, 0)),
                  pl.BlockSpec((1, 2 * DA_HEAD_DIM), lambda bi, j: (0, 0)),
                  qk_spec(0), qk_spec(half), kc_spec(0), kc_spec(half), qk_spec(0), qk_spec(half),
                  pl.BlockSpec((1, past, 2 * LANES), lambda bi, j: (bi, 0, j)),
                  pl.BlockSpec((1, t_q, 2 * LANES), lambda bi, j: (bi, 0, j))],
        out_specs=pl.BlockSpec((1, t_q, 2 * LANES), lambda bi, j: (bi, 0, j)),
        out_shape=jax.ShapeDtypeStruct((b, t_q, d), BF16),
        compiler_params=_params("parallel", "parallel"),
        name="diff_attn_sample",
    )(slopes2, lam_vec, subln_w.reshape(1, 2 * DA_HEAD_DIM), q, q, k_cache_t, k_cache_t,
      k_new, k_new, v_cache, v_new)


def _row_tile(m):
    return 512 if m % 512 == 0 else 256


def _hgrn_layer(x, s0, nw, w_in_bf, lb_logits, onw, w_out_bf, layer_j):
    b, t, d = x.shape
    x2 = x.reshape(b * t, d)
    tm = _row_tile(b * t)
    q, fx, iv, z = _inproj_hgrn(x2, nw.reshape(1, d), w_in_bf, tm)
    o, s_new = _hgrn_rec(q.reshape(b, t, d), fx.reshape(b, t, d), iv.reshape(b, t, d),
                         lb_logits, s0, layer_j)
    y = _outproj(o.reshape(b * t, d), z, x2, w_out_bf, onw.reshape(1, d), True, tm)
    return y.reshape(b, t, d), s_new


def _attn_layer(x, cache, nw, w_in_bf, qn_w, kn_w, lam_vec, subln_w, w_out_bf, layer_idx,
                layer_j, n_layers, k_stack, v_stack):
    b, t, d = x.shape
    x2 = x.reshape(b * t, d)
    groups = d // DA_HEAD_DIM
    gmat = jnp.kron(jnp.eye(LANES // DA_HEAD_DIM, dtype=F32),
                    jnp.full((DA_HEAD_DIM, DA_HEAD_DIM), 1.0 / DA_HEAD_DIM, F32)).astype(BF16)
    prompt = cache is None
    tm = 256 if t % 256 == 0 else t
    q, kb, vb, z, k_stack, v_stack = _inproj_attn(
        x2, t, nw.reshape(1, d), w_in_bf, jnp.tile(qn_w, groups).reshape(1, d),
        jnp.tile(kn_w, groups).reshape(1, d), gmat, layer_j, n_layers, k_stack, v_stack, prompt, tm)
    lam_init = 0.8 - 0.6 * math.exp(-0.3 * layer_idx)
    slopes2 = jnp.exp2(-8.0 * jnp.arange(1, DA_HEADS + 1, dtype=F32) / DA_HEADS) * LOG2E
    q, kb, vb = (a.reshape(b, t, d) for a in (q, kb, vb))
    if prompt:
        bound2 = (DA_HEAD_DIM ** 0.5 * LOG2E * 1.02) * jnp.max(jnp.abs(qn_w)) * jnp.max(jnp.abs(kn_w))
        o = _prompt_attention(q, kb, vb, lam_vec, subln_w, slopes2, bound2.reshape(1), lam_init)
    else:
        k_cache_t, v_cache = cache
        o = _sample_attention(q, kb, vb, k_cache_t, v_cache, lam_vec, subln_w, slopes2, lam_init)
    y = _outproj(o.reshape(b * t, d), z, x2, w_out_bf, jnp.ones((1, d), F32), False,
                 _row_tile(b * t))
    return y.reshape(b, t, d), k_stack, v_stack


def kernel(x_prompt, x_sample, cache_k, cache_v, state_hgrn, norm_w, hgrn_w_in, hgrn_lb_logits,
           hgrn_onorm_w, hgrn_w_out, attn_w_in, attn_q_norm, attn_k_norm, attn_lambda, attn_subln,
           attn_w_out):
    depth = norm_w.shape[0]
    n_attn = cache_k.shape[0]
    bp, tp, d = x_prompt.shape
    bs, ts, _ = x_sample.shape
    past = cache_k.shape[2]
    cache_k_t = jnp.transpose(cache_k, (0, 1, 3, 4, 5, 2)).reshape(n_attn, bs, d, past)
    cache_v2 = cache_v.reshape(n_attn, bs, past, d)
    yp, ys = x_prompt, x_sample
    kp = vp = ks_ = vs_ = None
    sp, ss = [], []
    for l in range(depth):
        j = l // N_MIXERS
        if l % N_MIXERS == 0:
            w_in = hgrn_w_in[j].astype(BF16)
            w_out = hgrn_w_out[j].astype(BF16)
            yp, s_p = _hgrn_layer(yp, None, norm_w[l], w_in, hgrn_lb_logits, hgrn_onorm_w[j],
                                  w_out, j)
            ys, s_s = _hgrn_layer(ys, state_hgrn[j], norm_w[l], w_in, hgrn_lb_logits,
                                  hgrn_onorm_w[j], w_out, j)
            sp.append(s_p)
            ss.append(s_s)
        else:
            w_in = attn_w_in[j].astype(BF16)
            w_out = attn_w_out[j].astype(BF16)
            yp, kp, vp = _attn_layer(yp, None, norm_w[l], w_in, attn_q_norm[j], attn_k_norm[j],
                                     attn_lambda[j], attn_subln[j], w_out, l, j, n_attn, kp, vp)
            ys, ks_, vs_ = _attn_layer(ys, (cache_k_t[j], cache_v2[j]), norm_w[l], w_in,
                                       attn_q_norm[j], attn_k_norm[j], attn_lambda[j],
                                       attn_subln[j], w_out, l, j, n_attn, ks_, vs_)
    new_k_prompt = jnp.transpose(kp.reshape(n_attn, bp, 2, DA_HEADS, DA_HEAD_DIM, tp),
                                 (0, 1, 5, 2, 3, 4))
    return (yp, ys, new_k_prompt,
            vp.reshape(n_attn, bp, tp, DA_HEADS, 2 * DA_HEAD_DIM),
            ks_.reshape(n_attn, bs, ts, 2, DA_HEADS, DA_HEAD_DIM),
            vs_.reshape(n_attn, bs, ts, DA_HEADS, 2 * DA_HEAD_DIM),
            jnp.stack(sp), jnp.stack(ss))
```

```python
import functools
import math

import jax
import jax.numpy as jnp
from jax import lax
from jax.experimental import pallas as pl
from jax.experimental.pallas import tpu as pltpu

F32 = jnp.float32
BF16 = jnp.bfloat16

EPS = 1e-6
NEG_INF = -1e30
LOG2E = 1.4426950408889634
LANES = 128
HG_HEADS = 8
HG_DK = 128
DA_HEADS = 8
DA_HEAD_DIM = 64
MASK_CHUNK = 64
N_MIXERS = 2

REC_CHUNK = 64
REC_LEVELS = (1, 2, 4, 8, 16, 32)
ATTN_TILE = 256
FIXED_SHIFT_MAX_SCORE = 40.0
VMEM_LIMIT = 56 * 1024 * 1024


def _dot(a, b):
    return jnp.dot(a, b, preferred_element_type=F32)


def _dot_nt(a, b):
    return lax.dot_general(a, b, (((1,), (1,)), ((), ())), preferred_element_type=F32)


def _dot_tn(a, b):
    return lax.dot_general(a, b, (((0,), (0,)), ((), ())), preferred_element_type=F32)


def _rms_rows(x, w):
    ms = jnp.mean(x * x, axis=-1, keepdims=True)
    return x * lax.rsqrt(ms + EPS) * w


def _resident(shape):
    nd = len(shape)
    return pl.BlockSpec(shape, lambda *_: (0,) * nd, pipeline_mode=pl.Buffered(1))


def _params(*semantics):
    return pltpu.CompilerParams(dimension_semantics=semantics, vmem_limit_bytes=VMEM_LIMIT)


def _inproj_hgrn_kernel(x_ref, nw_ref, w_ref, q_ref, fx_ref, i_ref, z_ref):
    d = x_ref.shape[1]
    h = _rms_rows(x_ref[...], nw_ref[...]).astype(BF16)
    for c, o_ref in enumerate((q_ref, fx_ref, i_ref, z_ref)):
        o_ref[...] = _dot(h, w_ref[:, c * d:(c + 1) * d]).astype(o_ref.dtype)


def _inproj_hgrn(x2, nw, w_bf, tm):
    m, d = x2.shape
    row = pl.BlockSpec((tm, d), lambda i: (i, 0))
    return pl.pallas_call(
        _inproj_hgrn_kernel,
        grid=(m // tm,),
        in_specs=[row, _resident((1, d)), _resident((d, 4 * d))],
        out_specs=[row, row, row, row],
        out_shape=[jax.ShapeDtypeStruct((m, d), BF16), jax.ShapeDtypeStruct((m, d), F32),
                   jax.ShapeDtypeStruct((m, d), BF16), jax.ShapeDtypeStruct((m, d), BF16)],
        compiler_params=_params("parallel"),
        name="inproj_hgrn",
    )(x2, nw, w_bf)


def _inproj_attn_kernel(k_transposed, n_alias, x_ref, nw_ref, w_ref, qnw_ref, knw_ref, g_ref, *rest):
    q_ref, kb_ref, vb_ref, z_ref, ks_ref, vs_ref = rest[n_alias:]
    d = x_ref.shape[1]
    h = _rms_rows(x_ref[...], nw_ref[...]).astype(BF16)
    gmat = g_ref[...]

    def group_norm(y, w):
        cols = []
        for t in range(d // LANES):
            yt = y[:, t * LANES:(t + 1) * LANES]
            ms = _dot((yt * yt).astype(BF16), gmat)
            cols.append(yt * lax.rsqrt(ms + EPS))
        return jnp.concatenate(cols, axis=1) * w

    q = group_norm(_dot(h, w_ref[:, 0:d]), qnw_ref[...])
    q_ref[...] = (q * (DA_HEAD_DIM ** -0.5 * LOG2E)).astype(BF16)
    k = group_norm(_dot(h, w_ref[:, d:2 * d]), knw_ref[...])
    kb_ref[...] = k.astype(BF16)
    ks_ref[0, 0] = k.T if k_transposed else k
    v = _dot(h, w_ref[:, 2 * d:3 * d])
    vs_ref[0] = v
    vb_ref[...] = v.astype(BF16)
    z_ref[...] = _dot(h, w_ref[:, 3 * d:4 * d]).astype(BF16)


def _inproj_attn(x2, seq, nw, w_bf, qnw, knw, gmat, layer_j, n_layers, k_stack, v_stack,
                 k_transposed, tm):
    m, d = x2.shape
    batch = m // seq
    tiles_per_seq = seq // tm
    row = pl.BlockSpec((tm, d), lambda i: (i, 0))
    if k_transposed:
        ks_shape = (n_layers, batch, d, seq)
        ks_spec = pl.BlockSpec((1, 1, d, tm),
                               lambda i: (layer_j, i // tiles_per_seq, 0, i % tiles_per_seq))
    else:
        ks_shape = (n_layers, batch, seq, d)
        ks_spec = pl.BlockSpec((1, 1, tm, d),
                               lambda i: (layer_j, i // tiles_per_seq, i % tiles_per_seq, 0))
    vs_spec = pl.BlockSpec((1, tm, d), lambda i: (layer_j, i, 0))
    in_specs = [row, _resident((1, d)), _resident((d, 4 * d)), _resident((1, d)),
                _resident((1, d)), _resident((LANES, LANES))]
    args = [x2, nw, w_bf, qnw, knw, gmat]
    aliases = {}
    if k_stack is not None:
        in_specs += [pl.BlockSpec(memory_space=pl.ANY)] * 2
        aliases = {len(args): 4, len(args) + 1: 5}
        args += [k_stack, v_stack]
    shp = lambda dt: jax.ShapeDtypeStruct((m, d), dt)
    return pl.pallas_call(
        functools.partial(_inproj_attn_kernel, k_transposed, len(aliases)),
        grid=(m // tm,),
        in_specs=in_specs,
        out_specs=[row, row, row, row, ks_spec, vs_spec],
        out_shape=[shp(BF16), shp(BF16), shp(BF16), shp(BF16),
                   jax.ShapeDtypeStruct(ks_shape, F32),
                   jax.ShapeDtypeStruct((n_layers, m, d), F32)],
        input_output_aliases=aliases,
        compiler_params=_params("parallel"),
        name="inproj_attn",
    )(*args)


def _outproj_kernel(full_norm, o_ref, z_ref, x_ref, w_ref, ow_ref, y_ref):
    o = o_ref[...].astype(F32)
    if full_norm:
        o = _rms_rows(o, ow_ref[...])
    z = z_ref[...].astype(F32)
    gated = (o * (z * jax.nn.sigmoid(z))).astype(BF16)
    y_ref[...] = x_ref[...] + _dot(gated, w_ref[...])


def _outproj(o2, z2, x2, w_bf, ow, full_norm, tm):
    m, d = x2.shape
    row = pl.BlockSpec((tm, d), lambda i: (i, 0))
    return pl.pallas_call(
        functools.partial(_outproj_kernel, full_norm),
        grid=(m // tm,),
        in_specs=[row, row, row, _resident((d, d)), _resident((1, d))],
        out_specs=row,
        out_shape=jax.ShapeDtypeStruct((m, d), F32),
        compiler_params=_params("parallel"),
        name="outproj",
    )(o2, z2, x2, w_bf, ow)


def _range_matrices(c):
    t = jnp.arange(c)[:, None]
    s = jnp.arange(c)[None, :]
    blocks = [s <= t]
    for w in REC_LEVELS:
        ref = (t // (2 * w)) * (2 * w) + w - 1
        upper = (t & w) != 0
        blocks.append(jnp.where(upper, (s > ref) & (s <= t), (s > t) & (s <= ref)))
    blocks.append(s > t)
    e = jnp.concatenate(blocks, axis=0).astype(BF16)
    return jnp.concatenate([e, e, e], axis=1)


def _hgrn_rec_kernel(layer_j, has_s0, q_ref, fx_ref, v_ref, lbl_ref, emat_ref, *rest):
    if has_s0:
        s0_ref, o_ref, sfin_ref, st_ref = rest
    else:
        o_ref, sfin_ref, st_ref = rest
    c_idx = pl.program_id(1)
    c = q_ref.shape[1]
    d = q_ref.shape[2]

    @pl.when(c_idx == 0)
    def _():
        for h in range(HG_HEADS):
            if has_s0:
                st_ref[h] = s0_ref[0, h].T
            else:
                st_ref[h] = jnp.zeros((HG_DK, HG_DK), F32)

    lg = lbl_ref[...]
    ex = jnp.exp(lg - jnp.max(lg, axis=0, keepdims=True))
    p = ex / jnp.sum(ex, axis=0, keepdims=True)
    cs = p[0:1]
    for r in range(1, layer_j + 1):
        cs = cs + p[r:r + 1]
    lb = cs - p[0:1]

    f = lb + (1.0 - lb) * jax.nn.sigmoid(fx_ref[0])
    g = jnp.log(f)
    kk = 1.0 - f
    g1 = g.astype(BF16)
    r1 = g - g1.astype(F32)
    g2 = r1.astype(BF16)
    g3 = (r1 - g2.astype(F32)).astype(BF16)
    decay = jnp.exp(_dot(emat_ref[...], jnp.concatenate([g1, g2, g3], axis=0)))

    q = q_ref[0].astype(F32) * (HG_DK ** -0.5)
    row = lax.broadcasted_iota(jnp.int32, (c, d), 0)
    q_state = (q * decay[0:c]).astype(BF16)
    k_state = (kk * decay[(len(REC_LEVELS) + 1) * c:]).astype(BF16)
    state_decay = decay[c - 1:c]
    q_lv, k_lv = [q.astype(BF16)], [kk.astype(BF16)]
    for li, w in enumerate(REC_LEVELS):
        x = decay[(li + 1) * c:(li + 2) * c]
        upper = (row & w) != 0
        q_lv.append(jnp.where(upper, q * x, 0.0).astype(BF16))
        k_lv.append(jnp.where(upper, 0.0, kk * x).astype(BF16))

    tt = lax.broadcasted_iota(jnp.int32, (c, c), 0)
    ss = lax.broadcasted_iota(jnp.int32, (c, c), 1)
    masks = [tt == ss] + [(tt // (2 * w)) == (ss // (2 * w)) for w in REC_LEVELS]

    for h in range(HG_HEADS):
        sl = slice(h * HG_DK, (h + 1) * HG_DK)
        att = jnp.zeros((c, c), F32)
        for lv in range(len(masks)):
            att = att + jnp.where(masks[lv], _dot_nt(q_lv[lv][:, sl], k_lv[lv][:, sl]), 0.0)
        vh = v_ref[0, :, sl]
        st = st_ref[h]
        o = _dot(att.astype(BF16), vh) + _dot_nt(q_state[:, sl], st.astype(BF16))
        o_ref[0, :, sl] = o.astype(o_ref.dtype)
        st_ref[h] = st * state_decay[:, sl] + _dot_tn(vh, k_state[:, sl])

    @pl.when(c_idx == pl.num_programs(1) - 1)
    def _():
        for h in range(HG_HEADS):
            sfin_ref[0, h] = st_ref[h].T


def _hgrn_rec(q, fx, v, lb_logits, s0, layer_j):
    b, t, d = q.shape
    c = REC_CHUNK
    blk = pl.BlockSpec((1, c, d), lambda i, j: (i, j, 0))
    st_blk = pl.BlockSpec((1, HG_HEADS, HG_DK, HG_DK), lambda i, j: (i, 0, 0, 0))
    emat = _range_matrices(c)
    in_specs = [blk, blk, blk, _resident(lb_logits.shape), _resident(emat.shape)]
    args = [q, fx, v, lb_logits, emat]
    if s0 is not None:
        in_specs.append(st_blk)
        args.append(s0)
    return pl.pallas_call(
        functools.partial(_hgrn_rec_kernel, layer_j, s0 is not None),
        grid=(b, t // c),
        in_specs=in_specs,
        out_specs=[blk, st_blk],
        out_shape=[jax.ShapeDtypeStruct((b, t, d), BF16),
                   jax.ShapeDtypeStruct((b, HG_HEADS, HG_DK, HG_DK), F32)],
        scratch_shapes=[pltpu.VMEM((HG_HEADS, HG_DK, HG_DK), F32)],
        compiler_params=_params("parallel", "arbitrary"),
        name="hgrn_rec",
    )(*args)


def _masked_queries(q_refs):
    tq = q_refs[0].shape[1]
    lane = lax.broadcasted_iota(jnp.int32, (tq, LANES), 1)
    zero = jnp.zeros((tq, LANES), BF16)
    return [[jnp.where(lane < DA_HEAD_DIM, qr[0], zero), jnp.where(lane >= DA_HEAD_DIM, qr[0], zero)]
            for qr in q_refs]


def _lambda_scalar(lam_ref, lam_init):
    lv = lam_ref[...]
    return (jnp.exp(jnp.sum(lv[0:1] * lv[1:2], axis=1, keepdims=True))
            - jnp.exp(jnp.sum(lv[2:3] * lv[3:4], axis=1, keepdims=True)) + lam_init)


def _finish_transposed(lam_ref, subw_ref, lam_init, l_sc, acc_sc, o_ref):
    lam = _lambda_scalar(lam_ref, lam_init)
    for r in range(2):
        o = acc_sc[2 * r] / l_sc[2 * r] - lam * (acc_sc[2 * r + 1] / l_sc[2 * r + 1])
        ms = jnp.mean(o * o, axis=0, keepdims=True)
        o = o * lax.rsqrt(ms + EPS) * subw_ref[...] * (1.0 - lam_init)
        o_ref[0, :, r * LANES:(r + 1) * LANES] = o.T.astype(o_ref.dtype)


def _attn_fixed_kernel(lam_init, slopes_ref, shift_ref, lam_ref, subw_ref, q0_ref, q1_ref,
                       k0_ref, k1_ref, v_ref, o_ref, l_sc, acc_sc, below_sc, last_sc):
    j = pl.program_id(1)
    i = pl.program_id(2)
    t = q0_ref.shape[1]

    @pl.when(i == 0)
    def _():
        row = lax.broadcasted_iota(jnp.int32, (2 * t, t), 0)
        col = lax.broadcasted_iota(jnp.int32, (2 * t, t), 1)
        below_dist = (3 * t + col - row).astype(F32)
        last_dist = jnp.abs(t + col - row).astype(F32)
        visible = (row < t) | (((row - t) // MASK_CHUNK) <= (col // MASK_CHUNK))
        for r in range(2):
            slope = slopes_ref[2 * j + r]
            below_sc[:, r * t:(r + 1) * t] = -slope * below_dist - shift_ref[0]
            last_sc[:, r * t:(r + 1) * t] = jnp.where(
                visible, -slope * last_dist - shift_ref[0], NEG_INF)

    qm = _masked_queries((q0_ref, q1_ref))
    q_pair = [jnp.concatenate(qm[mp], axis=0) for mp in range(2)]
    k_refs = (k0_ref, k1_ref)
    l_sc[...] = jnp.zeros(l_sc.shape, F32)
    acc_sc[...] = jnp.zeros(acc_sc.shape, F32)

    def block(k_start, rows, bias_ref, bias_row0, far):
        s = [_dot_nt(k_refs[mp][0, pl.ds(k_start, rows), :], q_pair[mp]) for mp in range(2)]
        for r in range(2):
            cols = slice(r * t, (r + 1) * t)
            bias = bias_ref[bias_row0:bias_row0 + rows, cols]
            p = [jnp.exp2(s[mp][:, cols] + bias) for mp in range(2)]
            l_new = [jnp.sum(pm, axis=0, keepdims=True) for pm in p]
            vt = v_ref[0, pl.ds(k_start, rows), r * LANES:(r + 1) * LANES]
            acc_new = _dot_tn(vt, jnp.concatenate([pm.astype(BF16) for pm in p], axis=1))
            if far is not None:
                weight = jnp.exp2(jnp.full((1, t), -slopes_ref[2 * j + r] * far, F32))
                l_new = [ln * weight for ln in l_new]
                acc_new = acc_new * jnp.concatenate([weight, weight], axis=1)
            for mp in range(2):
                l_sc[mp, :, cols] = l_sc[mp, :, cols] + l_new[mp]
            acc_sc[r] = acc_sc[r] + acc_new

    def two_tiles(pair_idx, carry):
        k_start = pl.multiple_of(pair_idx * (2 * t), 2 * t)
        block(k_start, 2 * t, below_sc, 0, ((i - 2 * pair_idx - 3) * t).astype(F32))
        return carry

    lax.fori_loop(0, jnp.maximum(i - 1, 0) // 2, two_tiles, 0)

    @pl.when((i >= 2) & (i % 2 == 0))
    def _():
        block(pl.multiple_of((i - 2) * t, t), t, below_sc, t, None)

    @pl.when(i >= 1)
    def _():
        block(pl.multiple_of((i - 1) * t, t), 2 * t, last_sc, 0, None)

    @pl.when(i == 0)
    def _():
        block(0, t, last_sc, t, None)

    lam = _lambda_scalar(lam_ref, lam_init)
    for r in range(2):
        num = acc_sc[r]
        o = (num[:, :t] / l_sc[0][:, r * t:(r + 1) * t]
             - lam * (num[:, t:] / l_sc[1][:, r * t:(r + 1) * t]))
        ms = jnp.mean(o * o, axis=0, keepdims=True)
        o = o * lax.rsqrt(ms + EPS) * subw_ref[...] * (1.0 - lam_init)
        o_ref[0, :, r * LANES:(r + 1) * LANES] = o.T.astype(o_ref.dtype)


def _attn_online_kernel(lam_init, slopes_ref, lam_ref, subw_ref, q0_ref, q1_ref, k0_ref, k1_ref,
                        v_ref, o_ref, m_sc, l_sc, acc_sc):
    j = pl.program_id(1)
    i = pl.program_id(2)
    t = q0_ref.shape[1]
    qm = _masked_queries((q0_ref, q1_ref))
    k_refs = (k0_ref, k1_ref)
    m_sc[...] = jnp.full(m_sc.shape, -jnp.inf, F32)
    l_sc[...] = jnp.zeros(l_sc.shape, F32)
    acc_sc[...] = jnp.zeros(acc_sc.shape, F32)

    def tile(kt_idx, diagonal):
        k_start = pl.multiple_of(kt_idx * t, t)
        row = lax.broadcasted_iota(jnp.int32, (t, t), 0)
        col = lax.broadcasted_iota(jnp.int32, (t, t), 1)
        dist = jnp.abs((i - kt_idx) * t + (col - row)).astype(F32)
        if diagonal:
            visible = (row // MASK_CHUNK) <= (col // MASK_CHUNK)
        for r in range(2):
            bias = -slopes_ref[2 * j + r] * dist
            vt = v_ref[0, pl.ds(k_start, t), r * LANES:(r + 1) * LANES]
            for mp in range(2):
                idx = 2 * r + mp
                kt = k_refs[mp][0, pl.ds(k_start, t), :]
                s = _dot_nt(kt, qm[mp][r]) + bias
                if diagonal:
                    s = jnp.where(visible, s, NEG_INF)
                m_prev = m_sc[idx]
                m_new = jnp.maximum(m_prev, jnp.max(s, axis=0, keepdims=True))
                alpha = jnp.exp2(m_prev - m_new)
                p = jnp.exp2(s - m_new)
                l_sc[idx] = alpha * l_sc[idx] + jnp.sum(p, axis=0, keepdims=True)
                acc_sc[idx] = alpha * acc_sc[idx] + _dot_tn(vt, p.astype(BF16))
                m_sc[idx] = m_new

    def below(kt_idx, carry):
        tile(kt_idx, False)
        return carry

    lax.fori_loop(0, i, below, 0)
    tile(i, True)
    _finish_transposed(lam_ref, subw_ref, lam_init, l_sc, acc_sc, o_ref)


def _prompt_attention(q, k, v, lam_vec, subln_w, slopes2, score_bound2, lam_init):
    b, t, d = q.shape
    tile = ATTN_TILE
    half = d // (2 * LANES)
    q_spec = lambda off: pl.BlockSpec((1, tile, LANES), lambda bi, j, i: (bi, i, j + off))
    k_spec = lambda off: pl.BlockSpec((1, t, LANES), lambda bi, j, i: (bi, 0, j + off))
    v_spec = pl.BlockSpec((1, t, 2 * LANES), lambda bi, j, i: (bi, 0, j))
    o_spec = pl.BlockSpec((1, tile, 2 * LANES), lambda bi, j, i: (bi, i, j))
    smem = pl.BlockSpec(memory_space=pltpu.SMEM)
    lam_spec = pl.BlockSpec(lam_vec.shape, lambda bi, j, i: (0, 0))
    subw_spec = pl.BlockSpec((2 * DA_HEAD_DIM, 1), lambda bi, j, i: (0, 0))
    data_specs = [q_spec(0), q_spec(half), k_spec(0), k_spec(half), v_spec]
    subw = subln_w.reshape(2 * DA_HEAD_DIM, 1)
    stat = pltpu.VMEM((4, 1, tile), F32)
    acc = pltpu.VMEM((4, 2 * DA_HEAD_DIM, tile), F32)
    common = dict(grid=(b, half, t // tile), out_specs=o_spec,
                  out_shape=jax.ShapeDtypeStruct((b, t, d), BF16),
                  compiler_params=_params("parallel", "parallel", "arbitrary"))

    def fixed(shift):
        return pl.pallas_call(
            functools.partial(_attn_fixed_kernel, lam_init),
            in_specs=[smem, smem, lam_spec, subw_spec] + data_specs,
            scratch_shapes=[pltpu.VMEM((2, 1, 2 * tile), F32),
                            pltpu.VMEM((2, 2 * DA_HEAD_DIM, 2 * tile), F32),
                            pltpu.VMEM((2 * tile, 2 * tile), F32),
                            pltpu.VMEM((2 * tile, 2 * tile), F32)],
            name="diff_attn_fixed", **common,
        )(slopes2, shift, lam_vec, subw, q, q, k, k, v)

    def online(_):
        return pl.pallas_call(
            functools.partial(_attn_online_kernel, lam_init),
            in_specs=[smem, lam_spec, subw_spec] + data_specs,
            scratch_shapes=[stat, stat, acc],
            name="diff_attn_online", **common,
        )(slopes2, lam_vec, subw, q, q, k, k, v)

    return lax.cond(score_bound2[0] <= FIXED_SHIFT_MAX_SCORE * LOG2E, fixed, online, score_bound2)


def _sample_attn_kernel(lam_init, past_visible, slopes_ref, lam_ref, subw_ref, q_ref, kc_ref,
                        kn_ref, vc_ref, vn_ref, o_ref):
    tq = q_ref.shape[1]
    past = kc_ref.shape[3]
    half = DA_HEADS // 2
    lam = _lambda_scalar(lam_ref, lam_init)
    lane = lax.broadcasted_iota(jnp.int32, (tq, LANES), 1)
    zero = jnp.zeros((tq, LANES), BF16)

    row_c = lax.broadcasted_iota(jnp.int32, (tq, past), 0)
    col_c = lax.broadcasted_iota(jnp.int32, (tq, past), 1)
    dist_c = (past + row_c - col_c).astype(F32)
    row_n = lax.broadcasted_iota(jnp.int32, (tq, tq), 0)
    col_n = lax.broadcasted_iota(jnp.int32, (tq, tq), 1)
    dist_n = jnp.abs(row_n - col_n).astype(F32)
    if not past_visible:
        vis_c = (col_c // MASK_CHUNK) <= ((past + row_c) // MASK_CHUNK)
        vis_n = ((past + col_n) // MASK_CHUNK) <= ((past + row_n) // MASK_CHUNK)

    for pair in range(half):
        tiles = [slice((mp * half + pair) * LANES, (mp * half + pair + 1) * LANES) for mp in range(2)]
        kc = [kc_ref[mp, 2 * pair:2 * pair + 2].reshape(LANES, past).astype(BF16) for mp in range(2)]
        for r in range(2):
            head = 2 * pair + r
            slope = slopes_ref[head]
            keep = (lane >= DA_HEAD_DIM) if r else (lane < DA_HEAD_DIM)
            vc = vc_ref[pl.ds(head, past, stride=DA_HEADS), :].astype(BF16)
            vn = vn_ref[0, :, head * LANES:(head + 1) * LANES]
            outs = []
            for mp in range(2):
                qm = jnp.where(keep, q_ref[0, :, tiles[mp]], zero)
                s_c = _dot(qm, kc[mp]) - slope * dist_c
                s_n = _dot_nt(qm, kn_ref[0, :, tiles[mp]]) - slope * dist_n
                if not past_visible:
                    s_c = jnp.where(vis_c, s_c, NEG_INF)
                    s_n = jnp.where(vis_n, s_n, NEG_INF)
                m = jnp.maximum(jnp.max(s_c, axis=1, keepdims=True),
                                jnp.max(s_n, axis=1, keepdims=True))
                p_c = jnp.exp2(s_c - m)
                p_n = jnp.exp2(s_n - m)
                l = jnp.sum(p_c, axis=1, keepdims=True) + jnp.sum(p_n, axis=1, keepdims=True)
                acc = _dot(p_c.astype(BF16), vc) + _dot(p_n.astype(BF16), vn)
                outs.append(acc / l)
            o = outs[0] - lam * outs[1]
            ms = jnp.mean(o * o, axis=1, keepdims=True)
            o = o * lax.rsqrt(ms + EPS) * subw_ref[...] * (1.0 - lam_init)
            o_ref[0, :, head * LANES:(head + 1) * LANES] = o.astype(o_ref.dtype)


def _sample_attention(q, k_new, v_new, k_cache, v_cache, layer_j, lam_vec, subln_w, slopes2,
                      lam_init):
    b, t_q, d = q.shape
    past = k_cache.shape[5]
    past_visible = past % MASK_CHUNK == 0 and t_q <= MASK_CHUNK
    row = pl.BlockSpec((1, t_q, d), lambda bi: (bi, 0, 0))
    return pl.pallas_call(
        functools.partial(_sample_attn_kernel, lam_init, past_visible),
        grid=(b,),
        in_specs=[pl.BlockSpec(memory_space=pltpu.SMEM),
                  pl.BlockSpec(lam_vec.shape, lambda bi: (0, 0)),
                  pl.BlockSpec((1, 2 * DA_HEAD_DIM), lambda bi: (0, 0)),
                  row,
                  pl.BlockSpec((None, None, 2, DA_HEADS, DA_HEAD_DIM, past),
                               lambda bi: (layer_j, bi, 0, 0, 0, 0)),
                  row,
                  pl.BlockSpec((None, None, past * DA_HEADS, 2 * DA_HEAD_DIM),
                               lambda bi: (layer_j, bi, 0, 0)),
                  row],
        out_specs=row,
        out_shape=jax.ShapeDtypeStruct((b, t_q, d), BF16),
        compiler_params=_params("parallel"),
        name="diff_attn_sample",
    )(slopes2, lam_vec, subln_w.reshape(1, 2 * DA_HEAD_DIM), q, k_cache, k_new, v_cache, v_new)


def _row_tile(m):
    return 512 if m % 512 == 0 else 256


def _hgrn_layer(x, s0, nw, w_in_bf, lb_logits, onw, w_out_bf, layer_j):
    b, t, d = x.shape
    x2 = x.reshape(b * t, d)
    tm = _row_tile(b * t)
    q, fx, iv, z = _inproj_hgrn(x2, nw.reshape(1, d), w_in_bf, tm)
    o, s_new = _hgrn_rec(q.reshape(b, t, d), fx.reshape(b, t, d), iv.reshape(b, t, d),
                         lb_logits, s0, layer_j)
    y = _outproj(o.reshape(b * t, d), z, x2, w_out_bf, onw.reshape(1, d), True, tm)
    return y.reshape(b, t, d), s_new


def _attn_layer(x, cache, nw, w_in_bf, qn_w, kn_w, lam_vec, subln_w, w_out_bf, layer_idx,
                layer_j, n_layers, k_stack, v_stack):
    b, t, d = x.shape
    x2 = x.reshape(b * t, d)
    groups = d // DA_HEAD_DIM
    gmat = jnp.kron(jnp.eye(LANES // DA_HEAD_DIM, dtype=F32),
                    jnp.full((DA_HEAD_DIM, DA_HEAD_DIM), 1.0 / DA_HEAD_DIM, F32)).astype(BF16)
    prompt = cache is None
    tm = 256 if t % 256 == 0 else t
    q, kb, vb, z, k_stack, v_stack = _inproj_attn(
        x2, t, nw.reshape(1, d), w_in_bf, jnp.tile(qn_w, groups).reshape(1, d),
        jnp.tile(kn_w, groups).reshape(1, d), gmat, layer_j, n_layers, k_stack, v_stack, prompt, tm)
    lam_init = 0.8 - 0.6 * math.exp(-0.3 * layer_idx)
    slopes2 = jnp.exp2(-8.0 * jnp.arange(1, DA_HEADS + 1, dtype=F32) / DA_HEADS) * LOG2E
    q, kb, vb = (a.reshape(b, t, d) for a in (q, kb, vb))
    if prompt:
        bound2 = (DA_HEAD_DIM ** 0.5 * LOG2E * 1.02) * jnp.max(jnp.abs(qn_w)) * jnp.max(jnp.abs(kn_w))
        o = _prompt_attention(q, kb, vb, lam_vec, subln_w, slopes2, bound2.reshape(1), lam_init)
    else:
        k_cache, v_cache = cache
        o = _sample_attention(q, kb, vb, k_cache, v_cache, layer_j, lam_vec, subln_w, slopes2,
                              lam_init)
    y = _outproj(o.reshape(b * t, d), z, x2, w_out_bf, jnp.ones((1, d), F32), False,
                 _row_tile(b * t))
    return y.reshape(b, t, d), k_stack, v_stack


def kernel(x_prompt, x_sample, cache_k, cache_v, state_hgrn, norm_w, hgrn_w_in, hgrn_lb_logits,
           hgrn_onorm_w, hgrn_w_out, attn_w_in, attn_q_norm, attn_k_norm, attn_lambda, attn_subln,
           attn_w_out):
    depth = norm_w.shape[0]
    n_attn = cache_k.shape[0]
    bp, tp, d = x_prompt.shape
    bs, ts, _ = x_sample.shape
    past = cache_k.shape[2]
    cache_k_t = jnp.transpose(cache_k, (0, 1, 3, 4, 5, 2))
    cache_v2 = cache_v.reshape(n_attn, bs, past * DA_HEADS, 2 * DA_HEAD_DIM)
    yp, ys = x_prompt, x_sample
    kp = vp = ks_ = vs_ = None
    sp, ss = [], []
    for l in range(depth):
        j = l // N_MIXERS
        if l % N_MIXERS == 0:
            w_in = hgrn_w_in[j].astype(BF16)
            w_out = hgrn_w_out[j].astype(BF16)
            yp, s_p = _hgrn_layer(yp, None, norm_w[l], w_in, hgrn_lb_logits, hgrn_onorm_w[j],
                                  w_out, j)
            ys, s_s = _hgrn_layer(ys, state_hgrn[j], norm_w[l], w_in, hgrn_lb_logits,
                                  hgrn_onorm_w[j], w_out, j)
            sp.append(s_p)
            ss.append(s_s)
        else:
            w_in = attn_w_in[j].astype(BF16)
            w_out = attn_w_out[j].astype(BF16)
            yp, kp, vp = _attn_layer(yp, None, norm_w[l], w_in, attn_q_norm[j], attn_k_norm[j],
                                     attn_lambda[j], attn_subln[j], w_out, l, j, n_attn, kp, vp)
            ys, ks_, vs_ = _attn_layer(ys, (cache_k_t, cache_v2), norm_w[l], w_in,
                                       attn_q_norm[j], attn_k_norm[j], attn_lambda[j],
                                       attn_subln[j], w_out, l, j, n_attn, ks_, vs_)
    new_k_prompt = jnp.transpose(kp.reshape(n_attn, bp, 2, DA_HEADS, DA_HEAD_DIM, tp),
                                 (0, 1, 5, 2, 3, 4))
    return (yp, ys, new_k_prompt,
            vp.reshape(n_attn, bp, tp, DA_HEADS, 2 * DA_HEAD_DIM),
            ks_.reshape(n_attn, bs, ts, 2, DA_HEADS, DA_HEAD_DIM),
            vs_.reshape(n_attn, bs, ts, DA_HEADS, 2 * DA_HEAD_DIM),
            jnp.stack(sp), jnp.stack(ss))
```

```python
import functools
import math

import jax
import jax.numpy as jnp
from jax import lax
from jax.experimental import pallas as pl
from jax.experimental.pallas import tpu as pltpu

F32 = jnp.float32
BF16 = jnp.bfloat16

EPS = 1e-6
NEG_INF = -1e30
LOG2E = 1.4426950408889634
LANES = 128
HG_HEADS = 8
HG_DK = 128
DA_HEADS = 8
DA_HEAD_DIM = 64
MASK_CHUNK = 64
N_MIXERS = 2

REC_CHUNK = 64
REC_MATMUL_LEVELS = (1, 2, 4)
REC_ROW_LEVELS = (8, 16, 32)
ATTN_TILE = 256
FIXED_SHIFT_MAX_SCORE = 40.0
VMEM_LIMIT = 56 * 1024 * 1024


def _dot(a, b):
    return jnp.dot(a, b, preferred_element_type=F32)


def _dot_nt(a, b):
    return lax.dot_general(a, b, (((1,), (1,)), ((), ())), preferred_element_type=F32)


def _dot_tn(a, b):
    return lax.dot_general(a, b, (((0,), (0,)), ((), ())), preferred_element_type=F32)


def _rms_rows(x, w):
    ms = jnp.mean(x * x, axis=-1, keepdims=True)
    return x * lax.rsqrt(ms + EPS) * w


def _resident(shape):
    nd = len(shape)
    return pl.BlockSpec(shape, lambda *_: (0,) * nd, pipeline_mode=pl.Buffered(1))


def _params(*semantics):
    return pltpu.CompilerParams(dimension_semantics=semantics, vmem_limit_bytes=VMEM_LIMIT)


def _inproj_hgrn_kernel(x_ref, nw_ref, w_ref, q_ref, fx_ref, i_ref, z_ref):
    d = x_ref.shape[1]
    h = _rms_rows(x_ref[...], nw_ref[...]).astype(BF16)
    q_ref[...] = (_dot(h, w_ref[:, 0:d]) * (HG_DK ** -0.5)).astype(q_ref.dtype)
    for c, o_ref in ((1, fx_ref), (2, i_ref), (3, z_ref)):
        o_ref[...] = _dot(h, w_ref[:, c * d:(c + 1) * d]).astype(o_ref.dtype)


def _inproj_hgrn(x2, nw, w_bf, tm):
    m, d = x2.shape
    row = pl.BlockSpec((tm, d), lambda i: (i, 0))
    return pl.pallas_call(
        _inproj_hgrn_kernel,
        grid=(m // tm,),
        in_specs=[row, _resident((1, d)), _resident((d, 4 * d))],
        out_specs=[row, row, row, row],
        out_shape=[jax.ShapeDtypeStruct((m, d), BF16), jax.ShapeDtypeStruct((m, d), F32),
                   jax.ShapeDtypeStruct((m, d), BF16), jax.ShapeDtypeStruct((m, d), BF16)],
        compiler_params=_params("parallel"),
        name="inproj_hgrn",
    )(x2, nw, w_bf)


def _inproj_attn_kernel(k_transposed, n_alias, x_ref, nw_ref, w_ref, qnw_ref, knw_ref, g_ref, *rest):
    q_ref, kb_ref, vb_ref, z_ref, ks_ref, vs_ref = rest[n_alias:]
    d = x_ref.shape[1]
    h = _rms_rows(x_ref[...], nw_ref[...]).astype(BF16)
    gmat = g_ref[...]

    def group_norm(y, w):
        cols = []
        for t in range(d // LANES):
            yt = y[:, t * LANES:(t + 1) * LANES]
            ms = _dot((yt * yt).astype(BF16), gmat)
            cols.append(yt * lax.rsqrt(ms + EPS))
        return jnp.concatenate(cols, axis=1) * w

    q = group_norm(_dot(h, w_ref[:, 0:d]), qnw_ref[...])
    q_ref[...] = (q * (DA_HEAD_DIM ** -0.5 * LOG2E)).astype(BF16)
    k = group_norm(_dot(h, w_ref[:, d:2 * d]), knw_ref[...])
    kb_ref[...] = k.astype(BF16)
    ks_ref[0, 0] = k.T if k_transposed else k
    v = _dot(h, w_ref[:, 2 * d:3 * d])
    vs_ref[0] = v
    vb_ref[...] = v.astype(BF16)
    z_ref[...] = _dot(h, w_ref[:, 3 * d:4 * d]).astype(BF16)


def _inproj_attn(x2, seq, nw, w_bf, qnw, knw, gmat, layer_j, n_layers, k_stack, v_stack,
                 k_transposed, tm):
    m, d = x2.shape
    batch = m // seq
    tiles_per_seq = seq // tm
    row = pl.BlockSpec((tm, d), lambda i: (i, 0))
    if k_transposed:
        ks_shape = (n_layers, batch, d, seq)
        ks_spec = pl.BlockSpec((1, 1, d, tm),
                               lambda i: (layer_j, i // tiles_per_seq, 0, i % tiles_per_seq))
    else:
        ks_shape = (n_layers, batch, seq, d)
        ks_spec = pl.BlockSpec((1, 1, tm, d),
                               lambda i: (layer_j, i // tiles_per_seq, i % tiles_per_seq, 0))
    vs_spec = pl.BlockSpec((1, tm, d), lambda i: (layer_j, i, 0))
    in_specs = [row, _resident((1, d)), _resident((d, 4 * d)), _resident((1, d)),
                _resident((1, d)), _resident((LANES, LANES))]
    args = [x2, nw, w_bf, qnw, knw, gmat]
    aliases = {}
    if k_stack is not None:
        in_specs += [pl.BlockSpec(memory_space=pl.ANY)] * 2
        aliases = {len(args): 4, len(args) + 1: 5}
        args += [k_stack, v_stack]
    shp = lambda dt: jax.ShapeDtypeStruct((m, d), dt)
    return pl.pallas_call(
        functools.partial(_inproj_attn_kernel, k_transposed, len(aliases)),
        grid=(m // tm,),
        in_specs=in_specs,
        out_specs=[row, row, row, row, ks_spec, vs_spec],
        out_shape=[shp(BF16), shp(BF16), shp(BF16), shp(BF16),
                   jax.ShapeDtypeStruct(ks_shape, F32),
                   jax.ShapeDtypeStruct((n_layers, m, d), F32)],
        input_output_aliases=aliases,
        compiler_params=_params("parallel"),
        name="inproj_attn",
    )(*args)


def _outproj_kernel(full_norm, o_ref, z_ref, x_ref, w_ref, ow_ref, y_ref):
    o = o_ref[...].astype(F32)
    if full_norm:
        o = _rms_rows(o, ow_ref[...])
    z = z_ref[...].astype(F32)
    gated = (o * (z * jax.nn.sigmoid(z))).astype(BF16)
    y_ref[...] = x_ref[...] + _dot(gated, w_ref[...])


def _outproj(o2, z2, x2, w_bf, ow, full_norm, tm):
    m, d = x2.shape
    row = pl.BlockSpec((tm, d), lambda i: (i, 0))
    return pl.pallas_call(
        functools.partial(_outproj_kernel, full_norm),
        grid=(m // tm,),
        in_specs=[row, row, row, _resident((d, d)), _resident((1, d))],
        out_specs=row,
        out_shape=jax.ShapeDtypeStruct((m, d), F32),
        compiler_params=_params("parallel"),
        name="outproj",
    )(o2, z2, x2, w_bf, ow)


def _range_matrices(c):
    t = jnp.arange(c)[:, None]
    s = jnp.arange(c)[None, :]
    blocks = [s <= t]
    for w in REC_MATMUL_LEVELS:
        ref = (t // (2 * w)) * (2 * w) + w - 1
        upper = (t & w) != 0
        blocks.append(jnp.where(upper, (s > ref) & (s <= t), (s > t) & (s <= ref)))
    e = jnp.concatenate(blocks, axis=0).astype(BF16)
    return jnp.concatenate([e, e], axis=1)


def _hgrn_rec_kernel(layer_j, has_s0, q_ref, fx_ref, v_ref, lbl_ref, emat_ref, *rest):
    if has_s0:
        s0_ref, o_ref, sfin_ref, st_ref = rest
    else:
        o_ref, sfin_ref, st_ref = rest
    c_idx = pl.program_id(1)
    c = q_ref.shape[1]
    d = q_ref.shape[2]

    @pl.when(c_idx == 0)
    def _():
        for h in range(HG_HEADS):
            if has_s0:
                st_ref[h] = s0_ref[0, h].T
            else:
                st_ref[h] = jnp.zeros((HG_DK, HG_DK), F32)

    lg = lbl_ref[...]
    ex = jnp.exp(lg - jnp.max(lg, axis=0, keepdims=True))
    p = ex / jnp.sum(ex, axis=0, keepdims=True)
    cs = p[0:1]
    for r in range(1, layer_j + 1):
        cs = cs + p[r:r + 1]
    lb = cs - p[0:1]

    f = lb + (1.0 - lb) * jax.nn.sigmoid(fx_ref[0])
    g = jnp.log2(f)
    kk = 1.0 - f
    g_hi = pltpu.bitcast(pltpu.bitcast(g, jnp.uint32) & jnp.uint32(0xFFFF0000), F32)
    g_pieces = jnp.concatenate([g_hi.astype(BF16), (g - g_hi).astype(BF16)], axis=0)
    ranges = _dot(emat_ref[...], g_pieces)
    b = ranges[0:c]

    q = q_ref[0].astype(F32)
    q_state = (q * jnp.exp2(b)).astype(BF16)
    k_state = (kk * jnp.exp2(b[c - 1:c] - b)).astype(BF16)
    state_decay = jnp.exp2(b[c - 1:c])

    row = lax.broadcasted_iota(jnp.int32, (c, d), 0)
    level_ops = []
    for li, w in enumerate(REC_MATMUL_LEVELS):
        x = jnp.exp2(ranges[(li + 1) * c:(li + 2) * c])
        level_ops.append((jnp.where((row & w) != 0, q, kk) * x).astype(BF16))
    for w in REC_ROW_LEVELS:
        pieces = []
        for blk in range(c // w):
            rows = slice(blk * w, (blk + 1) * w)
            mid = (blk // 2) * 2 * w + w - 1
            if blk % 2:
                pieces.append(q[rows] * jnp.exp2(b[rows] - b[mid:mid + 1]))
            else:
                pieces.append(kk[rows] * jnp.exp2(b[mid:mid + 1] - b[rows]))
        level_ops.append(jnp.concatenate(pieces, axis=0).astype(BF16))
    q_bf, k_bf = q.astype(BF16), kk.astype(BF16)

    tt = lax.broadcasted_iota(jnp.int32, (c, c), 0)
    ss = lax.broadcasted_iota(jnp.int32, (c, c), 1)
    diag_mask = tt == ss
    level_masks = [((tt // (2 * w)) == (ss // (2 * w))) & ((tt & w) != 0) & ((ss & w) == 0)
                   for w in REC_MATMUL_LEVELS + REC_ROW_LEVELS]

    heads = [slice(h * HG_DK, (h + 1) * HG_DK) for h in range(HG_HEADS)]
    att = []
    for sl in heads:
        a = jnp.where(diag_mask, _dot_nt(q_bf[:, sl], k_bf[:, sl]), 0.0)
        for mask, y in zip(level_masks, level_ops):
            a = jnp.where(mask, _dot_nt(y[:, sl], y[:, sl]), a)
        att.append(a.astype(BF16))
    for h, sl in enumerate(heads):
        o = _dot(att[h], v_ref[0, :, sl]) + _dot_nt(q_state[:, sl], st_ref[h].astype(BF16))
        o_ref[0, :, sl] = o.astype(o_ref.dtype)
    for h, sl in enumerate(heads):
        st_ref[h] = st_ref[h] * state_decay[:, sl] + _dot_tn(v_ref[0, :, sl], k_state[:, sl])

    @pl.when(c_idx == pl.num_programs(1) - 1)
    def _():
        for h in range(HG_HEADS):
            sfin_ref[0, h] = st_ref[h].T


def _hgrn_rec(q, fx, v, lb_logits, s0, layer_j):
    b, t, d = q.shape
    c = REC_CHUNK
    blk = pl.BlockSpec((1, c, d), lambda i, j: (i, j, 0))
    st_blk = pl.BlockSpec((1, HG_HEADS, HG_DK, HG_DK), lambda i, j: (i, 0, 0, 0))
    emat = _range_matrices(c)
    in_specs = [blk, blk, blk, _resident(lb_logits.shape), _resident(emat.shape)]
    args = [q, fx, v, lb_logits, emat]
    if s0 is not None:
        in_specs.append(st_blk)
        args.append(s0)
    return pl.pallas_call(
        functools.partial(_hgrn_rec_kernel, layer_j, s0 is not None),
        grid=(b, t // c),
        in_specs=in_specs,
        out_specs=[blk, st_blk],
        out_shape=[jax.ShapeDtypeStruct((b, t, d), BF16),
                   jax.ShapeDtypeStruct((b, HG_HEADS, HG_DK, HG_DK), F32)],
        scratch_shapes=[pltpu.VMEM((HG_HEADS, HG_DK, HG_DK), F32)],
        compiler_params=_params("parallel", "arbitrary"),
        name="hgrn_rec",
    )(*args)


def _masked_queries(q_refs):
    tq = q_refs[0].shape[1]
    lane = lax.broadcasted_iota(jnp.int32, (tq, LANES), 1)
    zero = jnp.zeros((tq, LANES), BF16)
    return [[jnp.where(lane < DA_HEAD_DIM, qr[0], zero), jnp.where(lane >= DA_HEAD_DIM, qr[0], zero)]
            for qr in q_refs]


def _lambda_scalar(lam_ref, lam_init):
    lv = lam_ref[...]
    return (jnp.exp(jnp.sum(lv[0:1] * lv[1:2], axis=1, keepdims=True))
            - jnp.exp(jnp.sum(lv[2:3] * lv[3:4], axis=1, keepdims=True)) + lam_init)


def _finish_transposed(lam_ref, subw_ref, lam_init, l_sc, acc_sc, o_ref):
    lam = _lambda_scalar(lam_ref, lam_init)
    for r in range(2):
        o = acc_sc[2 * r] / l_sc[2 * r] - lam * (acc_sc[2 * r + 1] / l_sc[2 * r + 1])
        ms = jnp.mean(o * o, axis=0, keepdims=True)
        o = o * lax.rsqrt(ms + EPS) * subw_ref[...] * (1.0 - lam_init)
        o_ref[0, :, r * LANES:(r + 1) * LANES] = o.T.astype(o_ref.dtype)


def _attn_fixed_kernel(lam_init, slopes_ref, shift_ref, lam_ref, subw_ref, q0_ref, q1_ref,
                       k0_ref, k1_ref, v_ref, o_ref, l_sc, acc_sc, below_sc, last_sc):
    j = pl.program_id(1)
    i = pl.program_id(2)
    t = q0_ref.shape[1]

    @pl.when(i == 0)
    def _():
        row = lax.broadcasted_iota(jnp.int32, (2 * t, t), 0)
        col = lax.broadcasted_iota(jnp.int32, (2 * t, t), 1)
        below_dist = (3 * t + col - row).astype(F32)
        last_dist = jnp.abs(t + col - row).astype(F32)
        visible = (row < t) | (((row - t) // MASK_CHUNK) <= (col // MASK_CHUNK))
        for r in range(2):
            slope = slopes_ref[2 * j + r]
            below_sc[:, r * t:(r + 1) * t] = -slope * below_dist - shift_ref[0]
            last_sc[:, r * t:(r + 1) * t] = jnp.where(
                visible, -slope * last_dist - shift_ref[0], NEG_INF)

    qm = _masked_queries((q0_ref, q1_ref))
    q_pair = [jnp.concatenate(qm[mp], axis=0) for mp in range(2)]
    k_refs = (k0_ref, k1_ref)
    l_sc[...] = jnp.zeros(l_sc.shape, F32)
    acc_sc[...] = jnp.zeros(acc_sc.shape, F32)

    def block(k_start, rows, bias_ref, bias_row0, far):
        s = [_dot_nt(k_refs[mp][0, pl.ds(k_start, rows), :], q_pair[mp]) for mp in range(2)]
        for r in range(2):
            cols = slice(r * t, (r + 1) * t)
            bias = bias_ref[bias_row0:bias_row0 + rows, cols]
            p = [jnp.exp2(s[mp][:, cols] + bias) for mp in range(2)]
            l_new = [jnp.sum(pm, axis=0, keepdims=True) for pm in p]
            vt = v_ref[0, pl.ds(k_start, rows), r * LANES:(r + 1) * LANES]
            acc_new = _dot_tn(vt, jnp.concatenate([pm.astype(BF16) for pm in p], axis=1))
            if far is not None:
                weight = jnp.exp2(jnp.full((1, t), -slopes_ref[2 * j + r] * far, F32))
                l_new = [ln * weight for ln in l_new]
                acc_new = acc_new * jnp.concatenate([weight, weight], axis=1)
            for mp in range(2):
                l_sc[mp, :, cols] = l_sc[mp, :, cols] + l_new[mp]
            acc_sc[r] = acc_sc[r] + acc_new

    def two_tiles(pair_idx, carry):
        k_start = pl.multiple_of(pair_idx * (2 * t), 2 * t)
        block(k_start, 2 * t, below_sc, 0, ((i - 2 * pair_idx - 3) * t).astype(F32))
        return carry

    lax.fori_loop(0, jnp.maximum(i - 1, 0) // 2, two_tiles, 0)

    @pl.when((i >= 2) & (i % 2 == 0))
    def _():
        block(pl.multiple_of((i - 2) * t, t), t, below_sc, t, None)

    @pl.when(i >= 1)
    def _():
        block(pl.multiple_of((i - 1) * t, t), 2 * t, last_sc, 0, None)

    @pl.when(i == 0)
    def _():
        block(0, t, last_sc, t, None)

    lam = _lambda_scalar(lam_ref, lam_init)
    for r in range(2):
        num = acc_sc[r]
        inv0 = 1.0 / l_sc[0][:, r * t:(r + 1) * t]
        inv1 = lam / l_sc[1][:, r * t:(r + 1) * t]
        o = num[:, :t] * inv0 - num[:, t:] * inv1
        ms = jnp.mean(o * o, axis=0, keepdims=True)
        o = o * lax.rsqrt(ms + EPS) * subw_ref[...] * (1.0 - lam_init)
        o_ref[0, :, r * LANES:(r + 1) * LANES] = o.T.astype(o_ref.dtype)


def _attn_online_kernel(lam_init, slopes_ref, lam_ref, subw_ref, q0_ref, q1_ref, k0_ref, k1_ref,
                        v_ref, o_ref, m_sc, l_sc, acc_sc):
    j = pl.program_id(1)
    i = pl.program_id(2)
    t = q0_ref.shape[1]
    qm = _masked_queries((q0_ref, q1_ref))
    k_refs = (k0_ref, k1_ref)
    m_sc[...] = jnp.full(m_sc.shape, -jnp.inf, F32)
    l_sc[...] = jnp.zeros(l_sc.shape, F32)
    acc_sc[...] = jnp.zeros(acc_sc.shape, F32)

    def tile(kt_idx, diagonal):
        k_start = pl.multiple_of(kt_idx * t, t)
        row = lax.broadcasted_iota(jnp.int32, (t, t), 0)
        col = lax.broadcasted_iota(jnp.int32, (t, t), 1)
        dist = jnp.abs((i - kt_idx) * t + (col - row)).astype(F32)
        if diagonal:
            visible = (row // MASK_CHUNK) <= (col // MASK_CHUNK)
        for r in range(2):
            bias = -slopes_ref[2 * j + r] * dist
            vt = v_ref[0, pl.ds(k_start, t), r * LANES:(r + 1) * LANES]
            for mp in range(2):
                idx = 2 * r + mp
                kt = k_refs[mp][0, pl.ds(k_start, t), :]
                s = _dot_nt(kt, qm[mp][r]) + bias
                if diagonal:
                    s = jnp.where(visible, s, NEG_INF)
                m_prev = m_sc[idx]
                m_new = jnp.maximum(m_prev, jnp.max(s, axis=0, keepdims=True))
                alpha = jnp.exp2(m_prev - m_new)
                p = jnp.exp2(s - m_new)
                l_sc[idx] = alpha * l_sc[idx] + jnp.sum(p, axis=0, keepdims=True)
                acc_sc[idx] = alpha * acc_sc[idx] + _dot_tn(vt, p.astype(BF16))
                m_sc[idx] = m_new

    def below(kt_idx, carry):
        tile(kt_idx, False)
        return carry

    lax.fori_loop(0, i, below, 0)
    tile(i, True)
    _finish_transposed(lam_ref, subw_ref, lam_init, l_sc, acc_sc, o_ref)


def _prompt_attention(q, k, v, lam_vec, subln_w, slopes2, score_bound2, lam_init):
    b, t, d = q.shape
    tile = ATTN_TILE
    half = d // (2 * LANES)
    q_spec = lambda off: pl.BlockSpec((1, tile, LANES), lambda bi, j, i: (bi, i, j + off))
    k_spec = lambda off: pl.BlockSpec((1, t, LANES), lambda bi, j, i: (bi, 0, j + off))
    v_spec = pl.BlockSpec((1, t, 2 * LANES), lambda bi, j, i: (bi, 0, j))
    o_spec = pl.BlockSpec((1, tile, 2 * LANES), lambda bi, j, i: (bi, i, j))
    smem = pl.BlockSpec(memory_space=pltpu.SMEM)
    lam_spec = pl.BlockSpec(lam_vec.shape, lambda bi, j, i: (0, 0))
    subw_spec = pl.BlockSpec((2 * DA_HEAD_DIM, 1), lambda bi, j, i: (0, 0))
    data_specs = [q_spec(0), q_spec(half), k_spec(0), k_spec(half), v_spec]
    subw = subln_w.reshape(2 * DA_HEAD_DIM, 1)
    stat = pltpu.VMEM((4, 1, tile), F32)
    acc = pltpu.VMEM((4, 2 * DA_HEAD_DIM, tile), F32)
    common = dict(grid=(b, half, t // tile), out_specs=o_spec,
                  out_shape=jax.ShapeDtypeStruct((b, t, d), BF16),
                  compiler_params=_params("parallel", "parallel", "arbitrary"))

    def fixed(shift):
        return pl.pallas_call(
            functools.partial(_attn_fixed_kernel, lam_init),
            in_specs=[smem, smem, lam_spec, subw_spec] + data_specs,
            scratch_shapes=[pltpu.VMEM((2, 1, 2 * tile), F32),
                            pltpu.VMEM((2, 2 * DA_HEAD_DIM, 2 * tile), F32),
                            pltpu.VMEM((2 * tile, 2 * tile), F32),
                            pltpu.VMEM((2 * tile, 2 * tile), F32)],
            name="diff_attn_fixed", **common,
        )(slopes2, shift, lam_vec, subw, q, q, k, k, v)

    def online(_):
        return pl.pallas_call(
            functools.partial(_attn_online_kernel, lam_init),
            in_specs=[smem, lam_spec, subw_spec] + data_specs,
            scratch_shapes=[stat, stat, acc],
            name="diff_attn_online", **common,
        )(slopes2, lam_vec, subw, q, q, k, k, v)

    return lax.cond(score_bound2[0] <= FIXED_SHIFT_MAX_SCORE * LOG2E, fixed, online, score_bound2)


def _sample_attn_kernel(lam_init, past_visible, slopes_ref, lam_ref, subw_ref, q_ref, kc_ref,
                        kn_ref, vc_ref, vn_ref, o_ref):
    tq = q_ref.shape[1]
    past = kc_ref.shape[3]
    half = DA_HEADS // 2
    lam = _lambda_scalar(lam_ref, lam_init)
    lane = lax.broadcasted_iota(jnp.int32, (tq, LANES), 1)
    zero = jnp.zeros((tq, LANES), BF16)

    row_c = lax.broadcasted_iota(jnp.int32, (tq, past), 0)
    col_c = lax.broadcasted_iota(jnp.int32, (tq, past), 1)
    dist_c = (past + row_c - col_c).astype(F32)
    row_n = lax.broadcasted_iota(jnp.int32, (tq, tq), 0)
    col_n = lax.broadcasted_iota(jnp.int32, (tq, tq), 1)
    dist_n = jnp.abs(row_n - col_n).astype(F32)
    if not past_visible:
        vis_c = (col_c // MASK_CHUNK) <= ((past + row_c) // MASK_CHUNK)
        vis_n = ((past + col_n) // MASK_CHUNK) <= ((past + row_n) // MASK_CHUNK)

    for pair in range(half):
        tiles = [slice((mp * half + pair) * LANES, (mp * half + pair + 1) * LANES) for mp in range(2)]
        kc = [kc_ref[mp, 2 * pair:2 * pair + 2].reshape(LANES, past).astype(BF16) for mp in range(2)]
        for r in range(2):
            head = 2 * pair + r
            slope = slopes_ref[head]
            keep = (lane >= DA_HEAD_DIM) if r else (lane < DA_HEAD_DIM)
            vc = vc_ref[pl.ds(head, past, stride=DA_HEADS), :].astype(BF16)
            vn = vn_ref[0, :, head * LANES:(head + 1) * LANES]
            outs = []
            for mp in range(2):
                qm = jnp.where(keep, q_ref[0, :, tiles[mp]], zero)
                s_c = _dot(qm, kc[mp]) - slope * dist_c
                s_n = _dot_nt(qm, kn_ref[0, :, tiles[mp]]) - slope * dist_n
                if not past_visible:
                    s_c = jnp.where(vis_c, s_c, NEG_INF)
                    s_n = jnp.where(vis_n, s_n, NEG_INF)
                m = jnp.maximum(jnp.max(s_c, axis=1, keepdims=True),
                                jnp.max(s_n, axis=1, keepdims=True))
                p_c = jnp.exp2(s_c - m)
                p_n = jnp.exp2(s_n - m)
                l = jnp.sum(p_c, axis=1, keepdims=True) + jnp.sum(p_n, axis=1, keepdims=True)
                acc = _dot(p_c.astype(BF16), vc) + _dot(p_n.astype(BF16), vn)
                outs.append(acc / l)
            o = outs[0] - lam * outs[1]
            ms = jnp.mean(o * o, axis=1, keepdims=True)
            o = o * lax.rsqrt(ms + EPS) * subw_ref[...] * (1.0 - lam_init)
            o_ref[0, :, head * LANES:(head + 1) * LANES] = o.astype(o_ref.dtype)


def _sample_attention(q, k_new, v_new, k_cache, v_cache, layer_j, lam_vec, subln_w, slopes2,
                      lam_init):
    b, t_q, d = q.shape
    past = k_cache.shape[5]
    past_visible = past % MASK_CHUNK == 0 and t_q <= MASK_CHUNK
    row = pl.BlockSpec((1, t_q, d), lambda bi: (bi, 0, 0))
    return pl.pallas_call(
        functools.partial(_sample_attn_kernel, lam_init, past_visible),
        grid=(b,),
        in_specs=[pl.BlockSpec(memory_space=pltpu.SMEM),
                  pl.BlockSpec(lam_vec.shape, lambda bi: (0, 0)),
                  pl.BlockSpec((1, 2 * DA_HEAD_DIM), lambda bi: (0, 0)),
                  row,
                  pl.BlockSpec((None, None, 2, DA_HEADS, DA_HEAD_DIM, past),
                               lambda bi: (layer_j, bi, 0, 0, 0, 0)),
                  row,
                  pl.BlockSpec((None, None, past * DA_HEADS, 2 * DA_HEAD_DIM),
                               lambda bi: (layer_j, bi, 0, 0)),
                  row],
        out_specs=row,
        out_shape=jax.ShapeDtypeStruct((b, t_q, d), BF16),
        compiler_params=_params("parallel"),
        name="diff_attn_sample",
    )(slopes2, lam_vec, subln_w.reshape(1, 2 * DA_HEAD_DIM), q, k_cache, k_new, v_cache, v_new)


def _row_tile(m):
    return 512 if m % 512 == 0 else 256


def _hgrn_layer(x, s0, nw, w_in_bf, lb_logits, onw, w_out_bf, layer_j):
    b, t, d = x.shape
    x2 = x.reshape(b * t, d)
    tm = _row_tile(b * t)
    q, fx, iv, z = _inproj_hgrn(x2, nw.reshape(1, d), w_in_bf, tm)
    o, s_new = _hgrn_rec(q.reshape(b, t, d), fx.reshape(b, t, d), iv.reshape(b, t, d),
                         lb_logits, s0, layer_j)
    y = _outproj(o.reshape(b * t, d), z, x2, w_out_bf, onw.reshape(1, d), True, tm)
    return y.reshape(b, t, d), s_new


def _attn_layer(x, cache, nw, w_in_bf, qn_w, kn_w, lam_vec, subln_w, w_out_bf, layer_idx,
                layer_j, n_layers, k_stack, v_stack):
    b, t, d = x.shape
    x2 = x.reshape(b * t, d)
    groups = d // DA_HEAD_DIM
    gmat = jnp.kron(jnp.eye(LANES // DA_HEAD_DIM, dtype=F32),
                    jnp.full((DA_HEAD_DIM, DA_HEAD_DIM), 1.0 / DA_HEAD_DIM, F32)).astype(BF16)
    prompt = cache is None
    tm = 256 if t % 256 == 0 else t
    q, kb, vb, z, k_stack, v_stack = _inproj_attn(
        x2, t, nw.reshape(1, d), w_in_bf, jnp.tile(qn_w, groups).reshape(1, d),
        jnp.tile(kn_w, groups).reshape(1, d), gmat, layer_j, n_layers, k_stack, v_stack, prompt, tm)
    lam_init = 0.8 - 0.6 * math.exp(-0.3 * layer_idx)
    slopes2 = jnp.exp2(-8.0 * jnp.arange(1, DA_HEADS + 1, dtype=F32) / DA_HEADS) * LOG2E
    q, kb, vb = (a.reshape(b, t, d) for a in (q, kb, vb))
    if prompt:
        bound2 = (DA_HEAD_DIM ** 0.5 * LOG2E * 1.02) * jnp.max(jnp.abs(qn_w)) * jnp.max(jnp.abs(kn_w))
        o = _prompt_attention(q, kb, vb, lam_vec, subln_w, slopes2, bound2.reshape(1), lam_init)
    else:
        k_cache, v_cache = cache
        o = _sample_attention(q, kb, vb, k_cache, v_cache, layer_j, lam_vec, subln_w, slopes2,
                              lam_init)
    y = _outproj(o.reshape(b * t, d), z, x2, w_out_bf, jnp.ones((1, d), F32), False,
                 _row_tile(b * t))
    return y.reshape(b, t, d), k_stack, v_stack


def kernel(x_prompt, x_sample, cache_k, cache_v, state_hgrn, norm_w, hgrn_w_in, hgrn_lb_logits,
           hgrn_onorm_w, hgrn_w_out, attn_w_in, attn_q_norm, attn_k_norm, attn_lambda, attn_subln,
           attn_w_out):
    depth = norm_w.shape[0]
    n_attn = cache_k.shape[0]
    bp, tp, d = x_prompt.shape
    bs, ts, _ = x_sample.shape
    past = cache_k.shape[2]
    cache_k_t = jnp.transpose(cache_k, (0, 1, 3, 4, 5, 2))
    cache_v2 = cache_v.reshape(n_attn, bs, past * DA_HEADS, 2 * DA_HEAD_DIM)
    yp, ys = x_prompt, x_sample
    kp = vp = ks_ = vs_ = None
    sp, ss = [], []
    for l in range(depth):
        j = l // N_MIXERS
        if l % N_MIXERS == 0:
            w_in = hgrn_w_in[j].astype(BF16)
            w_out = hgrn_w_out[j].astype(BF16)
            yp, s_p = _hgrn_layer(yp, None, norm_w[l], w_in, hgrn_lb_logits, hgrn_onorm_w[j],
                                  w_out, j)
            ys, s_s = _hgrn_layer(ys, state_hgrn[j], norm_w[l], w_in, hgrn_lb_logits,
                                  hgrn_onorm_w[j], w_out, j)
            sp.append(s_p)
            ss.append(s_s)
        else:
            w_in = attn_w_in[j].astype(BF16)
            w_out = attn_w_out[j].astype(BF16)
            yp, kp, vp = _attn_layer(yp, None, norm_w[l], w_in, attn_q_norm[j], attn_k_norm[j],
                                     attn_lambda[j], attn_subln[j], w_out, l, j, n_attn, kp, vp)
            ys, ks_, vs_ = _attn_layer(ys, (cache_k_t, cache_v2), norm_w[l], w_in,
                                       attn_q_norm[j], attn_k_norm[j], attn_lambda[j],
                                       attn_subln[j], w_out, l, j, n_attn, ks_, vs_)
    new_k_prompt = jnp.transpose(kp.reshape(n_attn, bp, 2, DA_HEADS, DA_HEAD_DIM, tp),
                                 (0, 1, 5, 2, 3, 4))
    return (yp, ys, new_k_prompt,
            vp.reshape(n_attn, bp, tp, DA_HEADS, 2 * DA_HEAD_DIM),
            ks_.reshape(n_attn, bs, ts, 2, DA_HEADS, DA_HEAD_DIM),
            vs_.reshape(n_attn, bs, ts, DA_HEADS, 2 * DA_HEAD_DIM),
            jnp.stack(sp), jnp.stack(ss))
```

```python
import functools
import math

import jax
import jax.numpy as jnp
from jax import lax
from jax.experimental import pallas as pl
from jax.experimental.pallas import tpu as pltpu

F32 = jnp.float32
BF16 = jnp.bfloat16

EPS = 1e-6
NEG_INF = -1e30
LOG2E = 1.4426950408889634
LANES = 128
MXU_COLS = 256
HG_HEADS = 8
HG_DK = 128
DA_HEADS = 8
DA_HEAD_DIM = 64
MASK_CHUNK = 64
N_MIXERS = 2

REC_CHUNK = 64
REC_MATMUL_LEVELS = (1, 2, 4)
REC_ROW_LEVELS = (8, 16, 32)
ATTN_TILE = 512
FIXED_SHIFT_MAX_SCORE = 40.0
VMEM_LIMIT = 56 * 1024 * 1024


def _dot(a, b):
    return jnp.dot(a, b, preferred_element_type=F32)


def _dot_nt(a, b):
    return lax.dot_general(a, b, (((1,), (1,)), ((), ())), preferred_element_type=F32)


def _dot_tn(a, b):
    return lax.dot_general(a, b, (((0,), (0,)), ((), ())), preferred_element_type=F32)


def _rms_rows(x, w):
    ms = jnp.mean(x * x, axis=-1, keepdims=True)
    return x * lax.rsqrt(ms + EPS) * w


def _resident(shape):
    nd = len(shape)
    return pl.BlockSpec(shape, lambda *_: (0,) * nd, pipeline_mode=pl.Buffered(1))


def _params(*semantics):
    return pltpu.CompilerParams(dimension_semantics=semantics, vmem_limit_bytes=VMEM_LIMIT)


def _inproj_hgrn_kernel(x_ref, nw_ref, w_ref, q_ref, fx_ref, i_ref, z_ref):
    d = x_ref.shape[1]
    h = _rms_rows(x_ref[...], nw_ref[...]).astype(BF16)
    q_ref[...] = (_dot(h, w_ref[:, 0:d]) * (HG_DK ** -0.5)).astype(q_ref.dtype)
    for c, o_ref in ((1, fx_ref), (2, i_ref), (3, z_ref)):
        o_ref[...] = _dot(h, w_ref[:, c * d:(c + 1) * d]).astype(o_ref.dtype)


def _inproj_hgrn(x2, nw, w_bf, tm):
    m, d = x2.shape
    row = pl.BlockSpec((tm, d), lambda i: (i, 0))
    return pl.pallas_call(
        _inproj_hgrn_kernel,
        grid=(m // tm,),
        in_specs=[row, _resident((1, d)), _resident((d, 4 * d))],
        out_specs=[row, row, row, row],
        out_shape=[jax.ShapeDtypeStruct((m, d), BF16), jax.ShapeDtypeStruct((m, d), F32),
                   jax.ShapeDtypeStruct((m, d), BF16), jax.ShapeDtypeStruct((m, d), BF16)],
        compiler_params=_params("parallel"),
        name="inproj_hgrn",
    )(x2, nw, w_bf)


def _inproj_attn_kernel(k_transposed, n_alias, x_ref, nw_ref, w_ref, qnw_ref, knw_ref, g_ref, *rest):
    q_ref, kb_ref, vb_ref, z_ref, ks_ref, vs_ref = rest[n_alias:]
    d = x_ref.shape[1]
    h = _rms_rows(x_ref[...], nw_ref[...]).astype(BF16)
    gmat = g_ref[...]
    gw = gmat.shape[0]

    def group_norm(y, w):
        cols = []
        for t in range(d // gw):
            yt = y[:, t * gw:(t + 1) * gw]
            ms = _dot((yt * yt).astype(BF16), gmat)
            cols.append(yt * lax.rsqrt(ms + EPS))
        return jnp.concatenate(cols, axis=1) * w

    q = group_norm(_dot(h, w_ref[:, 0:d]), qnw_ref[...])
    q_ref[...] = (q * (DA_HEAD_DIM ** -0.5 * LOG2E)).astype(BF16)
    k = group_norm(_dot(h, w_ref[:, d:2 * d]), knw_ref[...])
    kb_ref[...] = k.astype(BF16)
    ks_ref[0, 0] = k.T if k_transposed else k
    v = _dot(h, w_ref[:, 2 * d:3 * d])
    vs_ref[0] = v
    vb_ref[...] = v.astype(BF16)
    z_ref[...] = _dot(h, w_ref[:, 3 * d:4 * d]).astype(BF16)


def _inproj_attn(x2, seq, nw, w_bf, qnw, knw, gmat, layer_j, n_layers, k_stack, v_stack,
                 k_transposed, tm):
    m, d = x2.shape
    batch = m // seq
    tiles_per_seq = seq // tm
    row = pl.BlockSpec((tm, d), lambda i: (i, 0))
    if k_transposed:
        ks_shape = (n_layers, batch, d, seq)
        ks_spec = pl.BlockSpec((1, 1, d, tm),
                               lambda i: (layer_j, i // tiles_per_seq, 0, i % tiles_per_seq))
    else:
        ks_shape = (n_layers, batch, seq, d)
        ks_spec = pl.BlockSpec((1, 1, tm, d),
                               lambda i: (layer_j, i // tiles_per_seq, i % tiles_per_seq, 0))
    vs_spec = pl.BlockSpec((1, tm, d), lambda i: (layer_j, i, 0))
    in_specs = [row, _resident((1, d)), _resident((d, 4 * d)), _resident((1, d)),
                _resident((1, d)), _resident(gmat.shape)]
    args = [x2, nw, w_bf, qnw, knw, gmat]
    aliases = {}
    if k_stack is not None:
        in_specs += [pl.BlockSpec(memory_space=pl.ANY)] * 2
        aliases = {len(args): 4, len(args) + 1: 5}
        args += [k_stack, v_stack]
    shp = lambda dt: jax.ShapeDtypeStruct((m, d), dt)
    return pl.pallas_call(
        functools.partial(_inproj_attn_kernel, k_transposed, len(aliases)),
        grid=(m // tm,),
        in_specs=in_specs,
        out_specs=[row, row, row, row, ks_spec, vs_spec],
        out_shape=[shp(BF16), shp(BF16), shp(BF16), shp(BF16),
                   jax.ShapeDtypeStruct(ks_shape, F32),
                   jax.ShapeDtypeStruct((n_layers, m, d), F32)],
        input_output_aliases=aliases,
        compiler_params=_params("parallel"),
        name="inproj_attn",
    )(*args)


def _outproj_kernel(full_norm, o_ref, z_ref, x_ref, w_ref, ow_ref, y_ref):
    o = o_ref[...].astype(F32)
    if full_norm:
        o = _rms_rows(o, ow_ref[...])
    z = z_ref[...].astype(F32)
    gated = (o * (z * jax.nn.sigmoid(z))).astype(BF16)
    y_ref[...] = x_ref[...] + _dot(gated, w_ref[...])


def _outproj(o2, z2, x2, w_bf, ow, full_norm, tm):
    m, d = x2.shape
    row = pl.BlockSpec((tm, d), lambda i: (i, 0))
    return pl.pallas_call(
        functools.partial(_outproj_kernel, full_norm),
        grid=(m // tm,),
        in_specs=[row, row, row, _resident((d, d)), _resident((1, d))],
        out_specs=row,
        out_shape=jax.ShapeDtypeStruct((m, d), F32),
        compiler_params=_params("parallel"),
        name="outproj",
    )(o2, z2, x2, w_bf, ow)


def _range_matrices(c):
    t = jnp.arange(c)[:, None]
    s = jnp.arange(c)[None, :]
    blocks = [s <= t]
    for w in REC_MATMUL_LEVELS:
        ref = (t // (2 * w)) * (2 * w) + w - 1
        upper = (t & w) != 0
        blocks.append(jnp.where(upper, (s > ref) & (s <= t), (s > t) & (s <= ref)))
    e = jnp.concatenate(blocks, axis=0).astype(BF16)
    return jnp.concatenate([e, e], axis=1)


def _hgrn_rec_kernel(layer_j, has_s0, q_ref, fx_ref, v_ref, lbl_ref, emat_ref, *rest):
    if has_s0:
        s0_ref, o_ref, sfin_ref, st_ref = rest
    else:
        o_ref, sfin_ref, st_ref = rest
    c_idx = pl.program_id(1)
    c = q_ref.shape[1]
    d = q_ref.shape[2]

    @pl.when(c_idx == 0)
    def _():
        for h in range(HG_HEADS):
            if has_s0:
                st_ref[h] = s0_ref[0, h].T
            else:
                st_ref[h] = jnp.zeros((HG_DK, HG_DK), F32)

    lg = lbl_ref[...]
    ex = jnp.exp(lg - jnp.max(lg, axis=0, keepdims=True))
    p = ex / jnp.sum(ex, axis=0, keepdims=True)
    cs = p[0:1]
    for r in range(1, layer_j + 1):
        cs = cs + p[r:r + 1]
    lb = cs - p[0:1]

    f = lb + (1.0 - lb) * jax.nn.sigmoid(fx_ref[0])
    g = jnp.log2(f)
    kk = 1.0 - f
    g_hi = pltpu.bitcast(pltpu.bitcast(g, jnp.uint32) & jnp.uint32(0xFFFF0000), F32)
    g_pieces = jnp.concatenate([g_hi.astype(BF16), (g - g_hi).astype(BF16)], axis=0)
    ranges = _dot(emat_ref[...], g_pieces)
    b = ranges[0:c]

    q = q_ref[0].astype(F32)
    q_state = (q * jnp.exp2(b)).astype(BF16)
    k_state = (kk * jnp.exp2(b[c - 1:c] - b)).astype(BF16)
    state_decay = jnp.exp2(b[c - 1:c])

    row = lax.broadcasted_iota(jnp.int32, (c, d), 0)
    level_ops = []
    for li, w in enumerate(REC_MATMUL_LEVELS):
        x = jnp.exp2(ranges[(li + 1) * c:(li + 2) * c])
        level_ops.append((jnp.where((row & w) != 0, q, kk) * x).astype(BF16))
    for w in REC_ROW_LEVELS:
        pieces = []
        for blk in range(c // w):
            rows = slice(blk * w, (blk + 1) * w)
            mid = (blk // 2) * 2 * w + w - 1
            if blk % 2:
                pieces.append(q[rows] * jnp.exp2(b[rows] - b[mid:mid + 1]))
            else:
                pieces.append(kk[rows] * jnp.exp2(b[mid:mid + 1] - b[rows]))
        level_ops.append(jnp.concatenate(pieces, axis=0).astype(BF16))
    q_bf, k_bf = q.astype(BF16), kk.astype(BF16)

    tt = lax.broadcasted_iota(jnp.int32, (c, c), 0)
    ss = lax.broadcasted_iota(jnp.int32, (c, c), 1)
    diag_mask = tt == ss
    level_masks = [((tt // (2 * w)) == (ss // (2 * w))) & ((tt & w) != 0) & ((ss & w) == 0)
                   for w in REC_MATMUL_LEVELS + REC_ROW_LEVELS]

    heads = [slice(h * HG_DK, (h + 1) * HG_DK) for h in range(HG_HEADS)]
    att = []
    for sl in heads:
        a = jnp.where(diag_mask, _dot_nt(q_bf[:, sl], k_bf[:, sl]), 0.0)
        for mask, y in zip(level_masks, level_ops):
            a = jnp.where(mask, _dot_nt(y[:, sl], y[:, sl]), a)
        att.append(a.astype(BF16))
    for h, sl in enumerate(heads):
        o = _dot(att[h], v_ref[0, :, sl]) + _dot_nt(q_state[:, sl], st_ref[h].astype(BF16))
        o_ref[0, :, sl] = o.astype(o_ref.dtype)
    for h, sl in enumerate(heads):
        st_ref[h] = st_ref[h] * state_decay[:, sl] + _dot_tn(v_ref[0, :, sl], k_state[:, sl])

    @pl.when(c_idx == pl.num_programs(1) - 1)
    def _():
        for h in range(HG_HEADS):
            sfin_ref[0, h] = st_ref[h].T


def _hgrn_rec(q, fx, v, lb_logits, s0, layer_j):
    b, t, d = q.shape
    c = REC_CHUNK
    blk = pl.BlockSpec((1, c, d), lambda i, j: (i, j, 0))
    st_blk = pl.BlockSpec((1, HG_HEADS, HG_DK, HG_DK), lambda i, j: (i, 0, 0, 0))
    emat = _range_matrices(c)
    in_specs = [blk, blk, blk, _resident(lb_logits.shape), _resident(emat.shape)]
    args = [q, fx, v, lb_logits, emat]
    if s0 is not None:
        in_specs.append(st_blk)
        args.append(s0)
    return pl.pallas_call(
        functools.partial(_hgrn_rec_kernel, layer_j, s0 is not None),
        grid=(b, t // c),
        in_specs=in_specs,
        out_specs=[blk, st_blk],
        out_shape=[jax.ShapeDtypeStruct((b, t, d), BF16),
                   jax.ShapeDtypeStruct((b, HG_HEADS, HG_DK, HG_DK), F32)],
        scratch_shapes=[pltpu.VMEM((HG_HEADS, HG_DK, HG_DK), F32)],
        compiler_params=_params("parallel", "arbitrary"),
        name="hgrn_rec",
    )(*args)


def _masked_queries(q_refs):
    tq = q_refs[0].shape[1]
    lane = lax.broadcasted_iota(jnp.int32, (tq, LANES), 1)
    zero = jnp.zeros((tq, LANES), BF16)
    return [[jnp.where(lane < DA_HEAD_DIM, qr[0], zero), jnp.where(lane >= DA_HEAD_DIM, qr[0], zero)]
            for qr in q_refs]


def _lambda_scalar(lam_ref, lam_init):
    lv = lam_ref[...]
    return (jnp.exp(jnp.sum(lv[0:1] * lv[1:2], axis=1, keepdims=True))
            - jnp.exp(jnp.sum(lv[2:3] * lv[3:4], axis=1, keepdims=True)) + lam_init)


def _finish_transposed(lam_ref, subw_ref, lam_init, l_sc, acc_sc, o_ref):
    lam = _lambda_scalar(lam_ref, lam_init)
    for r in range(2):
        o = acc_sc[2 * r] / l_sc[2 * r] - lam * (acc_sc[2 * r + 1] / l_sc[2 * r + 1])
        ms = jnp.mean(o * o, axis=0, keepdims=True)
        o = o * lax.rsqrt(ms + EPS) * subw_ref[...] * (1.0 - lam_init)
        o_ref[0, :, r * LANES:(r + 1) * LANES] = o.T.astype(o_ref.dtype)


def _attn_fixed_kernel(lam_init, slopes_ref, shift_ref, lam_ref, subw_ref, q0_ref, q1_ref,
                       k0_ref, k1_ref, v_ref, o_ref, l_sc, acc_sc, below_sc, last_sc):
    j = pl.program_id(1)
    i = pl.program_id(2)
    t = q0_ref.shape[1]

    @pl.when(i == 0)
    def _():
        row = lax.broadcasted_iota(jnp.int32, (2 * t, t), 0)
        col = lax.broadcasted_iota(jnp.int32, (2 * t, t), 1)
        below_dist = (3 * t + col - row).astype(F32)
        last_dist = jnp.abs(t + col - row).astype(F32)
        visible = (row < t) | (((row - t) // MASK_CHUNK) <= (col // MASK_CHUNK))
        for r in range(2):
            slope = slopes_ref[2 * j + r]
            below_sc[:, r * t:(r + 1) * t] = -slope * below_dist - shift_ref[0]
            last_sc[:, r * t:(r + 1) * t] = jnp.where(
                visible, -slope * last_dist - shift_ref[0], NEG_INF)

    qm = _masked_queries((q0_ref, q1_ref))
    q_pair = [jnp.concatenate(qm[mp], axis=0) for mp in range(2)]
    k_refs = (k0_ref, k1_ref)
    l_sc[...] = jnp.zeros(l_sc.shape, F32)
    acc_sc[...] = jnp.zeros(acc_sc.shape, F32)

    def block(k_start, rows, bias_ref, bias_row0, far):
        s = [_dot_nt(k_refs[mp][0, pl.ds(k_start, rows), :], q_pair[mp]) for mp in range(2)]
        for r in range(2):
            cols = slice(r * t, (r + 1) * t)
            bias = bias_ref[bias_row0:bias_row0 + rows, cols]
            p = [jnp.exp2(s[mp][:, cols] + bias) for mp in range(2)]
            l_new = [jnp.sum(pm, axis=0, keepdims=True) for pm in p]
            vt = v_ref[0, pl.ds(k_start, rows), r * LANES:(r + 1) * LANES]
            acc_new = _dot_tn(vt, jnp.concatenate([pm.astype(BF16) for pm in p], axis=1))
            if far is not None:
                weight = jnp.exp2(jnp.full((1, t), -slopes_ref[2 * j + r] * far, F32))
                l_new = [ln * weight for ln in l_new]
                acc_new = acc_new * jnp.concatenate([weight, weight], axis=1)
            for mp in range(2):
                l_sc[mp, :, cols] = l_sc[mp, :, cols] + l_new[mp]
            acc_sc[r] = acc_sc[r] + acc_new

    def two_tiles(pair_idx, carry):
        k_start = pl.multiple_of(pair_idx * (2 * t), 2 * t)
        block(k_start, 2 * t, below_sc, 0, ((i - 2 * pair_idx - 3) * t).astype(F32))
        return carry

    lax.fori_loop(0, jnp.maximum(i - 1, 0) // 2, two_tiles, 0)

    @pl.when((i >= 2) & (i % 2 == 0))
    def _():
        block(pl.multiple_of((i - 2) * t, t), t, below_sc, t, None)

    @pl.when(i >= 1)
    def _():
        block(pl.multiple_of((i - 1) * t, t), 2 * t, last_sc, 0, None)

    @pl.when(i == 0)
    def _():
        block(0, t, last_sc, t, None)

    lam = _lambda_scalar(lam_ref, lam_init)
    for r in range(2):
        num = acc_sc[r]
        inv0 = 1.0 / l_sc[0][:, r * t:(r + 1) * t]
        inv1 = lam / l_sc[1][:, r * t:(r + 1) * t]
        o = num[:, :t] * inv0 - num[:, t:] * inv1
        ms = jnp.mean(o * o, axis=0, keepdims=True)
        o = o * lax.rsqrt(ms + EPS) * subw_ref[...] * (1.0 - lam_init)
        o_ref[0, :, r * LANES:(r + 1) * LANES] = o.T.astype(o_ref.dtype)


def _attn_online_kernel(lam_init, slopes_ref, lam_ref, subw_ref, q0_ref, q1_ref, k0_ref, k1_ref,
                        v_ref, o_ref, m_sc, l_sc, acc_sc):
    j = pl.program_id(1)
    i = pl.program_id(2)
    t = q0_ref.shape[1]
    qm = _masked_queries((q0_ref, q1_ref))
    k_refs = (k0_ref, k1_ref)
    m_sc[...] = jnp.full(m_sc.shape, -jnp.inf, F32)
    l_sc[...] = jnp.zeros(l_sc.shape, F32)
    acc_sc[...] = jnp.zeros(acc_sc.shape, F32)

    def tile(kt_idx, diagonal):
        k_start = pl.multiple_of(kt_idx * t, t)
        row = lax.broadcasted_iota(jnp.int32, (t, t), 0)
        col = lax.broadcasted_iota(jnp.int32, (t, t), 1)
        dist = jnp.abs((i - kt_idx) * t + (col - row)).astype(F32)
        if diagonal:
            visible = (row // MASK_CHUNK) <= (col // MASK_CHUNK)
        for r in range(2):
            bias = -slopes_ref[2 * j + r] * dist
            vt = v_ref[0, pl.ds(k_start, t), r * LANES:(r + 1) * LANES]
            for mp in range(2):
                idx = 2 * r + mp
                kt = k_refs[mp][0, pl.ds(k_start, t), :]
                s = _dot_nt(kt, qm[mp][r]) + bias
                if diagonal:
                    s = jnp.where(visible, s, NEG_INF)
                m_prev = m_sc[idx]
                m_new = jnp.maximum(m_prev, jnp.max(s, axis=0, keepdims=True))
                alpha = jnp.exp2(m_prev - m_new)
                p = jnp.exp2(s - m_new)
                l_sc[idx] = alpha * l_sc[idx] + jnp.sum(p, axis=0, keepdims=True)
                acc_sc[idx] = alpha * acc_sc[idx] + _dot_tn(vt, p.astype(BF16))
                m_sc[idx] = m_new

    def below(kt_idx, carry):
        tile(kt_idx, False)
        return carry

    lax.fori_loop(0, i, below, 0)
    tile(i, True)
    _finish_transposed(lam_ref, subw_ref, lam_init, l_sc, acc_sc, o_ref)


def _prompt_attention(q, k, v, lam_vec, subln_w, slopes2, score_bound2, lam_init):
    b, t, d = q.shape
    tile = ATTN_TILE
    half = d // (2 * LANES)
    q_spec = lambda off: pl.BlockSpec((1, tile, LANES), lambda bi, j, i: (bi, i, j + off))
    k_spec = lambda off: pl.BlockSpec((1, t, LANES), lambda bi, j, i: (bi, 0, j + off))
    v_spec = pl.BlockSpec((1, t, 2 * LANES), lambda bi, j, i: (bi, 0, j))
    o_spec = pl.BlockSpec((1, tile, 2 * LANES), lambda bi, j, i: (bi, i, j))
    smem = pl.BlockSpec(memory_space=pltpu.SMEM)
    lam_spec = pl.BlockSpec(lam_vec.shape, lambda bi, j, i: (0, 0))
    subw_spec = pl.BlockSpec((2 * DA_HEAD_DIM, 1), lambda bi, j, i: (0, 0))
    data_specs = [q_spec(0), q_spec(half), k_spec(0), k_spec(half), v_spec]
    subw = subln_w.reshape(2 * DA_HEAD_DIM, 1)
    stat = pltpu.VMEM((4, 1, tile), F32)
    acc = pltpu.VMEM((4, 2 * DA_HEAD_DIM, tile), F32)
    common = dict(grid=(b, half, t // tile), out_specs=o_spec,
                  out_shape=jax.ShapeDtypeStruct((b, t, d), BF16),
                  compiler_params=_params("parallel", "parallel", "arbitrary"))

    def fixed(shift):
        return pl.pallas_call(
            functools.partial(_attn_fixed_kernel, lam_init),
            in_specs=[smem, smem, lam_spec, subw_spec] + data_specs,
            scratch_shapes=[pltpu.VMEM((2, 1, 2 * tile), F32),
                            pltpu.VMEM((2, 2 * DA_HEAD_DIM, 2 * tile), F32),
                            pltpu.VMEM((2 * tile, 2 * tile), F32),
                            pltpu.VMEM((2 * tile, 2 * tile), F32)],
            name="diff_attn_fixed", **common,
        )(slopes2, shift, lam_vec, subw, q, q, k, k, v)

    def online(_):
        return pl.pallas_call(
            functools.partial(_attn_online_kernel, lam_init),
            in_specs=[smem, lam_spec, subw_spec] + data_specs,
            scratch_shapes=[stat, stat, acc],
            name="diff_attn_online", **common,
        )(slopes2, lam_vec, subw, q, q, k, k, v)

    return lax.cond(score_bound2[0] <= FIXED_SHIFT_MAX_SCORE * LOG2E, fixed, online, score_bound2)


def _sample_attn_kernel(lam_init, past_visible, slopes_ref, lam_ref, subw_ref, q_ref, kc_ref,
                        kn_ref, vc_ref, vn_ref, o_ref):
    tq = q_ref.shape[1]
    past = kc_ref.shape[3]
    half = DA_HEADS // 2
    lam = _lambda_scalar(lam_ref, lam_init)
    lane = lax.broadcasted_iota(jnp.int32, (tq, LANES), 1)
    zero = jnp.zeros((tq, LANES), BF16)

    row_c = lax.broadcasted_iota(jnp.int32, (tq, past), 0)
    col_c = lax.broadcasted_iota(jnp.int32, (tq, past), 1)
    dist_c = (past + row_c - col_c).astype(F32)
    row_n = lax.broadcasted_iota(jnp.int32, (tq, tq), 0)
    col_n = lax.broadcasted_iota(jnp.int32, (tq, tq), 1)
    dist_n = jnp.abs(row_n - col_n).astype(F32)
    if not past_visible:
        vis_c = (col_c // MASK_CHUNK) <= ((past + row_c) // MASK_CHUNK)
        vis_n = ((past + col_n) // MASK_CHUNK) <= ((past + row_n) // MASK_CHUNK)

    for pair in range(half):
        tiles = [slice((mp * half + pair) * LANES, (mp * half + pair + 1) * LANES) for mp in range(2)]
        kc = [kc_ref[mp, 2 * pair:2 * pair + 2].reshape(LANES, past).astype(BF16) for mp in range(2)]
        for r in range(2):
            head = 2 * pair + r
            slope = slopes_ref[head]
            keep = (lane >= DA_HEAD_DIM) if r else (lane < DA_HEAD_DIM)
            vc = vc_ref[pl.ds(head, past, stride=DA_HEADS), :].astype(BF16)
            vn = vn_ref[0, :, head * LANES:(head + 1) * LANES]
            outs = []
            for mp in range(2):
                qm = jnp.where(keep, q_ref[0, :, tiles[mp]], zero)
                s_c = _dot(qm, kc[mp]) - slope * dist_c
                s_n = _dot_nt(qm, kn_ref[0, :, tiles[mp]]) - slope * dist_n
                if not past_visible:
                    s_c = jnp.where(vis_c, s_c, NEG_INF)
                    s_n = jnp.where(vis_n, s_n, NEG_INF)
                m = jnp.maximum(jnp.max(s_c, axis=1, keepdims=True),
                                jnp.max(s_n, axis=1, keepdims=True))
                p_c = jnp.exp2(s_c - m)
                p_n = jnp.exp2(s_n - m)
                l = jnp.sum(p_c, axis=1, keepdims=True) + jnp.sum(p_n, axis=1, keepdims=True)
                acc = _dot(p_c.astype(BF16), vc) + _dot(p_n.astype(BF16), vn)
                outs.append(acc / l)
            o = outs[0] - lam * outs[1]
            ms = jnp.mean(o * o, axis=1, keepdims=True)
            o = o * lax.rsqrt(ms + EPS) * subw_ref[...] * (1.0 - lam_init)
            o_ref[0, :, head * LANES:(head + 1) * LANES] = o.astype(o_ref.dtype)


def _sample_attention(q, k_new, v_new, k_cache, v_cache, layer_j, lam_vec, subln_w, slopes2,
                      lam_init):
    b, t_q, d = q.shape
    past = k_cache.shape[5]
    past_visible = past % MASK_CHUNK == 0 and t_q <= MASK_CHUNK
    row = pl.BlockSpec((1, t_q, d), lambda bi: (bi, 0, 0))
    return pl.pallas_call(
        functools.partial(_sample_attn_kernel, lam_init, past_visible),
        grid=(b,),
        in_specs=[pl.BlockSpec(memory_space=pltpu.SMEM),
                  pl.BlockSpec(lam_vec.shape, lambda bi: (0, 0)),
                  pl.BlockSpec((1, 2 * DA_HEAD_DIM), lambda bi: (0, 0)),
                  row,
                  pl.BlockSpec((None, None, 2, DA_HEADS, DA_HEAD_DIM, past),
                               lambda bi: (layer_j, bi, 0, 0, 0, 0)),
                  row,
                  pl.BlockSpec((None, None, past * DA_HEADS, 2 * DA_HEAD_DIM),
                               lambda bi: (layer_j, bi, 0, 0)),
                  row],
        out_specs=row,
        out_shape=jax.ShapeDtypeStruct((b, t_q, d), BF16),
        compiler_params=_params("parallel"),
        name="diff_attn_sample",
    )(slopes2, lam_vec, subln_w.reshape(1, 2 * DA_HEAD_DIM), q, k_cache, k_new, v_cache, v_new)


def _row_tile(m):
    return 512 if m % 512 == 0 else 256


def _hgrn_layer(x, s0, nw, w_in_bf, lb_logits, onw, w_out_bf, layer_j):
    b, t, d = x.shape
    x2 = x.reshape(b * t, d)
    tm = _row_tile(b * t)
    q, fx, iv, z = _inproj_hgrn(x2, nw.reshape(1, d), w_in_bf, tm)
    o, s_new = _hgrn_rec(q.reshape(b, t, d), fx.reshape(b, t, d), iv.reshape(b, t, d),
                         lb_logits, s0, layer_j)
    y = _outproj(o.reshape(b * t, d), z, x2, w_out_bf, onw.reshape(1, d), True, tm)
    return y.reshape(b, t, d), s_new


def _attn_layer(x, cache, nw, w_in_bf, qn_w, kn_w, lam_vec, subln_w, w_out_bf, layer_idx,
                layer_j, n_layers, k_stack, v_stack):
    b, t, d = x.shape
    x2 = x.reshape(b * t, d)
    groups = d // DA_HEAD_DIM
    gmat = jnp.kron(jnp.eye(MXU_COLS // DA_HEAD_DIM, dtype=F32),
                    jnp.full((DA_HEAD_DIM, DA_HEAD_DIM), 1.0 / DA_HEAD_DIM, F32)).astype(BF16)
    prompt = cache is None
    tm = 512 if t % 512 == 0 else t
    q, kb, vb, z, k_stack, v_stack = _inproj_attn(
        x2, t, nw.reshape(1, d), w_in_bf, jnp.tile(qn_w, groups).reshape(1, d),
        jnp.tile(kn_w, groups).reshape(1, d), gmat, layer_j, n_layers, k_stack, v_stack, prompt, tm)
    lam_init = 0.8 - 0.6 * math.exp(-0.3 * layer_idx)
    slopes2 = jnp.exp2(-8.0 * jnp.arange(1, DA_HEADS + 1, dtype=F32) / DA_HEADS) * LOG2E
    q, kb, vb = (a.reshape(b, t, d) for a in (q, kb, vb))
    if prompt:
        bound2 = (DA_HEAD_DIM ** 0.5 * LOG2E * 1.02) * jnp.max(jnp.abs(qn_w)) * jnp.max(jnp.abs(kn_w))
        o = _prompt_attention(q, kb, vb, lam_vec, subln_w, slopes2, bound2.reshape(1), lam_init)
    else:
        k_cache, v_cache = cache
        o = _sample_attention(q, kb, vb, k_cache, v_cache, layer_j, lam_vec, subln_w, slopes2,
                              lam_init)
    y = _outproj(o.reshape(b * t, d), z, x2, w_out_bf, jnp.ones((1, d), F32), False,
                 _row_tile(b * t))
    return y.reshape(b, t, d), k_stack, v_stack


def kernel(x_prompt, x_sample, cache_k, cache_v, state_hgrn, norm_w, hgrn_w_in, hgrn_lb_logits,
           hgrn_onorm_w, hgrn_w_out, attn_w_in, attn_q_norm, attn_k_norm, attn_lambda, attn_subln,
           attn_w_out):
    depth = norm_w.shape[0]
    n_attn = cache_k.shape[0]
    bp, tp, d = x_prompt.shape
    bs, ts, _ = x_sample.shape
    past = cache_k.shape[2]
    cache_k_t = jnp.transpose(cache_k, (0, 1, 3, 4, 5, 2))
    cache_v2 = cache_v.reshape(n_attn, bs, past * DA_HEADS, 2 * DA_HEAD_DIM)
    yp, ys = x_prompt, x_sample
    kp = vp = ks_ = vs_ = None
    sp, ss = [], []
    for l in range(depth):
        j = l // N_MIXERS
        if l % N_MIXERS == 0:
            w_in = hgrn_w_in[j].astype(BF16)
            w_out = hgrn_w_out[j].astype(BF16)
            yp, s_p = _hgrn_layer(yp, None, norm_w[l], w_in, hgrn_lb_logits, hgrn_onorm_w[j],
                                  w_out, j)
            ys, s_s = _hgrn_layer(ys, state_hgrn[j], norm_w[l], w_in, hgrn_lb_logits,
                                  hgrn_onorm_w[j], w_out, j)
            sp.append(s_p)
            ss.append(s_s)
        else:
            w_in = attn_w_in[j].astype(BF16)
            w_out = attn_w_out[j].astype(BF16)
            yp, kp, vp = _attn_layer(yp, None, norm_w[l], w_in, attn_q_norm[j], attn_k_norm[j],
                                     attn_lambda[j], attn_subln[j], w_out, l, j, n_attn, kp, vp)
            ys, ks_, vs_ = _attn_layer(ys, (cache_k_t, cache_v2), norm_w[l], w_in,
                                       attn_q_norm[j], attn_k_norm[j], attn_lambda[j],
                                       attn_subln[j], w_out, l, j, n_attn, ks_, vs_)
    new_k_prompt = jnp.transpose(kp.reshape(n_attn, bp, 2, DA_HEADS, DA_HEAD_DIM, tp),
                                 (0, 1, 5, 2, 3, 4))
    return (yp, ys, new_k_prompt,
            vp.reshape(n_attn, bp, tp, DA_HEADS, 2 * DA_HEAD_DIM),
            ks_.reshape(n_attn, bs, ts, 2, DA_HEADS, DA_HEAD_DIM),
            vs_.reshape(n_attn, bs, ts, DA_HEADS, 2 * DA_HEAD_DIM),
            jnp.stack(sp), jnp.stack(ss))
```

```python
import functools
import math

import jax
import jax.numpy as jnp
from jax import lax
from jax.experimental import pallas as pl
from jax.experimental.pallas import tpu as pltpu

F32 = jnp.float32
BF16 = jnp.bfloat16

EPS = 1e-6
NEG_INF = -1e30
LOG2E = 1.4426950408889634
LANES = 128
MXU_COLS = 256
HG_HEADS = 8
HG_DK = 128
DA_HEADS = 8
DA_HEAD_DIM = 64
MASK_CHUNK = 64
N_MIXERS = 2

REC_CHUNK = 64
REC_CHUNKS_PER_STEP = 4
REC_MATMUL_LEVELS = (1, 2, 4)
REC_ROW_LEVELS = (8, 16, 32)
ATTN_TILE = 512
FIXED_SHIFT_MAX_SCORE = 40.0
VMEM_LIMIT = 56 * 1024 * 1024


def _dot(a, b):
    return jnp.dot(a, b, preferred_element_type=F32)


def _dot_nt(a, b):
    return lax.dot_general(a, b, (((1,), (1,)), ((), ())), preferred_element_type=F32)


def _dot_tn(a, b):
    return lax.dot_general(a, b, (((0,), (0,)), ((), ())), preferred_element_type=F32)


def _rms_rows(x, w):
    ms = jnp.mean(x * x, axis=-1, keepdims=True)
    return x * lax.rsqrt(ms + EPS) * w


def _resident(shape):
    nd = len(shape)
    return pl.BlockSpec(shape, lambda *_: (0,) * nd, pipeline_mode=pl.Buffered(1))


def _params(*semantics):
    return pltpu.CompilerParams(dimension_semantics=semantics, vmem_limit_bytes=VMEM_LIMIT)


def _inproj_hgrn_kernel(x_ref, nw_ref, w_ref, q_ref, fx_ref, i_ref, z_ref):
    d = x_ref.shape[1]
    h = _rms_rows(x_ref[...], nw_ref[...]).astype(BF16)
    q_ref[...] = (_dot(h, w_ref[:, 0:d]) * (HG_DK ** -0.5)).astype(q_ref.dtype)
    for c, o_ref in ((1, fx_ref), (2, i_ref), (3, z_ref)):
        o_ref[...] = _dot(h, w_ref[:, c * d:(c + 1) * d]).astype(o_ref.dtype)


def _inproj_hgrn(x2, nw, w_bf, tm):
    m, d = x2.shape
    row = pl.BlockSpec((tm, d), lambda i: (i, 0))
    return pl.pallas_call(
        _inproj_hgrn_kernel,
        grid=(m // tm,),
        in_specs=[row, _resident((1, d)), _resident((d, 4 * d))],
        out_specs=[row, row, row, row],
        out_shape=[jax.ShapeDtypeStruct((m, d), BF16), jax.ShapeDtypeStruct((m, d), F32),
                   jax.ShapeDtypeStruct((m, d), BF16), jax.ShapeDtypeStruct((m, d), BF16)],
        compiler_params=_params("parallel"),
        name="inproj_hgrn",
    )(x2, nw, w_bf)


def _inproj_attn_kernel(k_transposed, n_alias, x_ref, nw_ref, w_ref, qnw_ref, knw_ref, g_ref, *rest):
    q_ref, kb_ref, vb_ref, z_ref, ks_ref, vs_ref = rest[n_alias:]
    d = x_ref.shape[1]
    h = _rms_rows(x_ref[...], nw_ref[...]).astype(BF16)
    gmat = g_ref[...]
    gw = gmat.shape[0]

    def group_norm(y, w):
        cols = []
        for t in range(d // gw):
            yt = y[:, t * gw:(t + 1) * gw]
            ms = _dot((yt * yt).astype(BF16), gmat)
            cols.append(yt * lax.rsqrt(ms + EPS))
        return jnp.concatenate(cols, axis=1) * w

    q = group_norm(_dot(h, w_ref[:, 0:d]), qnw_ref[...])
    q_ref[...] = (q * (DA_HEAD_DIM ** -0.5 * LOG2E)).astype(BF16)
    k = group_norm(_dot(h, w_ref[:, d:2 * d]), knw_ref[...])
    kb_ref[...] = k.astype(BF16)
    ks_ref[0, 0] = k.T if k_transposed else k
    v = _dot(h, w_ref[:, 2 * d:3 * d])
    vs_ref[0] = v
    vb_ref[...] = v.astype(BF16)
    z_ref[...] = _dot(h, w_ref[:, 3 * d:4 * d]).astype(BF16)


def _inproj_attn(x2, seq, nw, w_bf, qnw, knw, gmat, layer_j, n_layers, k_stack, v_stack,
                 k_transposed, tm):
    m, d = x2.shape
    batch = m // seq
    tiles_per_seq = seq // tm
    row = pl.BlockSpec((tm, d), lambda i: (i, 0))
    if k_transposed:
        ks_shape = (n_layers, batch, d, seq)
        ks_spec = pl.BlockSpec((1, 1, d, tm),
                               lambda i: (layer_j, i // tiles_per_seq, 0, i % tiles_per_seq))
    else:
        ks_shape = (n_layers, batch, seq, d)
        ks_spec = pl.BlockSpec((1, 1, tm, d),
                               lambda i: (layer_j, i // tiles_per_seq, i % tiles_per_seq, 0))
    vs_spec = pl.BlockSpec((1, tm, d), lambda i: (layer_j, i, 0))
    in_specs = [row, _resident((1, d)), _resident((d, 4 * d)), _resident((1, d)),
                _resident((1, d)), _resident(gmat.shape)]
    args = [x2, nw, w_bf, qnw, knw, gmat]
    aliases = {}
    if k_stack is not None:
        in_specs += [pl.BlockSpec(memory_space=pl.ANY)] * 2
        aliases = {len(args): 4, len(args) + 1: 5}
        args += [k_stack, v_stack]
    shp = lambda dt: jax.ShapeDtypeStruct((m, d), dt)
    return pl.pallas_call(
        functools.partial(_inproj_attn_kernel, k_transposed, len(aliases)),
        grid=(m // tm,),
        in_specs=in_specs,
        out_specs=[row, row, row, row, ks_spec, vs_spec],
        out_shape=[shp(BF16), shp(BF16), shp(BF16), shp(BF16),
                   jax.ShapeDtypeStruct(ks_shape, F32),
                   jax.ShapeDtypeStruct((n_layers, m, d), F32)],
        input_output_aliases=aliases,
        compiler_params=_params("parallel"),
        name="inproj_attn",
    )(*args)


def _outproj_kernel(full_norm, o_ref, z_ref, x_ref, w_ref, ow_ref, y_ref):
    o = o_ref[...].astype(F32)
    if full_norm:
        o = _rms_rows(o, ow_ref[...])
    z = z_ref[...].astype(F32)
    gated = (o * (z * jax.nn.sigmoid(z))).astype(BF16)
    y_ref[...] = x_ref[...] + _dot(gated, w_ref[...])


def _outproj(o2, z2, x2, w_bf, ow, full_norm, tm):
    m, d = x2.shape
    row = pl.BlockSpec((tm, d), lambda i: (i, 0))
    return pl.pallas_call(
        functools.partial(_outproj_kernel, full_norm),
        grid=(m // tm,),
        in_specs=[row, row, row, _resident((d, d)), _resident((1, d))],
        out_specs=row,
        out_shape=jax.ShapeDtypeStruct((m, d), F32),
        compiler_params=_params("parallel"),
        name="outproj",
    )(o2, z2, x2, w_bf, ow)


def _range_matrices(c):
    t = jnp.arange(c)[:, None]
    s = jnp.arange(c)[None, :]
    blocks = [s <= t]
    for w in REC_MATMUL_LEVELS:
        ref = (t // (2 * w)) * (2 * w) + w - 1
        upper = (t & w) != 0
        blocks.append(jnp.where(upper, (s > ref) & (s <= t), (s > t) & (s <= ref)))
    e = jnp.concatenate(blocks, axis=0).astype(BF16)
    return jnp.concatenate([e, e], axis=1)


def _hgrn_rec_kernel(layer_j, has_s0, q_ref, fx_ref, v_ref, lbl_ref, emat_ref, *rest):
    if has_s0:
        s0_ref, o_ref, sfin_ref, st_ref = rest
    else:
        o_ref, sfin_ref, st_ref = rest
    c_idx = pl.program_id(1)
    c = REC_CHUNK
    n_chunks = q_ref.shape[1] // c
    d = q_ref.shape[2]

    @pl.when(c_idx == 0)
    def _():
        for h in range(HG_HEADS):
            if has_s0:
                st_ref[h] = s0_ref[0, h].T
            else:
                st_ref[h] = jnp.zeros((HG_DK, HG_DK), F32)

    lg = lbl_ref[...]
    ex = jnp.exp(lg - jnp.max(lg, axis=0, keepdims=True))
    p = ex / jnp.sum(ex, axis=0, keepdims=True)
    cs = p[0:1]
    for r in range(1, layer_j + 1):
        cs = cs + p[r:r + 1]
    lb = cs - p[0:1]

    row = lax.broadcasted_iota(jnp.int32, (c, d), 0)
    tt = lax.broadcasted_iota(jnp.int32, (c, c), 0)
    ss = lax.broadcasted_iota(jnp.int32, (c, c), 1)
    diag_mask = tt == ss
    level_masks = [((tt // (2 * w)) == (ss // (2 * w))) & ((tt & w) != 0) & ((ss & w) == 0)
                   for w in REC_MATMUL_LEVELS + REC_ROW_LEVELS]
    heads = [slice(h * HG_DK, (h + 1) * HG_DK) for h in range(HG_HEADS)]

    def operands(rows_c, after):
        lb_c = lb if after is None else lb + 0.0 * after
        f = lb_c + (1.0 - lb_c) * jax.nn.sigmoid(fx_ref[0, rows_c, :])
        g = jnp.log2(f)
        kk = 1.0 - f
        g_hi = pltpu.bitcast(pltpu.bitcast(g, jnp.uint32) & jnp.uint32(0xFFFF0000), F32)
        g_pieces = jnp.concatenate([g_hi.astype(BF16), (g - g_hi).astype(BF16)], axis=0)
        ranges = _dot(emat_ref[...], g_pieces)
        b = ranges[0:c]

        q = q_ref[0, rows_c, :].astype(F32)
        q_state = (q * jnp.exp2(b)).astype(BF16)
        k_state = (kk * jnp.exp2(b[c - 1:c] - b)).astype(BF16)
        state_decay = jnp.exp2(b[c - 1:c])

        level_ops = []
        for li, w in enumerate(REC_MATMUL_LEVELS):
            x = jnp.exp2(ranges[(li + 1) * c:(li + 2) * c])
            level_ops.append((jnp.where((row & w) != 0, q, kk) * x).astype(BF16))
        for w in REC_ROW_LEVELS:
            pieces = []
            for blk in range(c // w):
                rows = slice(blk * w, (blk + 1) * w)
                mid = (blk // 2) * 2 * w + w - 1
                if blk % 2:
                    pieces.append(q[rows] * jnp.exp2(b[rows] - b[mid:mid + 1]))
                else:
                    pieces.append(kk[rows] * jnp.exp2(b[mid:mid + 1] - b[rows]))
            level_ops.append(jnp.concatenate(pieces, axis=0).astype(BF16))
        done = (jnp.max(level_ops[-1], axis=0, keepdims=True).astype(F32)
                + jnp.max(q_state, axis=0, keepdims=True).astype(F32))
        return (q.astype(BF16), kk.astype(BF16), level_ops, q_state, k_state, state_decay), done

    def recur(rows_c, ops):
        q_bf, k_bf, level_ops, q_state, k_state, state_decay = ops
        att = []
        for sl in heads:
            a = jnp.where(diag_mask, _dot_nt(q_bf[:, sl], k_bf[:, sl]), 0.0)
            for mask, y in zip(level_masks, level_ops):
                a = jnp.where(mask, _dot_nt(y[:, sl], y[:, sl]), a)
            att.append(a.astype(BF16))
        for h, sl in enumerate(heads):
            o = (_dot(att[h], v_ref[0, rows_c, sl])
                 + _dot_nt(q_state[:, sl], st_ref[h].astype(BF16)))
            o_ref[0, rows_c, sl] = o.astype(o_ref.dtype)
        for h, sl in enumerate(heads):
            st_ref[h] = (st_ref[h] * state_decay[:, sl]
                         + _dot_tn(v_ref[0, rows_c, sl], k_state[:, sl]))

    chunk_rows = [slice(n * c, (n + 1) * c) for n in range(n_chunks)]
    ops, done = operands(chunk_rows[0], None)
    for n in range(n_chunks):
        ops_next = None
        if n + 1 < n_chunks:
            ops_next, done = operands(chunk_rows[n + 1], done)
        recur(chunk_rows[n], ops)
        ops = ops_next

    @pl.when(c_idx == pl.num_programs(1) - 1)
    def _():
        for h in range(HG_HEADS):
            sfin_ref[0, h] = st_ref[h].T


def _hgrn_rec(q, fx, v, lb_logits, s0, layer_j):
    b, t, d = q.shape
    c = min(t, REC_CHUNK * REC_CHUNKS_PER_STEP)
    blk = pl.BlockSpec((1, c, d), lambda i, j: (i, j, 0))
    st_blk = pl.BlockSpec((1, HG_HEADS, HG_DK, HG_DK), lambda i, j: (i, 0, 0, 0))
    emat = _range_matrices(REC_CHUNK)
    in_specs = [blk, blk, blk, _resident(lb_logits.shape), _resident(emat.shape)]
    args = [q, fx, v, lb_logits, emat]
    if s0 is not None:
        in_specs.append(st_blk)
        args.append(s0)
    return pl.pallas_call(
        functools.partial(_hgrn_rec_kernel, layer_j, s0 is not None),
        grid=(b, t // c),
        in_specs=in_specs,
        out_specs=[blk, st_blk],
        out_shape=[jax.ShapeDtypeStruct((b, t, d), BF16),
                   jax.ShapeDtypeStruct((b, HG_HEADS, HG_DK, HG_DK), F32)],
        scratch_shapes=[pltpu.VMEM((HG_HEADS, HG_DK, HG_DK), F32)],
        compiler_params=_params("parallel", "arbitrary"),
        name="hgrn_rec",
    )(*args)


def _masked_queries(q_refs):
    tq = q_refs[0].shape[1]
    lane = lax.broadcasted_iota(jnp.int32, (tq, LANES), 1)
    zero = jnp.zeros((tq, LANES), BF16)
    return [[jnp.where(lane < DA_HEAD_DIM, qr[0], zero), jnp.where(lane >= DA_HEAD_DIM, qr[0], zero)]
            for qr in q_refs]


def _lambda_scalar(lam_ref, lam_init):
    lv = lam_ref[...]
    return (jnp.exp(jnp.sum(lv[0:1] * lv[1:2], axis=1, keepdims=True))
            - jnp.exp(jnp.sum(lv[2:3] * lv[3:4], axis=1, keepdims=True)) + lam_init)


def _finish_transposed(lam_ref, subw_ref, lam_init, l_sc, acc_sc, o_ref):
    lam = _lambda_scalar(lam_ref, lam_init)
    for r in range(2):
        o = acc_sc[2 * r] / l_sc[2 * r] - lam * (acc_sc[2 * r + 1] / l_sc[2 * r + 1])
        ms = jnp.mean(o * o, axis=0, keepdims=True)
        o = o * lax.rsqrt(ms + EPS) * subw_ref[...] * (1.0 - lam_init)
        o_ref[0, :, r * LANES:(r + 1) * LANES] = o.T.astype(o_ref.dtype)


def _attn_fixed_kernel(lam_init, slopes_ref, shift_ref, lam_ref, subw_ref, q0_ref, q1_ref,
                       k0_ref, k1_ref, v_ref, o_ref, l_sc, acc_sc, below_sc, last_sc):
    j = pl.program_id(1)
    i = pl.program_id(2)
    t = q0_ref.shape[1]

    @pl.when(i == 0)
    def _():
        row = lax.broadcasted_iota(jnp.int32, (2 * t, t), 0)
        col = lax.broadcasted_iota(jnp.int32, (2 * t, t), 1)
        below_dist = (3 * t + col - row).astype(F32)
        last_dist = jnp.abs(t + col - row).astype(F32)
        visible = (row < t) | (((row - t) // MASK_CHUNK) <= (col // MASK_CHUNK))
        for r in range(2):
            slope = slopes_ref[2 * j + r]
            below_sc[:, r * t:(r + 1) * t] = -slope * below_dist - shift_ref[0]
            last_sc[:, r * t:(r + 1) * t] = jnp.where(
                visible, -slope * last_dist - shift_ref[0], NEG_INF)

    qm = _masked_queries((q0_ref, q1_ref))
    q_pair = [jnp.concatenate(qm[mp], axis=0) for mp in range(2)]
    k_refs = (k0_ref, k1_ref)
    l_sc[...] = jnp.zeros(l_sc.shape, F32)
    acc_sc[...] = jnp.zeros(acc_sc.shape, F32)

    def block(k_start, rows, bias_ref, bias_row0, far):
        s = [_dot_nt(k_refs[mp][0, pl.ds(k_start, rows), :], q_pair[mp]) for mp in range(2)]
        for r in range(2):
            cols = slice(r * t, (r + 1) * t)
            bias = bias_ref[bias_row0:bias_row0 + rows, cols]
            p = [jnp.exp2(s[mp][:, cols] + bias) for mp in range(2)]
            l_new = [jnp.sum(pm, axis=0, keepdims=True) for pm in p]
            vt = v_ref[0, pl.ds(k_start, rows), r * LANES:(r + 1) * LANES]
            acc_new = _dot_tn(vt, jnp.concatenate([pm.astype(BF16) for pm in p], axis=1))
            if far is not None:
                weight = jnp.exp2(jnp.full((1, t), -slopes_ref[2 * j + r] * far, F32))
                l_new = [ln * weight for ln in l_new]
                acc_new = acc_new * jnp.concatenate([weight, weight], axis=1)
            for mp in range(2):
                l_sc[mp, :, cols] = l_sc[mp, :, cols] + l_new[mp]
            acc_sc[r] = acc_sc[r] + acc_new

    def two_tiles(pair_idx, carry):
        k_start = pl.multiple_of(pair_idx * (2 * t), 2 * t)
        block(k_start, 2 * t, below_sc, 0, ((i - 2 * pair_idx - 3) * t).astype(F32))
        return carry

    lax.fori_loop(0, jnp.maximum(i - 1, 0) // 2, two_tiles, 0)

    @pl.when((i >= 2) & (i % 2 == 0))
    def _():
        block(pl.multiple_of((i - 2) * t, t), t, below_sc, t, None)

    @pl.when(i >= 1)
    def _():
        block(pl.multiple_of((i - 1) * t, t), 2 * t, last_sc, 0, None)

    @pl.when(i == 0)
    def _():
        block(0, t, last_sc, t, None)

    lam = _lambda_scalar(lam_ref, lam_init)
    for r in range(2):
        num = acc_sc[r]
        inv0 = 1.0 / l_sc[0][:, r * t:(r + 1) * t]
        inv1 = lam / l_sc[1][:, r * t:(r + 1) * t]
        o = num[:, :t] * inv0 - num[:, t:] * inv1
        ms = jnp.mean(o * o, axis=0, keepdims=True)
        o = o * lax.rsqrt(ms + EPS) * subw_ref[...] * (1.0 - lam_init)
        o_ref[0, :, r * LANES:(r + 1) * LANES] = o.T.astype(o_ref.dtype)


def _attn_online_kernel(lam_init, slopes_ref, lam_ref, subw_ref, q0_ref, q1_ref, k0_ref, k1_ref,
                        v_ref, o_ref, m_sc, l_sc, acc_sc):
    j = pl.program_id(1)
    i = pl.program_id(2)
    t = q0_ref.shape[1]
    qm = _masked_queries((q0_ref, q1_ref))
    k_refs = (k0_ref, k1_ref)
    m_sc[...] = jnp.full(m_sc.shape, -jnp.inf, F32)
    l_sc[...] = jnp.zeros(l_sc.shape, F32)
    acc_sc[...] = jnp.zeros(acc_sc.shape, F32)

    def tile(kt_idx, diagonal):
        k_start = pl.multiple_of(kt_idx * t, t)
        row = lax.broadcasted_iota(jnp.int32, (t, t), 0)
        col = lax.broadcasted_iota(jnp.int32, (t, t), 1)
        dist = jnp.abs((i - kt_idx) * t + (col - row)).astype(F32)
        if diagonal:
            visible = (row // MASK_CHUNK) <= (col // MASK_CHUNK)
        for r in range(2):
            bias = -slopes_ref[2 * j + r] * dist
            vt = v_ref[0, pl.ds(k_start, t), r * LANES:(r + 1) * LANES]
            for mp in range(2):
                idx = 2 * r + mp
                kt = k_refs[mp][0, pl.ds(k_start, t), :]
                s = _dot_nt(kt, qm[mp][r]) + bias
                if diagonal:
                    s = jnp.where(visible, s, NEG_INF)
                m_prev = m_sc[idx]
                m_new = jnp.maximum(m_prev, jnp.max(s, axis=0, keepdims=True))
                alpha = jnp.exp2(m_prev - m_new)
                p = jnp.exp2(s - m_new)
                l_sc[idx] = alpha * l_sc[idx] + jnp.sum(p, axis=0, keepdims=True)
                acc_sc[idx] = alpha * acc_sc[idx] + _dot_tn(vt, p.astype(BF16))
                m_sc[idx] = m_new

    def below(kt_idx, carry):
        tile(kt_idx, False)
        return carry

    lax.fori_loop(0, i, below, 0)
    tile(i, True)
    _finish_transposed(lam_ref, subw_ref, lam_init, l_sc, acc_sc, o_ref)


def _prompt_attention(q, k, v, lam_vec, subln_w, slopes2, score_bound2, lam_init):
    b, t, d = q.shape
    tile = ATTN_TILE
    half = d // (2 * LANES)
    q_spec = lambda off: pl.BlockSpec((1, tile, LANES), lambda bi, j, i: (bi, i, j + off))
    k_spec = lambda off: pl.BlockSpec((1, t, LANES), lambda bi, j, i: (bi, 0, j + off))
    v_spec = pl.BlockSpec((1, t, 2 * LANES), lambda bi, j, i: (bi, 0, j))
    o_spec = pl.BlockSpec((1, tile, 2 * LANES), lambda bi, j, i: (bi, i, j))
    smem = pl.BlockSpec(memory_space=pltpu.SMEM)
    lam_spec = pl.BlockSpec(lam_vec.shape, lambda bi, j, i: (0, 0))
    subw_spec = pl.BlockSpec((2 * DA_HEAD_DIM, 1), lambda bi, j, i: (0, 0))
    data_specs = [q_spec(0), q_spec(half), k_spec(0), k_spec(half), v_spec]
    subw = subln_w.reshape(2 * DA_HEAD_DIM, 1)
    stat = pltpu.VMEM((4, 1, tile), F32)
    acc = pltpu.VMEM((4, 2 * DA_HEAD_DIM, tile), F32)
    common = dict(grid=(b, half, t // tile), out_specs=o_spec,
                  out_shape=jax.ShapeDtypeStruct((b, t, d), BF16),
                  compiler_params=_params("parallel", "parallel", "arbitrary"))

    def fixed(shift):
        return pl.pallas_call(
            functools.partial(_attn_fixed_kernel, lam_init),
            in_specs=[smem, smem, lam_spec, subw_spec] + data_specs,
            scratch_shapes=[pltpu.VMEM((2, 1, 2 * tile), F32),
                            pltpu.VMEM((2, 2 * DA_HEAD_DIM, 2 * tile), F32),
                            pltpu.VMEM((2 * tile, 2 * tile), F32),
                            pltpu.VMEM((2 * tile, 2 * tile), F32)],
            name="diff_attn_fixed", **common,
        )(slopes2, shift, lam_vec, subw, q, q, k, k, v)

    def online(_):
        return pl.pallas_call(
            functools.partial(_attn_online_kernel, lam_init),
            in_specs=[smem, lam_spec, subw_spec] + data_specs,
            scratch_shapes=[stat, stat, acc],
            name="diff_attn_online", **common,
        )(slopes2, lam_vec, subw, q, q, k, k, v)

    return lax.cond(score_bound2[0] <= FIXED_SHIFT_MAX_SCORE * LOG2E, fixed, online, score_bound2)


def _sample_attn_kernel(lam_init, past_visible, slopes_ref, lam_ref, subw_ref, q_ref, kc_ref,
                        kn_ref, vc_ref, vn_ref, o_ref):
    tq = q_ref.shape[1]
    past = kc_ref.shape[3]
    half = DA_HEADS // 2
    lam = _lambda_scalar(lam_ref, lam_init)
    lane = lax.broadcasted_iota(jnp.int32, (tq, LANES), 1)
    zero = jnp.zeros((tq, LANES), BF16)

    row_c = lax.broadcasted_iota(jnp.int32, (tq, past), 0)
    col_c = lax.broadcasted_iota(jnp.int32, (tq, past), 1)
    dist_c = (past + row_c - col_c).astype(F32)
    row_n = lax.broadcasted_iota(jnp.int32, (tq, tq), 0)
    col_n = lax.broadcasted_iota(jnp.int32, (tq, tq), 1)
    dist_n = jnp.abs(row_n - col_n).astype(F32)
    if not past_visible:
        vis_c = (col_c // MASK_CHUNK) <= ((past + row_c) // MASK_CHUNK)
        vis_n = ((past + col_n) // MASK_CHUNK) <= ((past + row_n) // MASK_CHUNK)

    for pair in range(half):
        tiles = [slice((mp * half + pair) * LANES, (mp * half + pair + 1) * LANES) for mp in range(2)]
        kc = [kc_ref[mp, 2 * pair:2 * pair + 2].reshape(LANES, past).astype(BF16) for mp in range(2)]
        for r in range(2):
            head = 2 * pair + r
            slope = slopes_ref[head]
            keep = (lane >= DA_HEAD_DIM) if r else (lane < DA_HEAD_DIM)
            vc = vc_ref[pl.ds(head, past, stride=DA_HEADS), :].astype(BF16)
            vn = vn_ref[0, :, head * LANES:(head + 1) * LANES]
            outs = []
            for mp in range(2):
                qm = jnp.where(keep, q_ref[0, :, tiles[mp]], zero)
                s_c = _dot(qm, kc[mp]) - slope * dist_c
                s_n = _dot_nt(qm, kn_ref[0, :, tiles[mp]]) - slope * dist_n
                if not past_visible:
                    s_c = jnp.where(vis_c, s_c, NEG_INF)
                    s_n = jnp.where(vis_n, s_n, NEG_INF)
                m = jnp.maximum(jnp.max(s_c, axis=1, keepdims=True),
                                jnp.max(s_n, axis=1, keepdims=True))
                p_c = jnp.exp2(s_c - m)
                p_n = jnp.exp2(s_n - m)
                l = jnp.sum(p_c, axis=1, keepdims=True) + jnp.sum(p_n, axis=1, keepdims=True)
                acc = _dot(p_c.astype(BF16), vc) + _dot(p_n.astype(BF16), vn)
                outs.append(acc / l)
            o = outs[0] - lam * outs[1]
            ms = jnp.mean(o * o, axis=1, keepdims=True)
            o = o * lax.rsqrt(ms + EPS) * subw_ref[...] * (1.0 - lam_init)
            o_ref[0, :, head * LANES:(head + 1) * LANES] = o.astype(o_ref.dtype)


def _sample_attention(q, k_new, v_new, k_cache, v_cache, layer_j, lam_vec, subln_w, slopes2,
                      lam_init):
    b, t_q, d = q.shape
    past = k_cache.shape[5]
    past_visible = past % MASK_CHUNK == 0 and t_q <= MASK_CHUNK
    row = pl.BlockSpec((1, t_q, d), lambda bi: (bi, 0, 0))
    return pl.pallas_call(
        functools.partial(_sample_attn_kernel, lam_init, past_visible),
        grid=(b,),
        in_specs=[pl.BlockSpec(memory_space=pltpu.SMEM),
                  pl.BlockSpec(lam_vec.shape, lambda bi: (0, 0)),
                  pl.BlockSpec((1, 2 * DA_HEAD_DIM), lambda bi: (0, 0)),
                  row,
                  pl.BlockSpec((None, None, 2, DA_HEADS, DA_HEAD_DIM, past),
                               lambda bi: (layer_j, bi, 0, 0, 0, 0)),
                  row,
                  pl.BlockSpec((None, None, past * DA_HEADS, 2 * DA_HEAD_DIM),
                               lambda bi: (layer_j, bi, 0, 0)),
                  row],
        out_specs=row,
        out_shape=jax.ShapeDtypeStruct((b, t_q, d), BF16),
        compiler_params=_params("parallel"),
        name="diff_attn_sample",
    )(slopes2, lam_vec, subln_w.reshape(1, 2 * DA_HEAD_DIM), q, k_cache, k_new, v_cache, v_new)


def _row_tile(m):
    return 512 if m % 512 == 0 else 256


def _outproj_tile(m):
    return 1024 if m % 2048 == 0 else _row_tile(m)


def _hgrn_layer(x, s0, nw, w_in_bf, lb_logits, onw, w_out_bf, layer_j):
    b, t, d = x.shape
    x2 = x.reshape(b * t, d)
    tm = _row_tile(b * t)
    q, fx, iv, z = _inproj_hgrn(x2, nw.reshape(1, d), w_in_bf, tm)
    o, s_new = _hgrn_rec(q.reshape(b, t, d), fx.reshape(b, t, d), iv.reshape(b, t, d),
                         lb_logits, s0, layer_j)
    y = _outproj(o.reshape(b * t, d), z, x2, w_out_bf, onw.reshape(1, d), True,
                 _outproj_tile(b * t))
    return y.reshape(b, t, d), s_new


def _attn_layer(x, cache, nw, w_in_bf, qn_w, kn_w, lam_vec, subln_w, w_out_bf, layer_idx,
                layer_j, n_layers, k_stack, v_stack):
    b, t, d = x.shape
    x2 = x.reshape(b * t, d)
    groups = d // DA_HEAD_DIM
    gmat = jnp.kron(jnp.eye(MXU_COLS // DA_HEAD_DIM, dtype=F32),
                    jnp.full((DA_HEAD_DIM, DA_HEAD_DIM), 1.0 / DA_HEAD_DIM, F32)).astype(BF16)
    prompt = cache is None
    tm = 512 if t % 512 == 0 else t
    q, kb, vb, z, k_stack, v_stack = _inproj_attn(
        x2, t, nw.reshape(1, d), w_in_bf, jnp.tile(qn_w, groups).reshape(1, d),
        jnp.tile(kn_w, groups).reshape(1, d), gmat, layer_j, n_layers, k_stack, v_stack, prompt, tm)
    lam_init = 0.8 - 0.6 * math.exp(-0.3 * layer_idx)
    slopes2 = jnp.exp2(-8.0 * jnp.arange(1, DA_HEADS + 1, dtype=F32) / DA_HEADS) * LOG2E
    q, kb, vb = (a.reshape(b, t, d) for a in (q, kb, vb))
    if prompt:
        bound2 = (DA_HEAD_DIM ** 0.5 * LOG2E * 1.02) * jnp.max(jnp.abs(qn_w)) * jnp.max(jnp.abs(kn_w))
        o = _prompt_attention(q, kb, vb, lam_vec, subln_w, slopes2, bound2.reshape(1), lam_init)
    else:
        k_cache, v_cache = cache
        o = _sample_attention(q, kb, vb, k_cache, v_cache, layer_j, lam_vec, subln_w, slopes2,
                              lam_init)
    y = _outproj(o.reshape(b * t, d), z, x2, w_out_bf, jnp.ones((1, d), F32), False,
                 _outproj_tile(b * t))
    return y.reshape(b, t, d), k_stack, v_stack


def kernel(x_prompt, x_sample, cache_k, cache_v, state_hgrn, norm_w, hgrn_w_in, hgrn_lb_logits,
           hgrn_onorm_w, hgrn_w_out, attn_w_in, attn_q_norm, attn_k_norm, attn_lambda, attn_subln,
           attn_w_out):
    depth = norm_w.shape[0]
    n_attn = cache_k.shape[0]
    bp, tp, d = x_prompt.shape
    bs, ts, _ = x_sample.shape
    past = cache_k.shape[2]
    cache_k_t = jnp.transpose(cache_k, (0, 1, 3, 4, 5, 2))
    cache_v2 = cache_v.reshape(n_attn, bs, past * DA_HEADS, 2 * DA_HEAD_DIM)
    yp, ys = x_prompt, x_sample
    kp = vp = ks_ = vs_ = None
    sp, ss = [], []
    for l in range(depth):
        j = l // N_MIXERS
        if l % N_MIXERS == 0:
            w_in = hgrn_w_in[j].astype(BF16)
            w_out = hgrn_w_out[j].astype(BF16)
            yp, s_p = _hgrn_layer(yp, None, norm_w[l], w_in, hgrn_lb_logits, hgrn_onorm_w[j],
                                  w_out, j)
            ys, s_s = _hgrn_layer(ys, state_hgrn[j], norm_w[l], w_in, hgrn_lb_logits,
                                  hgrn_onorm_w[j], w_out, j)
            sp.append(s_p)
            ss.append(s_s)
        else:
            w_in = attn_w_in[j].astype(BF16)
            w_out = attn_w_out[j].astype(BF16)
            yp, kp, vp = _attn_layer(yp, None, norm_w[l], w_in, attn_q_norm[j], attn_k_norm[j],
                                     attn_lambda[j], attn_subln[j], w_out, l, j, n_attn, kp, vp)
            ys, ks_, vs_ = _attn_layer(ys, (cache_k_t, cache_v2), norm_w[l], w_in,
                                       attn_q_norm[j], attn_k_norm[j], attn_lambda[j],
                                       attn_subln[j], w_out, l, j, n_attn, ks_, vs_)
    new_k_prompt = jnp.transpose(kp.reshape(n_attn, bp, 2, DA_HEADS, DA_HEAD_DIM, tp),
                                 (0, 1, 5, 2, 3, 4))
    return (yp, ys, new_k_prompt,
            vp.reshape(n_attn, bp, tp, DA_HEADS, 2 * DA_HEAD_DIM),
            ks_.reshape(n_attn, bs, ts, 2, DA_HEADS, DA_HEAD_DIM),
            vs_.reshape(n_attn, bs, ts, DA_HEADS, 2 * DA_HEAD_DIM),
            jnp.stack(sp), jnp.stack(ss))
```

```python
import functools
import math

import jax
import jax.numpy as jnp
from jax import lax
from jax.experimental import pallas as pl
from jax.experimental.pallas import tpu as pltpu

F32 = jnp.float32
BF16 = jnp.bfloat16

EPS = 1e-6
NEG_INF = -1e30
LOG2E = 1.4426950408889634
LANES = 128
MXU_COLS = 256
HG_HEADS = 8
HG_DK = 128
DA_HEADS = 8
DA_HEAD_DIM = 64
MASK_CHUNK = 64
N_MIXERS = 2

REC_CHUNK = 64
REC_CHUNKS_PER_STEP = 4
REC_MATMUL_LEVELS = (1, 2, 4)
REC_ROW_LEVELS = (8, 16, 32)
ATTN_TILE = 512
FIXED_SHIFT_MAX_SCORE = 40.0
VMEM_LIMIT = 56 * 1024 * 1024


def _dot(a, b):
    return jnp.dot(a, b, preferred_element_type=F32)


def _dot_nt(a, b):
    return lax.dot_general(a, b, (((1,), (1,)), ((), ())), preferred_element_type=F32)


def _dot_tn(a, b):
    return lax.dot_general(a, b, (((0,), (0,)), ((), ())), preferred_element_type=F32)


def _rms_rows(x, w):
    ms = jnp.mean(x * x, axis=-1, keepdims=True)
    return x * lax.rsqrt(ms + EPS) * w


def _resident(shape):
    nd = len(shape)
    return pl.BlockSpec(shape, lambda *_: (0,) * nd, pipeline_mode=pl.Buffered(1))


def _params(*semantics):
    return pltpu.CompilerParams(dimension_semantics=semantics, vmem_limit_bytes=VMEM_LIMIT)


def _inproj_hgrn_kernel(x_ref, nw_ref, w_ref, q_ref, fx_ref, i_ref, z_ref):
    d = x_ref.shape[1]
    h = _rms_rows(x_ref[...], nw_ref[...]).astype(BF16)
    q_ref[...] = (_dot(h, w_ref[:, 0:d]) * (HG_DK ** -0.5)).astype(q_ref.dtype)
    for c, o_ref in ((1, fx_ref), (2, i_ref), (3, z_ref)):
        o_ref[...] = _dot(h, w_ref[:, c * d:(c + 1) * d]).astype(o_ref.dtype)


def _inproj_hgrn(x2, nw, w_bf, tm):
    m, d = x2.shape
    row = pl.BlockSpec((tm, d), lambda i: (i, 0))
    return pl.pallas_call(
        _inproj_hgrn_kernel,
        grid=(m // tm,),
        in_specs=[row, _resident((1, d)), _resident((d, 4 * d))],
        out_specs=[row, row, row, row],
        out_shape=[jax.ShapeDtypeStruct((m, d), BF16), jax.ShapeDtypeStruct((m, d), F32),
                   jax.ShapeDtypeStruct((m, d), BF16), jax.ShapeDtypeStruct((m, d), BF16)],
        compiler_params=_params("parallel"),
        name="inproj_hgrn",
    )(x2, nw, w_bf)


def _inproj_attn_kernel(k_transposed, n_alias, x_ref, nw_ref, w_ref, qnw_ref, knw_ref, g_ref, *rest):
    q_ref, kb_ref, vb_ref, z_ref, ks_ref, vs_ref = rest[n_alias:]
    d = x_ref.shape[1]
    h = _rms_rows(x_ref[...], nw_ref[...]).astype(BF16)
    gmat = g_ref[...]
    gw = gmat.shape[0]

    def group_norm(y, w):
        cols = []
        for t in range(d // gw):
            yt = y[:, t * gw:(t + 1) * gw]
            ms = _dot((yt * yt).astype(BF16), gmat)
            cols.append(yt * lax.rsqrt(ms + EPS))
        return jnp.concatenate(cols, axis=1) * w

    q = group_norm(_dot(h, w_ref[:, 0:d]), qnw_ref[...])
    q_ref[...] = (q * (DA_HEAD_DIM ** -0.5 * LOG2E)).astype(BF16)
    k = group_norm(_dot(h, w_ref[:, d:2 * d]), knw_ref[...])
    kb_ref[...] = k.astype(BF16)
    ks_ref[0, 0] = k.T if k_transposed else k
    v = _dot(h, w_ref[:, 2 * d:3 * d])
    vs_ref[0] = v
    vb_ref[...] = v.astype(BF16)
    z_ref[...] = _dot(h, w_ref[:, 3 * d:4 * d]).astype(BF16)


def _inproj_attn(x2, seq, nw, w_bf, qnw, knw, gmat, layer_j, n_layers, k_stack, v_stack,
                 k_transposed, tm):
    m, d = x2.shape
    batch = m // seq
    tiles_per_seq = seq // tm
    row = pl.BlockSpec((tm, d), lambda i: (i, 0))
    if k_transposed:
        ks_shape = (n_layers, batch, d, seq)
        ks_spec = pl.BlockSpec((1, 1, d, tm),
                               lambda i: (layer_j, i // tiles_per_seq, 0, i % tiles_per_seq))
    else:
        ks_shape = (n_layers, batch, seq, d)
        ks_spec = pl.BlockSpec((1, 1, tm, d),
                               lambda i: (layer_j, i // tiles_per_seq, i % tiles_per_seq, 0))
    vs_spec = pl.BlockSpec((1, tm, d), lambda i: (layer_j, i, 0))
    in_specs = [row, _resident((1, d)), _resident((d, 4 * d)), _resident((1, d)),
                _resident((1, d)), _resident(gmat.shape)]
    args = [x2, nw, w_bf, qnw, knw, gmat]
    aliases = {}
    if k_stack is not None:
        in_specs += [pl.BlockSpec(memory_space=pl.ANY)] * 2
        aliases = {len(args): 4, len(args) + 1: 5}
        args += [k_stack, v_stack]
    shp = lambda dt: jax.ShapeDtypeStruct((m, d), dt)
    return pl.pallas_call(
        functools.partial(_inproj_attn_kernel, k_transposed, len(aliases)),
        grid=(m // tm,),
        in_specs=in_specs,
        out_specs=[row, row, row, row, ks_spec, vs_spec],
        out_shape=[shp(BF16), shp(BF16), shp(BF16), shp(BF16),
                   jax.ShapeDtypeStruct(ks_shape, F32),
                   jax.ShapeDtypeStruct((n_layers, m, d), F32)],
        input_output_aliases=aliases,
        compiler_params=_params("parallel"),
        name="inproj_attn",
    )(*args)


def _outproj_kernel(full_norm, o_ref, z_ref, x_ref, w_ref, ow_ref, y_ref):
    o = o_ref[...].astype(F32)
    if full_norm:
        o = _rms_rows(o, ow_ref[...])
    z = z_ref[...].astype(F32)
    gated = (o * (z * jax.nn.sigmoid(z))).astype(BF16)
    y_ref[...] = x_ref[...] + _dot(gated, w_ref[...])


def _outproj(o2, z2, x2, w_bf, ow, full_norm, tm):
    m, d = x2.shape
    row = pl.BlockSpec((tm, d), lambda i: (i, 0))
    return pl.pallas_call(
        functools.partial(_outproj_kernel, full_norm),
        grid=(m // tm,),
        in_specs=[row, row, row, _resident((d, d)), _resident((1, d))],
        out_specs=row,
        out_shape=jax.ShapeDtypeStruct((m, d), F32),
        compiler_params=_params("parallel"),
        name="outproj",
    )(o2, z2, x2, w_bf, ow)


def _range_matrices(c):
    t = jnp.arange(c)[:, None]
    s = jnp.arange(c)[None, :]
    blocks = [s <= t]
    for w in REC_MATMUL_LEVELS:
        ref = (t // (2 * w)) * (2 * w) + w - 1
        upper = (t & w) != 0
        blocks.append(jnp.where(upper, (s > ref) & (s <= t), (s > t) & (s <= ref)))
    e = jnp.concatenate(blocks, axis=0).astype(BF16)
    return jnp.concatenate([e, e], axis=1)


def _hgrn_rec_kernel(layer_j, has_s0, q_ref, fx_ref, v_ref, lbl_ref, emat_ref, *rest):
    if has_s0:
        s0_ref, o_ref, sfin_ref, st_ref = rest
    else:
        o_ref, sfin_ref, st_ref = rest
    c_idx = pl.program_id(1)
    c = REC_CHUNK
    n_chunks = q_ref.shape[1] // c
    d = q_ref.shape[2]

    @pl.when(c_idx == 0)
    def _():
        for h in range(HG_HEADS):
            if has_s0:
                st_ref[h] = s0_ref[0, h].T
            else:
                st_ref[h] = jnp.zeros((HG_DK, HG_DK), F32)

    lg = lbl_ref[...]
    ex = jnp.exp(lg - jnp.max(lg, axis=0, keepdims=True))
    p = ex / jnp.sum(ex, axis=0, keepdims=True)
    cs = p[0:1]
    for r in range(1, layer_j + 1):
        cs = cs + p[r:r + 1]
    lb = cs - p[0:1]

    row = lax.broadcasted_iota(jnp.int32, (c, d), 0)
    tt = lax.broadcasted_iota(jnp.int32, (c, c), 0)
    ss = lax.broadcasted_iota(jnp.int32, (c, c), 1)
    diag_mask = tt == ss
    level_masks = [((tt // (2 * w)) == (ss // (2 * w))) & ((tt & w) != 0) & ((ss & w) == 0)
                   for w in REC_MATMUL_LEVELS + REC_ROW_LEVELS]
    heads = [slice(h * HG_DK, (h + 1) * HG_DK) for h in range(HG_HEADS)]

    def operands(rows_c, after):
        lb_c = lb if after is None else lb + 0.0 * after
        f = lb_c + (1.0 - lb_c) * jax.nn.sigmoid(fx_ref[0, rows_c, :])
        g = jnp.log2(f)
        kk = 1.0 - f
        g_hi = pltpu.bitcast(pltpu.bitcast(g, jnp.uint32) & jnp.uint32(0xFFFF0000), F32)
        g_pieces = jnp.concatenate([g_hi.astype(BF16), (g - g_hi).astype(BF16)], axis=0)
        ranges = _dot(emat_ref[...], g_pieces)
        b = ranges[0:c]

        q = q_ref[0, rows_c, :].astype(F32)
        q_state = (q * jnp.exp2(b)).astype(BF16)
        k_state = (kk * jnp.exp2(b[c - 1:c] - b)).astype(BF16)
        state_decay = jnp.exp2(b[c - 1:c])

        level_ops = []
        for li, w in enumerate(REC_MATMUL_LEVELS):
            x = jnp.exp2(ranges[(li + 1) * c:(li + 2) * c])
            level_ops.append((jnp.where((row & w) != 0, q, kk) * x).astype(BF16))
        for w in REC_ROW_LEVELS:
            pieces = []
            for blk in range(c // w):
                rows = slice(blk * w, (blk + 1) * w)
                mid = (blk // 2) * 2 * w + w - 1
                if blk % 2:
                    pieces.append(q[rows] * jnp.exp2(b[rows] - b[mid:mid + 1]))
                else:
                    pieces.append(kk[rows] * jnp.exp2(b[mid:mid + 1] - b[rows]))
            level_ops.append(jnp.concatenate(pieces, axis=0).astype(BF16))
        done = (jnp.max(level_ops[-1], axis=0, keepdims=True).astype(F32)
                + jnp.max(q_state, axis=0, keepdims=True).astype(F32))
        return (q.astype(BF16), kk.astype(BF16), level_ops, q_state, k_state, state_decay), done

    def recur(rows_c, ops):
        q_bf, k_bf, level_ops, q_state, k_state, state_decay = ops
        att = []
        for sl in heads:
            a = jnp.where(diag_mask, _dot_nt(q_bf[:, sl], k_bf[:, sl]), 0.0)
            for mask, y in zip(level_masks, level_ops):
                a = jnp.where(mask, _dot_nt(y[:, sl], y[:, sl]), a)
            att.append(a.astype(BF16))
        for h, sl in enumerate(heads):
            o = (_dot(att[h], v_ref[0, rows_c, sl])
                 + _dot_nt(q_state[:, sl], st_ref[h].astype(BF16)))
            o_ref[0, rows_c, sl] = o.astype(o_ref.dtype)
        for h, sl in enumerate(heads):
            st_ref[h] = (st_ref[h] * state_decay[:, sl]
                         + _dot_tn(v_ref[0, rows_c, sl], k_state[:, sl]))

    chunk_rows = [slice(n * c, (n + 1) * c) for n in range(n_chunks)]
    ops, done = operands(chunk_rows[0], None)
    for n in range(n_chunks):
        ops_next = None
        if n + 1 < n_chunks:
            ops_next, done = operands(chunk_rows[n + 1], done)
        recur(chunk_rows[n], ops)
        ops = ops_next

    @pl.when(c_idx == pl.num_programs(1) - 1)
    def _():
        for h in range(HG_HEADS):
            sfin_ref[0, h] = st_ref[h].T


def _hgrn_rec(q, fx, v, lb_logits, s0, layer_j):
    b, t, d = q.shape
    c = min(t, REC_CHUNK * REC_CHUNKS_PER_STEP)
    blk = pl.BlockSpec((1, c, d), lambda i, j: (i, j, 0))
    st_blk = pl.BlockSpec((1, HG_HEADS, HG_DK, HG_DK), lambda i, j: (i, 0, 0, 0))
    emat = _range_matrices(REC_CHUNK)
    in_specs = [blk, blk, blk, _resident(lb_logits.shape), _resident(emat.shape)]
    args = [q, fx, v, lb_logits, emat]
    if s0 is not None:
        in_specs.append(st_blk)
        args.append(s0)
    return pl.pallas_call(
        functools.partial(_hgrn_rec_kernel, layer_j, s0 is not None),
        grid=(b, t // c),
        in_specs=in_specs,
        out_specs=[blk, st_blk],
        out_shape=[jax.ShapeDtypeStruct((b, t, d), BF16),
                   jax.ShapeDtypeStruct((b, HG_HEADS, HG_DK, HG_DK), F32)],
        scratch_shapes=[pltpu.VMEM((HG_HEADS, HG_DK, HG_DK), F32)],
        compiler_params=_params("parallel", "arbitrary"),
        name="hgrn_rec",
    )(*args)


def _masked_queries(q_refs):
    tq = q_refs[0].shape[1]
    lane = lax.broadcasted_iota(jnp.int32, (tq, LANES), 1)
    zero = jnp.zeros((tq, LANES), BF16)
    return [[jnp.where(lane < DA_HEAD_DIM, qr[0], zero), jnp.where(lane >= DA_HEAD_DIM, qr[0], zero)]
            for qr in q_refs]


def _lambda_scalar(lam_ref, lam_init):
    lv = lam_ref[...]
    return (jnp.exp(jnp.sum(lv[0:1] * lv[1:2], axis=1, keepdims=True))
            - jnp.exp(jnp.sum(lv[2:3] * lv[3:4], axis=1, keepdims=True)) + lam_init)


def _finish_transposed(lam_ref, subw_ref, lam_init, l_sc, acc_sc, o_ref):
    lam = _lambda_scalar(lam_ref, lam_init)
    for r in range(2):
        o = acc_sc[2 * r] / l_sc[2 * r] - lam * (acc_sc[2 * r + 1] / l_sc[2 * r + 1])
        ms = jnp.mean(o * o, axis=0, keepdims=True)
        o = o * lax.rsqrt(ms + EPS) * subw_ref[...] * (1.0 - lam_init)
        o_ref[0, :, r * LANES:(r + 1) * LANES] = o.T.astype(o_ref.dtype)


def _attn_rows_kernel(lam_init, n_q, slopes_ref, shift_ref, lam_ref, subw_ref, q0_ref, q1_ref,
                      k0_ref, k1_ref, v_ref, o_ref, bias_sc):
    j = pl.program_id(0)
    bi = pl.program_id(1)
    i = pl.program_id(2)
    t = q0_ref.shape[1]

    @pl.when((bi == 0) & (i == 0))
    def _():
        off = lax.broadcasted_iota(jnp.int32, (n_q * t, t), 0) - (n_q - 1) * t
        col = lax.broadcasted_iota(jnp.int32, (n_q * t, t), 1)
        dist = jnp.abs(col - off).astype(F32)
        visible = (off // MASK_CHUNK) <= (col // MASK_CHUNK)
        for r in range(2):
            bias_sc[:, r * t:(r + 1) * t] = jnp.where(
                visible, -slopes_ref[2 * j + r] * dist - shift_ref[0], NEG_INF)

    qm = _masked_queries((q0_ref, q1_ref))
    q_pair = [jnp.concatenate(qm[mp], axis=0) for mp in range(2)]
    k_refs = (k0_ref, k1_ref)
    lam = _lambda_scalar(lam_ref, lam_init)

    def attend(tile_idx):
        rows = (tile_idx + 1) * t
        bias_row0 = (n_q - 1 - tile_idx) * t
        s = [_dot_nt(k_refs[mp][0, 0:rows, :], q_pair[mp]) for mp in range(2)]
        for r in range(2):
            cols = slice(r * t, (r + 1) * t)
            bias = bias_sc[bias_row0:bias_row0 + rows, cols]
            p = [jnp.exp2(s[mp][:, cols] + bias) for mp in range(2)]
            inv0 = 1.0 / jnp.sum(p[0], axis=0, keepdims=True)
            inv1 = lam / jnp.sum(p[1], axis=0, keepdims=True)
            num = _dot_tn(v_ref[0, 0:rows, r * LANES:(r + 1) * LANES],
                          jnp.concatenate([pm.astype(BF16) for pm in p], axis=1))
            o = num[:, :t] * inv0 - num[:, t:] * inv1
            ms = jnp.mean(o * o, axis=0, keepdims=True)
            o = o * lax.rsqrt(ms + EPS) * subw_ref[...] * (1.0 - lam_init)
            o_ref[0, :, r * LANES:(r + 1) * LANES] = o.T.astype(o_ref.dtype)

    for tile_idx in range(n_q):
        pl.when(i == tile_idx)(functools.partial(attend, tile_idx))


def _attn_online_kernel(lam_init, slopes_ref, lam_ref, subw_ref, q0_ref, q1_ref, k0_ref, k1_ref,
                        v_ref, o_ref, m_sc, l_sc, acc_sc):
    j = pl.program_id(1)
    i = pl.program_id(2)
    t = q0_ref.shape[1]
    qm = _masked_queries((q0_ref, q1_ref))
    k_refs = (k0_ref, k1_ref)
    m_sc[...] = jnp.full(m_sc.shape, -jnp.inf, F32)
    l_sc[...] = jnp.zeros(l_sc.shape, F32)
    acc_sc[...] = jnp.zeros(acc_sc.shape, F32)

    def tile(kt_idx, diagonal):
        k_start = pl.multiple_of(kt_idx * t, t)
        row = lax.broadcasted_iota(jnp.int32, (t, t), 0)
        col = lax.broadcasted_iota(jnp.int32, (t, t), 1)
        dist = jnp.abs((i - kt_idx) * t + (col - row)).astype(F32)
        if diagonal:
            visible = (row // MASK_CHUNK) <= (col // MASK_CHUNK)
        for r in range(2):
            bias = -slopes_ref[2 * j + r] * dist
            vt = v_ref[0, pl.ds(k_start, t), r * LANES:(r + 1) * LANES]
            for mp in range(2):
                idx = 2 * r + mp
                kt = k_refs[mp][0, pl.ds(k_start, t), :]
                s = _dot_nt(kt, qm[mp][r]) + bias
                if diagonal:
                    s = jnp.where(visible, s, NEG_INF)
                m_prev = m_sc[idx]
                m_new = jnp.maximum(m_prev, jnp.max(s, axis=0, keepdims=True))
                alpha = jnp.exp2(m_prev - m_new)
                p = jnp.exp2(s - m_new)
                l_sc[idx] = alpha * l_sc[idx] + jnp.sum(p, axis=0, keepdims=True)
                acc_sc[idx] = alpha * acc_sc[idx] + _dot_tn(vt, p.astype(BF16))
                m_sc[idx] = m_new

    def below(kt_idx, carry):
        tile(kt_idx, False)
        return carry

    lax.fori_loop(0, i, below, 0)
    tile(i, True)
    _finish_transposed(lam_ref, subw_ref, lam_init, l_sc, acc_sc, o_ref)


def _prompt_attention(q, k, v, lam_vec, subln_w, slopes2, score_bound2, lam_init):
    b, t, d = q.shape
    tile = ATTN_TILE
    half = d // (2 * LANES)
    q_spec = lambda off: pl.BlockSpec((1, tile, LANES), lambda bi, j, i: (bi, i, j + off))
    k_spec = lambda off: pl.BlockSpec((1, t, LANES), lambda bi, j, i: (bi, 0, j + off))
    v_spec = pl.BlockSpec((1, t, 2 * LANES), lambda bi, j, i: (bi, 0, j))
    o_spec = pl.BlockSpec((1, tile, 2 * LANES), lambda bi, j, i: (bi, i, j))
    smem = pl.BlockSpec(memory_space=pltpu.SMEM)
    lam_spec = pl.BlockSpec(lam_vec.shape, lambda bi, j, i: (0, 0))
    subw_spec = pl.BlockSpec((2 * DA_HEAD_DIM, 1), lambda bi, j, i: (0, 0))
    data_specs = [q_spec(0), q_spec(half), k_spec(0), k_spec(half), v_spec]
    subw = subln_w.reshape(2 * DA_HEAD_DIM, 1)
    stat = pltpu.VMEM((4, 1, tile), F32)
    acc = pltpu.VMEM((4, 2 * DA_HEAD_DIM, tile), F32)
    common = dict(grid=(b, half, t // tile), out_specs=o_spec,
                  out_shape=jax.ShapeDtypeStruct((b, t, d), BF16),
                  compiler_params=_params("parallel", "parallel", "arbitrary"))

    def fixed(shift):
        reorder = lambda spec: pl.BlockSpec(spec.block_shape,
                                            lambda j, bi, i, f=spec.index_map: f(bi, j, i))
        return pl.pallas_call(
            functools.partial(_attn_rows_kernel, lam_init, t // tile),
            grid=(half, b, t // tile),
            in_specs=[smem, smem, reorder(lam_spec), reorder(subw_spec)]
            + [reorder(spec) for spec in data_specs],
            out_specs=reorder(o_spec),
            out_shape=jax.ShapeDtypeStruct((b, t, d), BF16),
            scratch_shapes=[pltpu.VMEM((t, 2 * tile), F32)],
            compiler_params=_params("arbitrary", "arbitrary", "arbitrary"),
            name="diff_attn_fixed",
        )(slopes2, shift, lam_vec, subw, q, q, k, k, v)

    def online(_):
        return pl.pallas_call(
            functools.partial(_attn_online_kernel, lam_init),
            in_specs=[smem, lam_spec, subw_spec] + data_specs,
            scratch_shapes=[stat, stat, acc],
            name="diff_attn_online", **common,
        )(slopes2, lam_vec, subw, q, q, k, k, v)

    return lax.cond(score_bound2[0] <= FIXED_SHIFT_MAX_SCORE * LOG2E, fixed, online, score_bound2)


def _sample_attn_kernel(lam_init, past_visible, slopes_ref, lam_ref, subw_ref, q_ref, kc_ref,
                        kn_ref, vc_ref, vn_ref, o_ref):
    tq = q_ref.shape[1]
    past = kc_ref.shape[3]
    half = DA_HEADS // 2
    lam = _lambda_scalar(lam_ref, lam_init)
    lane = lax.broadcasted_iota(jnp.int32, (tq, LANES), 1)
    zero = jnp.zeros((tq, LANES), BF16)

    row_c = lax.broadcasted_iota(jnp.int32, (tq, past), 0)
    col_c = lax.broadcasted_iota(jnp.int32, (tq, past), 1)
    dist_c = (past + row_c - col_c).astype(F32)
    row_n = lax.broadcasted_iota(jnp.int32, (tq, tq), 0)
    col_n = lax.broadcasted_iota(jnp.int32, (tq, tq), 1)
    dist_n = jnp.abs(row_n - col_n).astype(F32)
    if not past_visible:
        vis_c = (col_c // MASK_CHUNK) <= ((past + row_c) // MASK_CHUNK)
        vis_n = ((past + col_n) // MASK_CHUNK) <= ((past + row_n) // MASK_CHUNK)

    for pair in range(half):
        tiles = [slice((mp * half + pair) * LANES, (mp * half + pair + 1) * LANES) for mp in range(2)]
        kc = [kc_ref[mp, 2 * pair:2 * pair + 2].reshape(LANES, past).astype(BF16) for mp in range(2)]
        for r in range(2):
            head = 2 * pair + r
            slope = slopes_ref[head]
            keep = (lane >= DA_HEAD_DIM) if r else (lane < DA_HEAD_DIM)
            vc = vc_ref[pl.ds(head, past, stride=DA_HEADS), :].astype(BF16)
            vn = vn_ref[0, :, head * LANES:(head + 1) * LANES]
            outs = []
            for mp in range(2):
                qm = jnp.where(keep, q_ref[0, :, tiles[mp]], zero)
                s_c = _dot(qm, kc[mp]) - slope * dist_c
                s_n = _dot_nt(qm, kn_ref[0, :, tiles[mp]]) - slope * dist_n
                if not past_visible:
                    s_c = jnp.where(vis_c, s_c, NEG_INF)
                    s_n = jnp.where(vis_n, s_n, NEG_INF)
                m = jnp.maximum(jnp.max(s_c, axis=1, keepdims=True),
                                jnp.max(s_n, axis=1, keepdims=True))
                p_c = jnp.exp2(s_c - m)
                p_n = jnp.exp2(s_n - m)
                l = jnp.sum(p_c, axis=1, keepdims=True) + jnp.sum(p_n, axis=1, keepdims=True)
                acc = _dot(p_c.astype(BF16), vc) + _dot(p_n.astype(BF16), vn)
                outs.append(acc / l)
            o = outs[0] - lam * outs[1]
            ms = jnp.mean(o * o, axis=1, keepdims=True)
            o = o * lax.rsqrt(ms + EPS) * subw_ref[...] * (1.0 - lam_init)
            o_ref[0, :, head * LANES:(head + 1) * LANES] = o.astype(o_ref.dtype)


def _sample_attention(q, k_new, v_new, k_cache, v_cache, layer_j, lam_vec, subln_w, slopes2,
                      lam_init):
    b, t_q, d = q.shape
    past = k_cache.shape[5]
    past_visible = past % MASK_CHUNK == 0 and t_q <= MASK_CHUNK
    row = pl.BlockSpec((1, t_q, d), lambda bi: (bi, 0, 0))
    return pl.pallas_call(
        functools.partial(_sample_attn_kernel, lam_init, past_visible),
        grid=(b,),
        in_specs=[pl.BlockSpec(memory_space=pltpu.SMEM),
                  pl.BlockSpec(lam_vec.shape, lambda bi: (0, 0)),
                  pl.BlockSpec((1, 2 * DA_HEAD_DIM), lambda bi: (0, 0)),
                  row,
                  pl.BlockSpec((None, None, 2, DA_HEADS, DA_HEAD_DIM, past),
                               lambda bi: (layer_j, bi, 0, 0, 0, 0)),
                  row,
                  pl.BlockSpec((None, None, past * DA_HEADS, 2 * DA_HEAD_DIM),
                               lambda bi: (layer_j, bi, 0, 0)),
                  row],
        out_specs=row,
        out_shape=jax.ShapeDtypeStruct((b, t_q, d), BF16),
        compiler_params=_params("parallel"),
        name="diff_attn_sample",
    )(slopes2, lam_vec, subln_w.reshape(1, 2 * DA_HEAD_DIM), q, k_cache, k_new, v_cache, v_new)


def _row_tile(m):
    return 512 if m % 512 == 0 else 256


def _outproj_tile(m):
    return 1024 if m % 2048 == 0 else _row_tile(m)


def _hgrn_layer(x, s0, nw, w_in_bf, lb_logits, onw, w_out_bf, layer_j):
    b, t, d = x.shape
    x2 = x.reshape(b * t, d)
    tm = _row_tile(b * t)
    q, fx, iv, z = _inproj_hgrn(x2, nw.reshape(1, d), w_in_bf, tm)
    o, s_new = _hgrn_rec(q.reshape(b, t, d), fx.reshape(b, t, d), iv.reshape(b, t, d),
                         lb_logits, s0, layer_j)
    y = _outproj(o.reshape(b * t, d), z, x2, w_out_bf, onw.reshape(1, d), True,
                 _outproj_tile(b * t))
    return y.reshape(b, t, d), s_new


def _attn_layer(x, cache, nw, w_in_bf, qn_w, kn_w, lam_vec, subln_w, w_out_bf, layer_idx,
                layer_j, n_layers, k_stack, v_stack):
    b, t, d = x.shape
    x2 = x.reshape(b * t, d)
    groups = d // DA_HEAD_DIM
    gmat = jnp.kron(jnp.eye(MXU_COLS // DA_HEAD_DIM, dtype=F32),
                    jnp.full((DA_HEAD_DIM, DA_HEAD_DIM), 1.0 / DA_HEAD_DIM, F32)).astype(BF16)
    prompt = cache is None
    tm = 512 if t % 512 == 0 else t
    q, kb, vb, z, k_stack, v_stack = _inproj_attn(
        x2, t, nw.reshape(1, d), w_in_bf, jnp.tile(qn_w, groups).reshape(1, d),
        jnp.tile(kn_w, groups).reshape(1, d), gmat, layer_j, n_layers, k_stack, v_stack, prompt, tm)
    lam_init = 0.8 - 0.6 * math.exp(-0.3 * layer_idx)
    slopes2 = jnp.exp2(-8.0 * jnp.arange(1, DA_HEADS + 1, dtype=F32) / DA_HEADS) * LOG2E
    q, kb, vb = (a.reshape(b, t, d) for a in (q, kb, vb))
    if prompt:
        bound2 = (DA_HEAD_DIM ** 0.5 * LOG2E * 1.02) * jnp.max(jnp.abs(qn_w)) * jnp.max(jnp.abs(kn_w))
        o = _prompt_attention(q, kb, vb, lam_vec, subln_w, slopes2, bound2.reshape(1), lam_init)
    else:
        k_cache, v_cache = cache
        o = _sample_attention(q, kb, vb, k_cache, v_cache, layer_j, lam_vec, subln_w, slopes2,
                              lam_init)
    y = _outproj(o.reshape(b * t, d), z, x2, w_out_bf, jnp.ones((1, d), F32), False,
                 _outproj_tile(b * t))
    return y.reshape(b, t, d), k_stack, v_stack


def kernel(x_prompt, x_sample, cache_k, cache_v, state_hgrn, norm_w, hgrn_w_in, hgrn_lb_logits,
           hgrn_onorm_w, hgrn_w_out, attn_w_in, attn_q_norm, attn_k_norm, attn_lambda, attn_subln,
           attn_w_out):
    depth = norm_w.shape[0]
    n_attn = cache_k.shape[0]
    bp, tp, d = x_prompt.shape
    bs, ts, _ = x_sample.shape
    past = cache_k.shape[2]
    cache_k_t = jnp.transpose(cache_k, (0, 1, 3, 4, 5, 2))
    cache_v2 = cache_v.reshape(n_attn, bs, past * DA_HEADS, 2 * DA_HEAD_DIM)
    yp, ys = x_prompt, x_sample
    kp = vp = ks_ = vs_ = None
    sp, ss = [], []
    for l in range(depth):
        j = l // N_MIXERS
        if l % N_MIXERS == 0:
            w_in = hgrn_w_in[j].astype(BF16)
            w_out = hgrn_w_out[j].astype(BF16)
            yp, s_p = _hgrn_layer(yp, None, norm_w[l], w_in, hgrn_lb_logits, hgrn_onorm_w[j],
                                  w_out, j)
            ys, s_s = _hgrn_layer(ys, state_hgrn[j], norm_w[l], w_in, hgrn_lb_logits,
                                  hgrn_onorm_w[j], w_out, j)
            sp.append(s_p)
            ss.append(s_s)
        else:
            w_in = attn_w_in[j].astype(BF16)
            w_out = attn_w_out[j].astype(BF16)
            yp, kp, vp = _attn_layer(yp, None, norm_w[l], w_in, attn_q_norm[j], attn_k_norm[j],
                                     attn_lambda[j], attn_subln[j], w_out, l, j, n_attn, kp, vp)
            ys, ks_, vs_ = _attn_layer(ys, (cache_k_t, cache_v2), norm_w[l], w_in,
                                       attn_q_norm[j], attn_k_norm[j], attn_lambda[j],
                                       attn_subln[j], w_out, l, j, n_attn, ks_, vs_)
    new_k_prompt = jnp.transpose(kp.reshape(n_attn, bp, 2, DA_HEADS, DA_HEAD_DIM, tp),
                                 (0, 1, 5, 2, 3, 4))
    return (yp, ys, new_k_prompt,
            vp.reshape(n_attn, bp, tp, DA_HEADS, 2 * DA_HEAD_DIM),
            ks_.reshape(n_attn, bs, ts, 2, DA_HEADS, DA_HEAD_DIM),
            vs_.reshape(n_attn, bs, ts, DA_HEADS, 2 * DA_HEAD_DIM),
            jnp.stack(sp), jnp.stack(ss))
```

```python
import functools
import math

import jax
import jax.numpy as jnp
from jax import lax
from jax.experimental import pallas as pl
from jax.experimental.pallas import tpu as pltpu

F32 = jnp.float32
BF16 = jnp.bfloat16

EPS = 1e-6
NEG_INF = -1e30
LOG2E = 1.4426950408889634
LANES = 128
MXU_COLS = 256
HG_HEADS = 8
HG_DK = 128
DA_HEADS = 8
DA_HEAD_DIM = 64
MASK_CHUNK = 64
N_MIXERS = 2

REC_CHUNK = 64
REC_CHUNKS_PER_STEP = 4
REC_STREAMS_PER_STEP = 1
REC_MATMUL_LEVELS = (1, 2, 4)
REC_ROW_LEVELS = (8, 16, 32)
ATTN_TILE = 512
FIXED_SHIFT_MAX_SCORE = 40.0
VMEM_LIMIT = 56 * 1024 * 1024


def _dot(a, b):
    return jnp.dot(a, b, preferred_element_type=F32)


def _dot_nt(a, b):
    return lax.dot_general(a, b, (((1,), (1,)), ((), ())), preferred_element_type=F32)


def _dot_tn(a, b):
    return lax.dot_general(a, b, (((0,), (0,)), ((), ())), preferred_element_type=F32)


def _rms_rows(x, w):
    ms = jnp.mean(x * x, axis=-1, keepdims=True)
    return x * lax.rsqrt(ms + EPS) * w


def _resident(shape):
    nd = len(shape)
    return pl.BlockSpec(shape, lambda *_: (0,) * nd, pipeline_mode=pl.Buffered(1))


def _params(*semantics):
    return pltpu.CompilerParams(dimension_semantics=semantics, vmem_limit_bytes=VMEM_LIMIT)


def _inproj_hgrn_kernel(x_ref, nw_ref, w_ref, q_ref, fx_ref, i_ref, z_ref):
    d = x_ref.shape[1]
    h = _rms_rows(x_ref[...], nw_ref[...]).astype(BF16)
    q_ref[...] = (_dot(h, w_ref[:, 0:d]) * (HG_DK ** -0.5)).astype(q_ref.dtype)
    for c, o_ref in ((1, fx_ref), (2, i_ref), (3, z_ref)):
        o_ref[...] = _dot(h, w_ref[:, c * d:(c + 1) * d]).astype(o_ref.dtype)


def _inproj_hgrn(x2, nw, w_bf, tm):
    m, d = x2.shape
    row = pl.BlockSpec((tm, d), lambda i: (i, 0))
    return pl.pallas_call(
        _inproj_hgrn_kernel,
        grid=(m // tm,),
        in_specs=[row, _resident((1, d)), _resident((d, 4 * d))],
        out_specs=[row, row, row, row],
        out_shape=[jax.ShapeDtypeStruct((m, d), BF16), jax.ShapeDtypeStruct((m, d), F32),
                   jax.ShapeDtypeStruct((m, d), BF16), jax.ShapeDtypeStruct((m, d), BF16)],
        compiler_params=_params("parallel"),
        name="inproj_hgrn",
    )(x2, nw, w_bf)


def _inproj_attn_kernel(k_transposed, n_alias, x_ref, nw_ref, w_ref, qnw_ref, knw_ref, g_ref, *rest):
    q_ref, kb_ref, vb_ref, z_ref, ks_ref, vs_ref = rest[n_alias:]
    d = x_ref.shape[1]
    h = _rms_rows(x_ref[...], nw_ref[...]).astype(BF16)
    gmat = g_ref[...]
    gw = gmat.shape[0]

    def group_norm(y, w):
        cols = []
        for t in range(d // gw):
            yt = y[:, t * gw:(t + 1) * gw]
            ms = _dot((yt * yt).astype(BF16), gmat)
            cols.append(yt * lax.rsqrt(ms + EPS))
        return jnp.concatenate(cols, axis=1) * w

    q = group_norm(_dot(h, w_ref[:, 0:d]), qnw_ref[...])
    q_ref[...] = (q * (DA_HEAD_DIM ** -0.5 * LOG2E)).astype(BF16)
    k = group_norm(_dot(h, w_ref[:, d:2 * d]), knw_ref[...])
    kb_ref[...] = k.astype(BF16)
    ks_ref[0, 0] = k.T if k_transposed else k
    v = _dot(h, w_ref[:, 2 * d:3 * d])
    for hd in range(DA_HEADS):
        vs_ref[0, pl.ds(hd, x_ref.shape[0], stride=DA_HEADS), :] = v[:, hd * LANES:(hd + 1) * LANES]
    vb_ref[...] = v.astype(BF16)
    z_ref[...] = _dot(h, w_ref[:, 3 * d:4 * d]).astype(BF16)


def _inproj_attn(x2, seq, nw, w_bf, qnw, knw, gmat, layer_j, n_layers, k_stack, v_stack,
                 k_transposed, tm):
    m, d = x2.shape
    batch = m // seq
    tiles_per_seq = seq // tm
    row = pl.BlockSpec((tm, d), lambda i: (i, 0))
    if k_transposed:
        ks_shape = (n_layers, batch, d, seq)
        ks_spec = pl.BlockSpec((1, 1, d, tm),
                               lambda i: (layer_j, i // tiles_per_seq, 0, i % tiles_per_seq))
    else:
        ks_shape = (n_layers, batch, seq, d)
        ks_spec = pl.BlockSpec((1, 1, tm, d),
                               lambda i: (layer_j, i // tiles_per_seq, i % tiles_per_seq, 0))
    vs_spec = pl.BlockSpec((1, tm * DA_HEADS, 2 * DA_HEAD_DIM), lambda i: (layer_j, i, 0))
    in_specs = [row, _resident((1, d)), _resident((d, 4 * d)), _resident((1, d)),
                _resident((1, d)), _resident(gmat.shape)]
    args = [x2, nw, w_bf, qnw, knw, gmat]
    aliases = {}
    if k_stack is not None:
        in_specs += [pl.BlockSpec(memory_space=pl.ANY)] * 2
        aliases = {len(args): 4, len(args) + 1: 5}
        args += [k_stack, v_stack]
    shp = lambda dt: jax.ShapeDtypeStruct((m, d), dt)
    return pl.pallas_call(
        functools.partial(_inproj_attn_kernel, k_transposed, len(aliases)),
        grid=(m // tm,),
        in_specs=in_specs,
        out_specs=[row, row, row, row, ks_spec, vs_spec],
        out_shape=[shp(BF16), shp(BF16), shp(BF16), shp(BF16),
                   jax.ShapeDtypeStruct(ks_shape, F32),
                   jax.ShapeDtypeStruct((n_layers, m * DA_HEADS, 2 * DA_HEAD_DIM), F32)],
        input_output_aliases=aliases,
        compiler_params=_params("parallel"),
        name="inproj_attn",
    )(*args)


def _outproj_kernel(full_norm, o_ref, z_ref, x_ref, w_ref, ow_ref, y_ref):
    o = o_ref[...].astype(F32)
    if full_norm:
        o = _rms_rows(o, ow_ref[...])
    z = z_ref[...].astype(F32)
    gated = (o * (z * jax.nn.sigmoid(z))).astype(BF16)
    y_ref[...] = x_ref[...] + _dot(gated, w_ref[...])


def _outproj(o2, z2, x2, w_bf, ow, full_norm, tm):
    m, d = x2.shape
    row = pl.BlockSpec((tm, d), lambda i: (i, 0))
    return pl.pallas_call(
        functools.partial(_outproj_kernel, full_norm),
        grid=(m // tm,),
        in_specs=[row, row, row, _resident((d, d)), _resident((1, d))],
        out_specs=row,
        out_shape=jax.ShapeDtypeStruct((m, d), F32),
        compiler_params=_params("parallel"),
        name="outproj",
    )(o2, z2, x2, w_bf, ow)


def _range_matrices(c):
    t = jnp.arange(c)[:, None]
    s = jnp.arange(c)[None, :]
    blocks = [s <= t]
    for w in REC_MATMUL_LEVELS:
        ref = (t // (2 * w)) * (2 * w) + w - 1
        upper = (t & w) != 0
        blocks.append(jnp.where(upper, (s > ref) & (s <= t), (s > t) & (s <= ref)))
    e = jnp.concatenate(blocks, axis=0).astype(BF16)
    return jnp.concatenate([e, e], axis=1)


def _hgrn_rec_kernel(layer_j, has_s0, q_ref, fx_ref, v_ref, lbl_ref, emat_ref, *rest):
    if has_s0:
        s0_ref, o_ref, sfin_ref, st_ref = rest
    else:
        o_ref, sfin_ref, st_ref = rest
    c_idx = pl.program_id(1)
    c = REC_CHUNK
    n_streams = q_ref.shape[0]
    n_chunks = q_ref.shape[1] // c
    d = q_ref.shape[2]

    @pl.when(c_idx == 0)
    def _():
        for si in range(n_streams):
            for h in range(HG_HEADS):
                if has_s0:
                    st_ref[si, h] = s0_ref[si, h].T
                else:
                    st_ref[si, h] = jnp.zeros((HG_DK, HG_DK), F32)

    lg = lbl_ref[...]
    ex = jnp.exp(lg - jnp.max(lg, axis=0, keepdims=True))
    p = ex / jnp.sum(ex, axis=0, keepdims=True)
    cs = p[0:1]
    for r in range(1, layer_j + 1):
        cs = cs + p[r:r + 1]
    lb = cs - p[0:1]

    group_lanes = MXU_COLS
    heads_per_group = group_lanes // HG_DK
    row = lax.broadcasted_iota(jnp.int32, (c, group_lanes), 0)
    tt = lax.broadcasted_iota(jnp.int32, (c, c), 0)
    ss = lax.broadcasted_iota(jnp.int32, (c, c), 1)
    diag_mask = tt == ss
    level_masks = [((tt // (2 * w)) == (ss // (2 * w))) & ((tt & w) != 0) & ((ss & w) == 0)
                   for w in REC_MATMUL_LEVELS + REC_ROW_LEVELS]
    heads = [slice(h * HG_DK, (h + 1) * HG_DK) for h in range(HG_HEADS)]

    def operands(si, rows_c, lanes, after):
        lb_c = lb[:, lanes] if after is None else lb[:, lanes] + 0.0 * after
        f = lb_c + (1.0 - lb_c) * jax.nn.sigmoid(fx_ref[si, rows_c, lanes])
        g = jnp.log2(f)
        kk = 1.0 - f
        g_hi = pltpu.bitcast(pltpu.bitcast(g, jnp.uint32) & jnp.uint32(0xFFFF0000), F32)
        g_pieces = jnp.concatenate([g_hi.astype(BF16), (g - g_hi).astype(BF16)], axis=0)
        ranges = _dot(emat_ref[...], g_pieces)
        b = ranges[0:c]

        q = q_ref[si, rows_c, lanes].astype(F32)
        q_state = (q * jnp.exp2(b)).astype(BF16)
        k_state = (kk * jnp.exp2(b[c - 1:c] - b)).astype(BF16)
        state_decay = jnp.exp2(b[c - 1:c])

        level_ops = []
        for li, w in enumerate(REC_MATMUL_LEVELS):
            x = jnp.exp2(ranges[(li + 1) * c:(li + 2) * c])
            level_ops.append((jnp.where((row & w) != 0, q, kk) * x).astype(BF16))
        for w in REC_ROW_LEVELS:
            pieces = []
            for blk in range(c // w):
                rows = slice(blk * w, (blk + 1) * w)
                mid = (blk // 2) * 2 * w + w - 1
                if blk % 2:
                    pieces.append(q[rows] * jnp.exp2(b[rows] - b[mid:mid + 1]))
                else:
                    pieces.append(kk[rows] * jnp.exp2(b[mid:mid + 1] - b[rows]))
            level_ops.append(jnp.concatenate(pieces, axis=0).astype(BF16))
        done = (jnp.max(level_ops[-1], axis=0, keepdims=True).astype(F32)
                + jnp.max(q_state, axis=0, keepdims=True).astype(F32))
        return (q.astype(BF16), kk.astype(BF16), level_ops, q_state, k_state, state_decay), done

    def recur(si, rows_c, group_ops):
        per_head = []
        for ops in group_ops:
            per_head += [(ops, slice(k * HG_DK, (k + 1) * HG_DK)) for k in range(heads_per_group)]
        att = []
        for (q_bf, k_bf, level_ops, _, _, _), sl in per_head:
            a = jnp.where(diag_mask, _dot_nt(q_bf[:, sl], k_bf[:, sl]), 0.0)
            for mask, y in zip(level_masks, level_ops):
                a = jnp.where(mask, _dot_nt(y[:, sl], y[:, sl]), a)
            att.append(a.astype(BF16))
        for h, ((_, _, _, q_state, _, _), sl) in enumerate(per_head):
            o = (_dot(att[h], v_ref[si, rows_c, heads[h]])
                 + _dot_nt(q_state[:, sl], st_ref[si, h].astype(BF16)))
            o_ref[si, rows_c, heads[h]] = o.astype(o_ref.dtype)
        for h, ((_, _, _, _, k_state, state_decay), sl) in enumerate(per_head):
            st_ref[si, h] = (st_ref[si, h] * state_decay[:, sl]
                             + _dot_tn(v_ref[si, rows_c, heads[h]], k_state[:, sl]))

    def chunk_operands(si, rows_c, after):
        group_ops, done = [], None
        for g in range(d // group_lanes):
            ops, done_g = operands(si, rows_c, slice(g * group_lanes, (g + 1) * group_lanes), after)
            group_ops.append(ops)
            done = done_g if done is None else done + done_g
        return group_ops, done

    chunk_rows = [slice(n * c, (n + 1) * c) for n in range(n_chunks)]
    ops, done = [], None
    for si in range(n_streams):
        first, done = chunk_operands(si, chunk_rows[0], done)
        ops.append(first)
    for n in range(n_chunks):
        ops_next = []
        if n + 1 < n_chunks:
            for si in range(n_streams):
                nxt, done = chunk_operands(si, chunk_rows[n + 1], done)
                ops_next.append(nxt)
        for si in range(n_streams):
            recur(si, chunk_rows[n], ops[si])
        ops = ops_next

    @pl.when(c_idx == pl.num_programs(1) - 1)
    def _():
        for si in range(n_streams):
            for h in range(HG_HEADS):
                sfin_ref[si, h] = st_ref[si, h].T


def _hgrn_rec(q, fx, v, lb_logits, s0, layer_j):
    b, t, d = q.shape
    c = min(t, REC_CHUNK * REC_CHUNKS_PER_STEP)
    ns = REC_STREAMS_PER_STEP
    blk = pl.BlockSpec((ns, c, d), lambda i, j: (i, j, 0))
    st_blk = pl.BlockSpec((ns, HG_HEADS, HG_DK, HG_DK), lambda i, j: (i, 0, 0, 0))
    emat = _range_matrices(REC_CHUNK)
    in_specs = [blk, blk, blk, _resident(lb_logits.shape), _resident(emat.shape)]
    args = [q, fx, v, lb_logits, emat]
    if s0 is not None:
        in_specs.append(st_blk)
        args.append(s0)
    return pl.pallas_call(
        functools.partial(_hgrn_rec_kernel, layer_j, s0 is not None),
        grid=(b // ns, t // c),
        in_specs=in_specs,
        out_specs=[blk, st_blk],
        out_shape=[jax.ShapeDtypeStruct((b, t, d), BF16),
                   jax.ShapeDtypeStruct((b, HG_HEADS, HG_DK, HG_DK), F32)],
        scratch_shapes=[pltpu.VMEM((ns, HG_HEADS, HG_DK, HG_DK), F32)],
        compiler_params=_params("parallel", "arbitrary"),
        name="hgrn_rec",
    )(*args)


def _masked_queries(q_refs):
    tq = q_refs[0].shape[1]
    lane = lax.broadcasted_iota(jnp.int32, (tq, LANES), 1)
    zero = jnp.zeros((tq, LANES), BF16)
    return [[jnp.where(lane < DA_HEAD_DIM, qr[0], zero), jnp.where(lane >= DA_HEAD_DIM, qr[0], zero)]
            for qr in q_refs]


def _lambda_scalar(lam_ref, lam_init):
    lv = lam_ref[...]
    return (jnp.exp(jnp.sum(lv[0:1] * lv[1:2], axis=1, keepdims=True))
            - jnp.exp(jnp.sum(lv[2:3] * lv[3:4], axis=1, keepdims=True)) + lam_init)


def _finish_transposed(lam_ref, subw_ref, lam_init, l_sc, acc_sc, o_ref):
    lam = _lambda_scalar(lam_ref, lam_init)
    for r in range(2):
        o = acc_sc[2 * r] / l_sc[2 * r] - lam * (acc_sc[2 * r + 1] / l_sc[2 * r + 1])
        ms = jnp.mean(o * o, axis=0, keepdims=True)
        o = o * lax.rsqrt(ms + EPS) * subw_ref[...] * (1.0 - lam_init)
        o_ref[0, :, r * LANES:(r + 1) * LANES] = o.T.astype(o_ref.dtype)


def _attn_rows_kernel(lam_init, n_q, slopes_ref, shift_ref, lam_ref, subw_ref, q0_ref, q1_ref,
                      k0_ref, k1_ref, v_ref, o_ref, bias_sc):
    j = pl.program_id(0)
    bi = pl.program_id(1)
    i = pl.program_id(2)
    t = q0_ref.shape[1]

    @pl.when((bi == 0) & (i == 0))
    def _():
        off = lax.broadcasted_iota(jnp.int32, (n_q * t, t), 0) - (n_q - 1) * t
        col = lax.broadcasted_iota(jnp.int32, (n_q * t, t), 1)
        dist = jnp.abs(col - off).astype(F32)
        visible = (off // MASK_CHUNK) <= (col // MASK_CHUNK)
        for r in range(2):
            bias_sc[:, r * t:(r + 1) * t] = jnp.where(
                visible, -slopes_ref[2 * j + r] * dist - shift_ref[0], NEG_INF)

    qm = _masked_queries((q0_ref, q1_ref))
    q_pair = [jnp.concatenate(qm[mp], axis=0) for mp in range(2)]
    k_refs = (k0_ref, k1_ref)
    lam = _lambda_scalar(lam_ref, lam_init)

    def attend(tile_idx):
        rows = (tile_idx + 1) * t
        bias_row0 = (n_q - 1 - tile_idx) * t
        s = [_dot_nt(k_refs[mp][0, 0:rows, :], q_pair[mp]) for mp in range(2)]
        for r in range(2):
            cols = slice(r * t, (r + 1) * t)
            bias = bias_sc[bias_row0:bias_row0 + rows, cols]
            p = [jnp.exp2(s[mp][:, cols] + bias) for mp in range(2)]
            inv0 = 1.0 / jnp.sum(p[0], axis=0, keepdims=True)
            inv1 = lam / jnp.sum(p[1], axis=0, keepdims=True)
            num = _dot_tn(v_ref[0, 0:rows, r * LANES:(r + 1) * LANES],
                          jnp.concatenate([pm.astype(BF16) for pm in p], axis=1))
            o = num[:, :t] * inv0 - num[:, t:] * inv1
            ms = jnp.mean(o * o, axis=0, keepdims=True)
            o = o * lax.rsqrt(ms + EPS) * subw_ref[...] * (1.0 - lam_init)
            o_ref[0, :, r * LANES:(r + 1) * LANES] = o.T.astype(o_ref.dtype)

    for tile_idx in range(n_q):
        pl.when(i == tile_idx)(functools.partial(attend, tile_idx))


def _attn_online_kernel(lam_init, slopes_ref, lam_ref, subw_ref, q0_ref, q1_ref, k0_ref, k1_ref,
                        v_ref, o_ref, m_sc, l_sc, acc_sc):
    j = pl.program_id(1)
    i = pl.program_id(2)
    t = q0_ref.shape[1]
    qm = _masked_queries((q0_ref, q1_ref))
    k_refs = (k0_ref, k1_ref)
    m_sc[...] = jnp.full(m_sc.shape, -jnp.inf, F32)
    l_sc[...] = jnp.zeros(l_sc.shape, F32)
    acc_sc[...] = jnp.zeros(acc_sc.shape, F32)

    def tile(kt_idx, diagonal):
        k_start = pl.multiple_of(kt_idx * t, t)
        row = lax.broadcasted_iota(jnp.int32, (t, t), 0)
        col = lax.broadcasted_iota(jnp.int32, (t, t), 1)
        dist = jnp.abs((i - kt_idx) * t + (col - row)).astype(F32)
        if diagonal:
            visible = (row // MASK_CHUNK) <= (col // MASK_CHUNK)
        for r in range(2):
            bias = -slopes_ref[2 * j + r] * dist
            vt = v_ref[0, pl.ds(k_start, t), r * LANES:(r + 1) * LANES]
            for mp in range(2):
                idx = 2 * r + mp
                kt = k_refs[mp][0, pl.ds(k_start, t), :]
                s = _dot_nt(kt, qm[mp][r]) + bias
                if diagonal:
                    s = jnp.where(visible, s, NEG_INF)
                m_prev = m_sc[idx]
                m_new = jnp.maximum(m_prev, jnp.max(s, axis=0, keepdims=True))
                alpha = jnp.exp2(m_prev - m_new)
                p = jnp.exp2(s - m_new)
                l_sc[idx] = alpha * l_sc[idx] + jnp.sum(p, axis=0, keepdims=True)
                acc_sc[idx] = alpha * acc_sc[idx] + _dot_tn(vt, p.astype(BF16))
                m_sc[idx] = m_new

    def below(kt_idx, carry):
        tile(kt_idx, False)
        return carry

    lax.fori_loop(0, i, below, 0)
    tile(i, True)
    _finish_transposed(lam_ref, subw_ref, lam_init, l_sc, acc_sc, o_ref)


def _prompt_attention(q, k, v, lam_vec, subln_w, slopes2, score_bound2, lam_init):
    b, t, d = q.shape
    tile = ATTN_TILE
    half = d // (2 * LANES)
    q_spec = lambda off: pl.BlockSpec((1, tile, LANES), lambda bi, j, i: (bi, i, j + off))
    k_spec = lambda off: pl.BlockSpec((1, t, LANES), lambda bi, j, i: (bi, 0, j + off))
    v_spec = pl.BlockSpec((1, t, 2 * LANES), lambda bi, j, i: (bi, 0, j))
    o_spec = pl.BlockSpec((1, tile, 2 * LANES), lambda bi, j, i: (bi, i, j))
    smem = pl.BlockSpec(memory_space=pltpu.SMEM)
    lam_spec = pl.BlockSpec(lam_vec.shape, lambda bi, j, i: (0, 0))
    subw_spec = pl.BlockSpec((2 * DA_HEAD_DIM, 1), lambda bi, j, i: (0, 0))
    data_specs = [q_spec(0), q_spec(half), k_spec(0), k_spec(half), v_spec]
    subw = subln_w.reshape(2 * DA_HEAD_DIM, 1)
    stat = pltpu.VMEM((4, 1, tile), F32)
    acc = pltpu.VMEM((4, 2 * DA_HEAD_DIM, tile), F32)
    common = dict(grid=(b, half, t // tile), out_specs=o_spec,
                  out_shape=jax.ShapeDtypeStruct((b, t, d), BF16),
                  compiler_params=_params("parallel", "parallel", "arbitrary"))

    def fixed(shift):
        reorder = lambda spec: pl.BlockSpec(spec.block_shape,
                                            lambda j, bi, i, f=spec.index_map: f(bi, j, i))
        return pl.pallas_call(
            functools.partial(_attn_rows_kernel, lam_init, t // tile),
            grid=(half, b, t // tile),
            in_specs=[smem, smem, reorder(lam_spec), reorder(subw_spec)]
            + [reorder(spec) for spec in data_specs],
            out_specs=reorder(o_spec),
            out_shape=jax.ShapeDtypeStruct((b, t, d), BF16),
            scratch_shapes=[pltpu.VMEM((t, 2 * tile), F32)],
            compiler_params=_params("arbitrary", "arbitrary", "arbitrary"),
            name="diff_attn_fixed",
        )(slopes2, shift, lam_vec, subw, q, q, k, k, v)

    def online(_):
        return pl.pallas_call(
            functools.partial(_attn_online_kernel, lam_init),
            in_specs=[smem, lam_spec, subw_spec] + data_specs,
            scratch_shapes=[stat, stat, acc],
            name="diff_attn_online", **common,
        )(slopes2, lam_vec, subw, q, q, k, k, v)

    return lax.cond(score_bound2[0] <= FIXED_SHIFT_MAX_SCORE * LOG2E, fixed, online, score_bound2)


def _sample_attn_kernel(lam_init, past_visible, slopes_ref, lam_ref, subw_ref, q_ref, kc_ref,
                        kn_ref, vc_ref, vn_ref, o_ref):
    tq = q_ref.shape[1]
    past = kc_ref.shape[3]
    half = DA_HEADS // 2
    lam = _lambda_scalar(lam_ref, lam_init)
    lane = lax.broadcasted_iota(jnp.int32, (tq, LANES), 1)
    zero = jnp.zeros((tq, LANES), BF16)

    row_c = lax.broadcasted_iota(jnp.int32, (tq, past), 0)
    col_c = lax.broadcasted_iota(jnp.int32, (tq, past), 1)
    dist_c = (past + row_c - col_c).astype(F32)
    row_n = lax.broadcasted_iota(jnp.int32, (tq, tq), 0)
    col_n = lax.broadcasted_iota(jnp.int32, (tq, tq), 1)
    dist_n = jnp.abs(row_n - col_n).astype(F32)
    if not past_visible:
        vis_c = (col_c // MASK_CHUNK) <= ((past + row_c) // MASK_CHUNK)
        vis_n = ((past + col_n) // MASK_CHUNK) <= ((past + row_n) // MASK_CHUNK)

    for pair in range(half):
        tiles = [slice((mp * half + pair) * LANES, (mp * half + pair + 1) * LANES) for mp in range(2)]
        kc = [kc_ref[mp, 2 * pair:2 * pair + 2].reshape(LANES, past).astype(BF16) for mp in range(2)]
        for r in range(2):
            head = 2 * pair + r
            slope = slopes_ref[head]
            keep = (lane >= DA_HEAD_DIM) if r else (lane < DA_HEAD_DIM)
            vc = vc_ref[pl.ds(head, past, stride=DA_HEADS), :].astype(BF16)
            vn = vn_ref[0, :, head * LANES:(head + 1) * LANES]
            outs = []
            for mp in range(2):
                qm = jnp.where(keep, q_ref[0, :, tiles[mp]], zero)
                s_c = _dot(qm, kc[mp]) - slope * dist_c
                s_n = _dot_nt(qm, kn_ref[0, :, tiles[mp]]) - slope * dist_n
                if not past_visible:
                    s_c = jnp.where(vis_c, s_c, NEG_INF)
                    s_n = jnp.where(vis_n, s_n, NEG_INF)
                m = jnp.maximum(jnp.max(s_c, axis=1, keepdims=True),
                                jnp.max(s_n, axis=1, keepdims=True))
                p_c = jnp.exp2(s_c - m)
                p_n = jnp.exp2(s_n - m)
                l = jnp.sum(p_c, axis=1, keepdims=True) + jnp.sum(p_n, axis=1, keepdims=True)
                acc = _dot(p_c.astype(BF16), vc) + _dot(p_n.astype(BF16), vn)
                outs.append(acc / l)
            o = outs[0] - lam * outs[1]
            ms = jnp.mean(o * o, axis=1, keepdims=True)
            o = o * lax.rsqrt(ms + EPS) * subw_ref[...] * (1.0 - lam_init)
            o_ref[0, :, head * LANES:(head + 1) * LANES] = o.astype(o_ref.dtype)


def _sample_attention(q, k_new, v_new, k_cache, v_cache, layer_j, lam_vec, subln_w, slopes2,
                      lam_init):
    b, t_q, d = q.shape
    past = k_cache.shape[5]
    past_visible = past % MASK_CHUNK == 0 and t_q <= MASK_CHUNK
    row = pl.BlockSpec((1, t_q, d), lambda bi: (bi, 0, 0))
    return pl.pallas_call(
        functools.partial(_sample_attn_kernel, lam_init, past_visible),
        grid=(b,),
        in_specs=[pl.BlockSpec(memory_space=pltpu.SMEM),
                  pl.BlockSpec(lam_vec.shape, lambda bi: (0, 0)),
                  pl.BlockSpec((1, 2 * DA_HEAD_DIM), lambda bi: (0, 0)),
                  row,
                  pl.BlockSpec((None, None, 2, DA_HEADS, DA_HEAD_DIM, past),
                               lambda bi: (layer_j, bi, 0, 0, 0, 0)),
                  row,
                  pl.BlockSpec((None, None, past * DA_HEADS, 2 * DA_HEAD_DIM),
                               lambda bi: (layer_j, bi, 0, 0)),
                  row],
        out_specs=row,
        out_shape=jax.ShapeDtypeStruct((b, t_q, d), BF16),
        compiler_params=_params("parallel"),
        name="diff_attn_sample",
    )(slopes2, lam_vec, subln_w.reshape(1, 2 * DA_HEAD_DIM), q, k_cache, k_new, v_cache, v_new)


def _row_tile(m):
    return 512 if m % 512 == 0 else 256


def _outproj_tile(m):
    return 1024 if m % 2048 == 0 else _row_tile(m)


def _hgrn_layer(x, s0, nw, w_in_bf, lb_logits, onw, w_out_bf, layer_j):
    b, t, d = x.shape
    x2 = x.reshape(b * t, d)
    tm = _row_tile(b * t)
    q, fx, iv, z = _inproj_hgrn(x2, nw.reshape(1, d), w_in_bf, tm)
    o, s_new = _hgrn_rec(q.reshape(b, t, d), fx.reshape(b, t, d), iv.reshape(b, t, d),
                         lb_logits, s0, layer_j)
    y = _outproj(o.reshape(b * t, d), z, x2, w_out_bf, onw.reshape(1, d), True,
                 _outproj_tile(b * t))
    return y.reshape(b, t, d), s_new


def _attn_layer(x, cache, nw, w_in_bf, qn_w, kn_w, lam_vec, subln_w, w_out_bf, layer_idx,
                layer_j, n_layers, k_stack, v_stack):
    b, t, d = x.shape
    x2 = x.reshape(b * t, d)
    groups = d // DA_HEAD_DIM
    gmat = jnp.kron(jnp.eye(MXU_COLS // DA_HEAD_DIM, dtype=F32),
                    jnp.full((DA_HEAD_DIM, DA_HEAD_DIM), 1.0 / DA_HEAD_DIM, F32)).astype(BF16)
    prompt = cache is None
    tm = 512 if t % 512 == 0 else t
    q, kb, vb, z, k_stack, v_stack = _inproj_attn(
        x2, t, nw.reshape(1, d), w_in_bf, jnp.tile(qn_w, groups).reshape(1, d),
        jnp.tile(kn_w, groups).reshape(1, d), gmat, layer_j, n_layers, k_stack, v_stack, prompt, tm)
    lam_init = 0.8 - 0.6 * math.exp(-0.3 * layer_idx)
    slopes2 = jnp.exp2(-8.0 * jnp.arange(1, DA_HEADS + 1, dtype=F32) / DA_HEADS) * LOG2E
    q, kb, vb = (a.reshape(b, t, d) for a in (q, kb, vb))
    if prompt:
        bound2 = (DA_HEAD_DIM ** 0.5 * LOG2E * 1.02) * jnp.max(jnp.abs(qn_w)) * jnp.max(jnp.abs(kn_w))
        o = _prompt_attention(q, kb, vb, lam_vec, subln_w, slopes2, bound2.reshape(1), lam_init)
    else:
        k_cache, v_cache = cache
        o = _sample_attention(q, kb, vb, k_cache, v_cache, layer_j, lam_vec, subln_w, slopes2,
                              lam_init)
    y = _outproj(o.reshape(b * t, d), z, x2, w_out_bf, jnp.ones((1, d), F32), False,
                 _outproj_tile(b * t))
    return y.reshape(b, t, d), k_stack, v_stack


def kernel(x_prompt, x_sample, cache_k, cache_v, state_hgrn, norm_w, hgrn_w_in, hgrn_lb_logits,
           hgrn_onorm_w, hgrn_w_out, attn_w_in, attn_q_norm, attn_k_norm, attn_lambda, attn_subln,
           attn_w_out):
    depth = norm_w.shape[0]
    n_attn = cache_k.shape[0]
    bp, tp, d = x_prompt.shape
    bs, ts, _ = x_sample.shape
    past = cache_k.shape[2]
    cache_k_t = jnp.transpose(cache_k, (0, 1, 3, 4, 5, 2))
    cache_v2 = cache_v.reshape(n_attn, bs, past * DA_HEADS, 2 * DA_HEAD_DIM)
    yp, ys = x_prompt, x_sample
    kp = vp = ks_ = vs_ = None
    sp, ss = [], []
    for l in range(depth):
        j = l // N_MIXERS
        if l % N_MIXERS == 0:
            w_in = hgrn_w_in[j].astype(BF16)
            w_out = hgrn_w_out[j].astype(BF16)
            yp, s_p = _hgrn_layer(yp, None, norm_w[l], w_in, hgrn_lb_logits, hgrn_onorm_w[j],
                                  w_out, j)
            ys, s_s = _hgrn_layer(ys, state_hgrn[j], norm_w[l], w_in, hgrn_lb_logits,
                                  hgrn_onorm_w[j], w_out, j)
            sp.append(s_p)
            ss.append(s_s)
        else:
            w_in = attn_w_in[j].astype(BF16)
            w_out = attn_w_out[j].astype(BF16)
            yp, kp, vp = _attn_layer(yp, None, norm_w[l], w_in, attn_q_norm[j], attn_k_norm[j],
                                     attn_lambda[j], attn_subln[j], w_out, l, j, n_attn, kp, vp)
            ys, ks_, vs_ = _attn_layer(ys, (cache_k_t, cache_v2), norm_w[l], w_in,
                                       attn_q_norm[j], attn_k_norm[j], attn_lambda[j],
                                       attn_subln[j], w_out, l, j, n_attn, ks_, vs_)
    new_k_prompt = jnp.transpose(kp.reshape(n_attn, bp, 2, DA_HEADS, DA_HEAD_DIM, tp),
                                 (0, 1, 5, 2, 3, 4))
    return (yp, ys, new_k_prompt,
            vp.reshape(n_attn, bp, tp, DA_HEADS, 2 * DA_HEAD_DIM),
            ks_.reshape(n_attn, bs, ts, 2, DA_HEADS, DA_HEAD_DIM),
            vs_.reshape(n_attn, bs, ts, DA_HEADS, 2 * DA_HEAD_DIM),
            jnp.stack(sp), jnp.stack(ss))
```

```python
import functools
import math

import jax
import jax.numpy as jnp
from jax import lax
from jax.experimental import pallas as pl
from jax.experimental.pallas import tpu as pltpu

F32 = jnp.float32
BF16 = jnp.bfloat16

EPS = 1e-6
NEG_INF = -1e30
LOG2E = 1.4426950408889634
LANES = 128
MXU_COLS = 256
HG_HEADS = 8
HG_DK = 128
DA_HEADS = 8
DA_HEAD_DIM = 64
MASK_CHUNK = 64
N_MIXERS = 2

REC_CHUNK = 64
REC_CHUNKS_PER_STEP = 8
REC_STREAMS_PER_STEP = 1
REC_MATMUL_LEVELS = (1, 2, 4)
REC_ROW_LEVELS = (8, 16, 32)
ATTN_TILE = 512
FIXED_SHIFT_MAX_SCORE = 40.0
VMEM_LIMIT = 56 * 1024 * 1024


def _dot(a, b):
    return jnp.dot(a, b, preferred_element_type=F32)


def _dot_nt(a, b):
    return lax.dot_general(a, b, (((1,), (1,)), ((), ())), preferred_element_type=F32)


def _dot_tn(a, b):
    return lax.dot_general(a, b, (((0,), (0,)), ((), ())), preferred_element_type=F32)


def _rms_rows(x, w):
    ms = jnp.mean(x * x, axis=-1, keepdims=True)
    return x * lax.rsqrt(ms + EPS) * w


def _resident(shape):
    nd = len(shape)
    return pl.BlockSpec(shape, lambda *_: (0,) * nd, pipeline_mode=pl.Buffered(1))


def _params(*semantics):
    return pltpu.CompilerParams(dimension_semantics=semantics, vmem_limit_bytes=VMEM_LIMIT)


def _inproj_hgrn_kernel(x_ref, nw_ref, w_ref, q_ref, fx_ref, i_ref, z_ref):
    d = x_ref.shape[1]
    h = _rms_rows(x_ref[...], nw_ref[...]).astype(BF16)
    q_ref[...] = (_dot(h, w_ref[:, 0:d]) * (HG_DK ** -0.5)).astype(q_ref.dtype)
    for c, o_ref in ((1, fx_ref), (2, i_ref), (3, z_ref)):
        o_ref[...] = _dot(h, w_ref[:, c * d:(c + 1) * d]).astype(o_ref.dtype)


def _inproj_hgrn(x2, nw, w_bf, tm):
    m, d = x2.shape
    row = pl.BlockSpec((tm, d), lambda i: (i, 0))
    return pl.pallas_call(
        _inproj_hgrn_kernel,
        grid=(m // tm,),
        in_specs=[row, _resident((1, d)), _resident((d, 4 * d))],
        out_specs=[row, row, row, row],
        out_shape=[jax.ShapeDtypeStruct((m, d), BF16), jax.ShapeDtypeStruct((m, d), F32),
                   jax.ShapeDtypeStruct((m, d), BF16), jax.ShapeDtypeStruct((m, d), BF16)],
        compiler_params=_params("parallel"),
        name="inproj_hgrn",
    )(x2, nw, w_bf)


def _inproj_attn_kernel(k_transposed, n_alias, x_ref, nw_ref, w_ref, qnw_ref, knw_ref, g_ref, *rest):
    q_ref, kb_ref, vb_ref, z_ref, ks_ref, vs_ref = rest[n_alias:]
    d = x_ref.shape[1]
    h = _rms_rows(x_ref[...], nw_ref[...]).astype(BF16)
    gmat = g_ref[...]
    gw = gmat.shape[0]

    def group_norm(y, w):
        cols = []
        for t in range(d // gw):
            yt = y[:, t * gw:(t + 1) * gw]
            ms = _dot((yt * yt).astype(BF16), gmat)
            cols.append(yt * lax.rsqrt(ms + EPS))
        return jnp.concatenate(cols, axis=1) * w

    q = group_norm(_dot(h, w_ref[:, 0:d]), qnw_ref[...])
    q_ref[...] = (q * (DA_HEAD_DIM ** -0.5 * LOG2E)).astype(BF16)
    k = group_norm(_dot(h, w_ref[:, d:2 * d]), knw_ref[...])
    kb_ref[...] = k.astype(BF16)
    ks_ref[0, 0] = k.T if k_transposed else k
    v = _dot(h, w_ref[:, 2 * d:3 * d])
    for hd in range(DA_HEADS):
        vs_ref[0, pl.ds(hd, x_ref.shape[0], stride=DA_HEADS), :] = v[:, hd * LANES:(hd + 1) * LANES]
    vb_ref[...] = v.astype(BF16)
    z_ref[...] = _dot(h, w_ref[:, 3 * d:4 * d]).astype(BF16)


def _inproj_attn(x2, seq, nw, w_bf, qnw, knw, gmat, layer_j, n_layers, k_stack, v_stack,
                 k_transposed, tm):
    m, d = x2.shape
    batch = m // seq
    tiles_per_seq = seq // tm
    row = pl.BlockSpec((tm, d), lambda i: (i, 0))
    if k_transposed:
        ks_shape = (n_layers, batch, d, seq)
        ks_spec = pl.BlockSpec((1, 1, d, tm),
                               lambda i: (layer_j, i // tiles_per_seq, 0, i % tiles_per_seq))
    else:
        ks_shape = (n_layers, batch, seq, d)
        ks_spec = pl.BlockSpec((1, 1, tm, d),
                               lambda i: (layer_j, i // tiles_per_seq, i % tiles_per_seq, 0))
    vs_spec = pl.BlockSpec((1, tm * DA_HEADS, 2 * DA_HEAD_DIM), lambda i: (layer_j, i, 0))
    in_specs = [row, _resident((1, d)), _resident((d, 4 * d)), _resident((1, d)),
                _resident((1, d)), _resident(gmat.shape)]
    args = [x2, nw, w_bf, qnw, knw, gmat]
    aliases = {}
    if k_stack is not None:
        in_specs += [pl.BlockSpec(memory_space=pl.ANY)] * 2
        aliases = {len(args): 4, len(args) + 1: 5}
        args += [k_stack, v_stack]
    shp = lambda dt: jax.ShapeDtypeStruct((m, d), dt)
    return pl.pallas_call(
        functools.partial(_inproj_attn_kernel, k_transposed, len(aliases)),
        grid=(m // tm,),
        in_specs=in_specs,
        out_specs=[row, row, row, row, ks_spec, vs_spec],
        out_shape=[shp(BF16), shp(BF16), shp(BF16), shp(BF16),
                   jax.ShapeDtypeStruct(ks_shape, F32),
                   jax.ShapeDtypeStruct((n_layers, m * DA_HEADS, 2 * DA_HEAD_DIM), F32)],
        input_output_aliases=aliases,
        compiler_params=_params("parallel"),
        name="inproj_attn",
    )(*args)


def _outproj_kernel(full_norm, o_ref, z_ref, x_ref, w_ref, ow_ref, y_ref):
    o = o_ref[...].astype(F32)
    if full_norm:
        o = _rms_rows(o, ow_ref[...])
    z = z_ref[...].astype(F32)
    gated = (o * (z * jax.nn.sigmoid(z))).astype(BF16)
    y_ref[...] = x_ref[...] + _dot(gated, w_ref[...])


def _outproj(o2, z2, x2, w_bf, ow, full_norm, tm):
    m, d = x2.shape
    row = pl.BlockSpec((tm, d), lambda i: (i, 0))
    return pl.pallas_call(
        functools.partial(_outproj_kernel, full_norm),
        grid=(m // tm,),
        in_specs=[row, row, row, _resident((d, d)), _resident((1, d))],
        out_specs=row,
        out_shape=jax.ShapeDtypeStruct((m, d), F32),
        compiler_params=_params("parallel"),
        name="outproj",
    )(o2, z2, x2, w_bf, ow)


def _range_matrices(c):
    t = jnp.arange(c)[:, None]
    s = jnp.arange(c)[None, :]
    blocks = [s <= t]
    for w in REC_MATMUL_LEVELS:
        ref = (t // (2 * w)) * (2 * w) + w - 1
        upper = (t & w) != 0
        blocks.append(jnp.where(upper, (s > ref) & (s <= t), (s > t) & (s <= ref)))
    e = jnp.concatenate(blocks, axis=0).astype(BF16)
    return jnp.concatenate([e, e], axis=1)


def _hgrn_rec_kernel(layer_j, has_s0, q_ref, fx_ref, v_ref, lbl_ref, emat_ref, *rest):
    if has_s0:
        s0_ref, o_ref, sfin_ref, st_ref = rest
    else:
        o_ref, sfin_ref, st_ref = rest
    c_idx = pl.program_id(1)
    c = REC_CHUNK
    n_streams = q_ref.shape[0]
    n_chunks = q_ref.shape[1] // c
    d = q_ref.shape[2]

    @pl.when(c_idx == 0)
    def _():
        for si in range(n_streams):
            for h in range(HG_HEADS):
                if has_s0:
                    st_ref[si, h] = s0_ref[si, h].T
                else:
                    st_ref[si, h] = jnp.zeros((HG_DK, HG_DK), F32)

    lg = lbl_ref[...]
    ex = jnp.exp(lg - jnp.max(lg, axis=0, keepdims=True))
    p = ex / jnp.sum(ex, axis=0, keepdims=True)
    cs = p[0:1]
    for r in range(1, layer_j + 1):
        cs = cs + p[r:r + 1]
    lb = cs - p[0:1]

    group_lanes = MXU_COLS
    heads_per_group = group_lanes // HG_DK
    row = lax.broadcasted_iota(jnp.int32, (c, group_lanes), 0)
    tt = lax.broadcasted_iota(jnp.int32, (c, c), 0)
    ss = lax.broadcasted_iota(jnp.int32, (c, c), 1)
    diag_mask = tt == ss
    level_masks = [((tt // (2 * w)) == (ss // (2 * w))) & ((tt & w) != 0) & ((ss & w) == 0)
                   for w in REC_MATMUL_LEVELS + REC_ROW_LEVELS]
    heads = [slice(h * HG_DK, (h + 1) * HG_DK) for h in range(HG_HEADS)]

    def operands(si, rows_c, lanes, after):
        lb_c = lb[:, lanes] if after is None else lb[:, lanes] + 0.0 * after
        f = lb_c + (1.0 - lb_c) * jax.nn.sigmoid(fx_ref[si, rows_c, lanes])
        g = jnp.log2(f)
        kk = 1.0 - f
        g_hi = pltpu.bitcast(pltpu.bitcast(g, jnp.uint32) & jnp.uint32(0xFFFF0000), F32)
        g_pieces = jnp.concatenate([g_hi.astype(BF16), (g - g_hi).astype(BF16)], axis=0)
        ranges = _dot(emat_ref[...], g_pieces)
        b = ranges[0:c]

        q = q_ref[si, rows_c, lanes].astype(F32)
        q_state = (q * jnp.exp2(b)).astype(BF16)
        k_state = (kk * jnp.exp2(b[c - 1:c] - b)).astype(BF16)
        state_decay = jnp.exp2(b[c - 1:c])

        level_ops = []
        for li, w in enumerate(REC_MATMUL_LEVELS):
            x = jnp.exp2(ranges[(li + 1) * c:(li + 2) * c])
            level_ops.append((jnp.where((row & w) != 0, q, kk) * x).astype(BF16))
        for w in REC_ROW_LEVELS:
            pieces = []
            for blk in range(c // w):
                rows = slice(blk * w, (blk + 1) * w)
                mid = (blk // 2) * 2 * w + w - 1
                if blk % 2:
                    pieces.append(q[rows] * jnp.exp2(b[rows] - b[mid:mid + 1]))
                else:
                    pieces.append(kk[rows] * jnp.exp2(b[mid:mid + 1] - b[rows]))
            level_ops.append(jnp.concatenate(pieces, axis=0).astype(BF16))
        done = (jnp.max(level_ops[-1], axis=0, keepdims=True).astype(F32)
                + jnp.max(q_state, axis=0, keepdims=True).astype(F32))
        return (q.astype(BF16), kk.astype(BF16), level_ops, q_state, k_state, state_decay), done

    def recur(si, rows_c, group_ops):
        per_head = []
        for ops in group_ops:
            per_head += [(ops, slice(k * HG_DK, (k + 1) * HG_DK)) for k in range(heads_per_group)]
        att = []
        for (q_bf, k_bf, level_ops, _, _, _), sl in per_head:
            a = jnp.where(diag_mask, _dot_nt(q_bf[:, sl], k_bf[:, sl]), 0.0)
            for mask, y in zip(level_masks, level_ops):
                a = jnp.where(mask, _dot_nt(y[:, sl], y[:, sl]), a)
            att.append(a.astype(BF16))
        for h, ((_, _, _, q_state, _, _), sl) in enumerate(per_head):
            o = (_dot(att[h], v_ref[si, rows_c, heads[h]])
                 + _dot_nt(q_state[:, sl], st_ref[si, h].astype(BF16)))
            o_ref[si, rows_c, heads[h]] = o.astype(o_ref.dtype)
        for h, ((_, _, _, _, k_state, state_decay), sl) in enumerate(per_head):
            st_ref[si, h] = (st_ref[si, h] * state_decay[:, sl]
                             + _dot_tn(v_ref[si, rows_c, heads[h]], k_state[:, sl]))

    def chunk_operands(si, rows_c, after):
        group_ops, done = [], None
        for g in range(d // group_lanes):
            ops, done_g = operands(si, rows_c, slice(g * group_lanes, (g + 1) * group_lanes), after)
            group_ops.append(ops)
            done = done_g if done is None else done + done_g
        return group_ops, done

    chunk_rows = [slice(n * c, (n + 1) * c) for n in range(n_chunks)]
    ops, done = [], None
    for si in range(n_streams):
        first, done = chunk_operands(si, chunk_rows[0], done)
        ops.append(first)
    for n in range(n_chunks):
        ops_next = []
        if n + 1 < n_chunks:
            for si in range(n_streams):
                nxt, done = chunk_operands(si, chunk_rows[n + 1], done)
                ops_next.append(nxt)
        for si in range(n_streams):
            recur(si, chunk_rows[n], ops[si])
        ops = ops_next

    @pl.when(c_idx == pl.num_programs(1) - 1)
    def _():
        for si in range(n_streams):
            for h in range(HG_HEADS):
                sfin_ref[si, h] = st_ref[si, h].T


def _hgrn_rec(q, fx, v, lb_logits, s0, layer_j):
    b, t, d = q.shape
    c = min(t, REC_CHUNK * REC_CHUNKS_PER_STEP)
    ns = REC_STREAMS_PER_STEP
    blk = pl.BlockSpec((ns, c, d), lambda i, j: (i, j, 0))
    st_blk = pl.BlockSpec((ns, HG_HEADS, HG_DK, HG_DK), lambda i, j: (i, 0, 0, 0))
    emat = _range_matrices(REC_CHUNK)
    in_specs = [blk, blk, blk, _resident(lb_logits.shape), _resident(emat.shape)]
    args = [q, fx, v, lb_logits, emat]
    if s0 is not None:
        in_specs.append(st_blk)
        args.append(s0)
    return pl.pallas_call(
        functools.partial(_hgrn_rec_kernel, layer_j, s0 is not None),
        grid=(b // ns, t // c),
        in_specs=in_specs,
        out_specs=[blk, st_blk],
        out_shape=[jax.ShapeDtypeStruct((b, t, d), BF16),
                   jax.ShapeDtypeStruct((b, HG_HEADS, HG_DK, HG_DK), F32)],
        scratch_shapes=[pltpu.VMEM((ns, HG_HEADS, HG_DK, HG_DK), F32)],
        compiler_params=_params("parallel", "arbitrary"),
        name="hgrn_rec",
    )(*args)


def _masked_queries(q_refs):
    tq = q_refs[0].shape[1]
    lane = lax.broadcasted_iota(jnp.int32, (tq, LANES), 1)
    zero = jnp.zeros((tq, LANES), BF16)
    return [[jnp.where(lane < DA_HEAD_DIM, qr[0], zero), jnp.where(lane >= DA_HEAD_DIM, qr[0], zero)]
            for qr in q_refs]


def _lambda_scalar(lam_ref, lam_init):
    lv = lam_ref[...]
    return (jnp.exp(jnp.sum(lv[0:1] * lv[1:2], axis=1, keepdims=True))
            - jnp.exp(jnp.sum(lv[2:3] * lv[3:4], axis=1, keepdims=True)) + lam_init)


def _finish_transposed(lam_ref, subw_ref, lam_init, l_sc, acc_sc, o_ref):
    lam = _lambda_scalar(lam_ref, lam_init)
    for r in range(2):
        o = acc_sc[2 * r] / l_sc[2 * r] - lam * (acc_sc[2 * r + 1] / l_sc[2 * r + 1])
        ms = jnp.mean(o * o, axis=0, keepdims=True)
        o = o * lax.rsqrt(ms + EPS) * subw_ref[...] * (1.0 - lam_init)
        o_ref[0, :, r * LANES:(r + 1) * LANES] = o.T.astype(o_ref.dtype)


def _attn_rows_kernel(lam_init, n_q, slopes_ref, shift_ref, lam_ref, subw_ref, q0_ref, q1_ref,
                      k0_ref, k1_ref, v_ref, o_ref, bias_sc):
    j = pl.program_id(0)
    bi = pl.program_id(1)
    i = pl.program_id(2)
    t = q0_ref.shape[1]

    @pl.when((bi == 0) & (i == 0))
    def _():
        off = lax.broadcasted_iota(jnp.int32, (n_q * t, t), 0) - (n_q - 1) * t
        col = lax.broadcasted_iota(jnp.int32, (n_q * t, t), 1)
        dist = jnp.abs(col - off).astype(F32)
        visible = (off // MASK_CHUNK) <= (col // MASK_CHUNK)
        for r in range(2):
            bias_sc[:, r * t:(r + 1) * t] = jnp.where(
                visible, -slopes_ref[2 * j + r] * dist - shift_ref[0], NEG_INF)

    qm = _masked_queries((q0_ref, q1_ref))
    q_pair = [jnp.concatenate(qm[mp], axis=0) for mp in range(2)]
    k_refs = (k0_ref, k1_ref)
    lam = _lambda_scalar(lam_ref, lam_init)

    def attend(tile_idx):
        rows = (tile_idx + 1) * t
        bias_row0 = (n_q - 1 - tile_idx) * t
        s = [_dot_nt(k_refs[mp][0, 0:rows, :], q_pair[mp]) for mp in range(2)]
        for r in range(2):
            cols = slice(r * t, (r + 1) * t)
            bias = bias_sc[bias_row0:bias_row0 + rows, cols]
            p = [jnp.exp2(s[mp][:, cols] + bias) for mp in range(2)]
            inv0 = 1.0 / jnp.sum(p[0], axis=0, keepdims=True)
            inv1 = lam / jnp.sum(p[1], axis=0, keepdims=True)
            num = _dot_tn(v_ref[0, 0:rows, r * LANES:(r + 1) * LANES],
                          jnp.concatenate([pm.astype(BF16) for pm in p], axis=1))
            o = num[:, :t] * inv0 - num[:, t:] * inv1
            ms = jnp.mean(o * o, axis=0, keepdims=True)
            o = o * lax.rsqrt(ms + EPS) * subw_ref[...] * (1.0 - lam_init)
            o_ref[0, :, r * LANES:(r + 1) * LANES] = o.T.astype(o_ref.dtype)

    for tile_idx in range(n_q):
        pl.when(i == tile_idx)(functools.partial(attend, tile_idx))


def _attn_online_kernel(lam_init, slopes_ref, lam_ref, subw_ref, q0_ref, q1_ref, k0_ref, k1_ref,
                        v_ref, o_ref, m_sc, l_sc, acc_sc):
    j = pl.program_id(1)
    i = pl.program_id(2)
    t = q0_ref.shape[1]
    qm = _masked_queries((q0_ref, q1_ref))
    k_refs = (k0_ref, k1_ref)
    m_sc[...] = jnp.full(m_sc.shape, -jnp.inf, F32)
    l_sc[...] = jnp.zeros(l_sc.shape, F32)
    acc_sc[...] = jnp.zeros(acc_sc.shape, F32)

    def tile(kt_idx, diagonal):
        k_start = pl.multiple_of(kt_idx * t, t)
        row = lax.broadcasted_iota(jnp.int32, (t, t), 0)
        col = lax.broadcasted_iota(jnp.int32, (t, t), 1)
        dist = jnp.abs((i - kt_idx) * t + (col - row)).astype(F32)
        if diagonal:
            visible = (row // MASK_CHUNK) <= (col // MASK_CHUNK)
        for r in range(2):
            bias = -slopes_ref[2 * j + r] * dist
            vt = v_ref[0, pl.ds(k_start, t), r * LANES:(r + 1) * LANES]
            for mp in range(2):
                idx = 2 * r + mp
                kt = k_refs[mp][0, pl.ds(k_start, t), :]
                s = _dot_nt(kt, qm[mp][r]) + bias
                if diagonal:
                    s = jnp.where(visible, s, NEG_INF)
                m_prev = m_sc[idx]
                m_new = jnp.maximum(m_prev, jnp.max(s, axis=0, keepdims=True))
                alpha = jnp.exp2(m_prev - m_new)
                p = jnp.exp2(s - m_new)
                l_sc[idx] = alpha * l_sc[idx] + jnp.sum(p, axis=0, keepdims=True)
                acc_sc[idx] = alpha * acc_sc[idx] + _dot_tn(vt, p.astype(BF16))
                m_sc[idx] = m_new

    def below(kt_idx, carry):
        tile(kt_idx, False)
        return carry

    lax.fori_loop(0, i, below, 0)
    tile(i, True)
    _finish_transposed(lam_ref, subw_ref, lam_init, l_sc, acc_sc, o_ref)


def _prompt_attention(q, k, v, lam_vec, subln_w, slopes2, score_bound2, lam_init):
    b, t, d = q.shape
    tile = ATTN_TILE
    half = d // (2 * LANES)
    q_spec = lambda off: pl.BlockSpec((1, tile, LANES), lambda bi, j, i: (bi, i, j + off))
    k_spec = lambda off: pl.BlockSpec((1, t, LANES), lambda bi, j, i: (bi, 0, j + off))
    v_spec = pl.BlockSpec((1, t, 2 * LANES), lambda bi, j, i: (bi, 0, j))
    o_spec = pl.BlockSpec((1, tile, 2 * LANES), lambda bi, j, i: (bi, i, j))
    smem = pl.BlockSpec(memory_space=pltpu.SMEM)
    lam_spec = pl.BlockSpec(lam_vec.shape, lambda bi, j, i: (0, 0))
    subw_spec = pl.BlockSpec((2 * DA_HEAD_DIM, 1), lambda bi, j, i: (0, 0))
    data_specs = [q_spec(0), q_spec(half), k_spec(0), k_spec(half), v_spec]
    subw = subln_w.reshape(2 * DA_HEAD_DIM, 1)
    stat = pltpu.VMEM((4, 1, tile), F32)
    acc = pltpu.VMEM((4, 2 * DA_HEAD_DIM, tile), F32)
    common = dict(grid=(b, half, t // tile), out_specs=o_spec,
                  out_shape=jax.ShapeDtypeStruct((b, t, d), BF16),
                  compiler_params=_params("parallel", "parallel", "arbitrary"))

    def fixed(shift):
        reorder = lambda spec: pl.BlockSpec(spec.block_shape,
                                            lambda j, bi, i, f=spec.index_map: f(bi, j, i))
        return pl.pallas_call(
            functools.partial(_attn_rows_kernel, lam_init, t // tile),
            grid=(half, b, t // tile),
            in_specs=[smem, smem, reorder(lam_spec), reorder(subw_spec)]
            + [reorder(spec) for spec in data_specs],
            out_specs=reorder(o_spec),
            out_shape=jax.ShapeDtypeStruct((b, t, d), BF16),
            scratch_shapes=[pltpu.VMEM((t, 2 * tile), F32)],
            compiler_params=_params("arbitrary", "arbitrary", "arbitrary"),
            name="diff_attn_fixed",
        )(slopes2, shift, lam_vec, subw, q, q, k, k, v)

    def online(_):
        return pl.pallas_call(
            functools.partial(_attn_online_kernel, lam_init),
            in_specs=[smem, lam_spec, subw_spec] + data_specs,
            scratch_shapes=[stat, stat, acc],
            name="diff_attn_online", **common,
        )(slopes2, lam_vec, subw, q, q, k, k, v)

    return lax.cond(score_bound2[0] <= FIXED_SHIFT_MAX_SCORE * LOG2E, fixed, online, score_bound2)


def _sample_attn_kernel(lam_init, past_visible, slopes_ref, lam_ref, subw_ref, q_ref, kc_ref,
                        kn_ref, vc_ref, vn_ref, o_ref):
    tq = q_ref.shape[1]
    past = kc_ref.shape[3]
    half = DA_HEADS // 2
    lam = _lambda_scalar(lam_ref, lam_init)
    lane = lax.broadcasted_iota(jnp.int32, (tq, LANES), 1)
    zero = jnp.zeros((tq, LANES), BF16)

    row_c = lax.broadcasted_iota(jnp.int32, (tq, past), 0)
    col_c = lax.broadcasted_iota(jnp.int32, (tq, past), 1)
    dist_c = (past + row_c - col_c).astype(F32)
    row_n = lax.broadcasted_iota(jnp.int32, (tq, tq), 0)
    col_n = lax.broadcasted_iota(jnp.int32, (tq, tq), 1)
    dist_n = jnp.abs(row_n - col_n).astype(F32)
    if not past_visible:
        vis_c = (col_c // MASK_CHUNK) <= ((past + row_c) // MASK_CHUNK)
        vis_n = ((past + col_n) // MASK_CHUNK) <= ((past + row_n) // MASK_CHUNK)

    for pair in range(half):
        tiles = [slice((mp * half + pair) * LANES, (mp * half + pair + 1) * LANES) for mp in range(2)]
        kc = [kc_ref[mp, 2 * pair:2 * pair + 2].reshape(LANES, past).astype(BF16) for mp in range(2)]
        for r in range(2):
            head = 2 * pair + r
            slope = slopes_ref[head]
            keep = (lane >= DA_HEAD_DIM) if r else (lane < DA_HEAD_DIM)
            vc = vc_ref[pl.ds(head, past, stride=DA_HEADS), :].astype(BF16)
            vn = vn_ref[0, :, head * LANES:(head + 1) * LANES]
            outs = []
            for mp in range(2):
                qm = jnp.where(keep, q_ref[0, :, tiles[mp]], zero)
                s_c = _dot(qm, kc[mp]) - slope * dist_c
                s_n = _dot_nt(qm, kn_ref[0, :, tiles[mp]]) - slope * dist_n
                if not past_visible:
                    s_c = jnp.where(vis_c, s_c, NEG_INF)
                    s_n = jnp.where(vis_n, s_n, NEG_INF)
                m = jnp.maximum(jnp.max(s_c, axis=1, keepdims=True),
                                jnp.max(s_n, axis=1, keepdims=True))
                p_c = jnp.exp2(s_c - m)
                p_n = jnp.exp2(s_n - m)
                l = jnp.sum(p_c, axis=1, keepdims=True) + jnp.sum(p_n, axis=1, keepdims=True)
                acc = _dot(p_c.astype(BF16), vc) + _dot(p_n.astype(BF16), vn)
                outs.append(acc / l)
            o = outs[0] - lam * outs[1]
            ms = jnp.mean(o * o, axis=1, keepdims=True)
            o = o * lax.rsqrt(ms + EPS) * subw_ref[...] * (1.0 - lam_init)
            o_ref[0, :, head * LANES:(head + 1) * LANES] = o.astype(o_ref.dtype)


def _sample_attention(q, k_new, v_new, k_cache, v_cache, layer_j, lam_vec, subln_w, slopes2,
                      lam_init):
    b, t_q, d = q.shape
    past = k_cache.shape[5]
    past_visible = past % MASK_CHUNK == 0 and t_q <= MASK_CHUNK
    row = pl.BlockSpec((1, t_q, d), lambda bi: (bi, 0, 0))
    return pl.pallas_call(
        functools.partial(_sample_attn_kernel, lam_init, past_visible),
        grid=(b,),
        in_specs=[pl.BlockSpec(memory_space=pltpu.SMEM),
                  pl.BlockSpec(lam_vec.shape, lambda bi: (0, 0)),
                  pl.BlockSpec((1, 2 * DA_HEAD_DIM), lambda bi: (0, 0)),
                  row,
                  pl.BlockSpec((None, None, 2, DA_HEADS, DA_HEAD_DIM, past),
                               lambda bi: (layer_j, bi, 0, 0, 0, 0)),
                  row,
                  pl.BlockSpec((None, None, past * DA_HEADS, 2 * DA_HEAD_DIM),
                               lambda bi: (layer_j, bi, 0, 0)),
                  row],
        out_specs=row,
        out_shape=jax.ShapeDtypeStruct((b, t_q, d), BF16),
        compiler_params=_params("parallel"),
        name="diff_attn_sample",
    )(slopes2, lam_vec, subln_w.reshape(1, 2 * DA_HEAD_DIM), q, k_cache, k_new, v_cache, v_new)


def _row_tile(m):
    return 512 if m % 512 == 0 else 256


def _outproj_tile(m):
    return 1024 if m % 2048 == 0 else _row_tile(m)


def _hgrn_layer(x, s0, nw, w_in_bf, lb_logits, onw, w_out_bf, layer_j):
    b, t, d = x.shape
    x2 = x.reshape(b * t, d)
    q, fx, iv, z = _inproj_hgrn(x2, nw.reshape(1, d), w_in_bf, _outproj_tile(b * t))
    o, s_new = _hgrn_rec(q.reshape(b, t, d), fx.reshape(b, t, d), iv.reshape(b, t, d),
                         lb_logits, s0, layer_j)
    y = _outproj(o.reshape(b * t, d), z, x2, w_out_bf, onw.reshape(1, d), True,
                 _outproj_tile(b * t))
    return y.reshape(b, t, d), s_new


def _attn_layer(x, cache, nw, w_in_bf, qn_w, kn_w, lam_vec, subln_w, w_out_bf, layer_idx,
                layer_j, n_layers, k_stack, v_stack):
    b, t, d = x.shape
    x2 = x.reshape(b * t, d)
    groups = d // DA_HEAD_DIM
    gmat = jnp.kron(jnp.eye(MXU_COLS // DA_HEAD_DIM, dtype=F32),
                    jnp.full((DA_HEAD_DIM, DA_HEAD_DIM), 1.0 / DA_HEAD_DIM, F32)).astype(BF16)
    prompt = cache is None
    tm = 512 if t % 512 == 0 else t
    q, kb, vb, z, k_stack, v_stack = _inproj_attn(
        x2, t, nw.reshape(1, d), w_in_bf, jnp.tile(qn_w, groups).reshape(1, d),
        jnp.tile(kn_w, groups).reshape(1, d), gmat, layer_j, n_layers, k_stack, v_stack, prompt, tm)
    lam_init = 0.8 - 0.6 * math.exp(-0.3 * layer_idx)
    slopes2 = jnp.exp2(-8.0 * jnp.arange(1, DA_HEADS + 1, dtype=F32) / DA_HEADS) * LOG2E
    q, kb, vb = (a.reshape(b, t, d) for a in (q, kb, vb))
    if prompt:
        bound2 = (DA_HEAD_DIM ** 0.5 * LOG2E * 1.02) * jnp.max(jnp.abs(qn_w)) * jnp.max(jnp.abs(kn_w))
        o = _prompt_attention(q, kb, vb, lam_vec, subln_w, slopes2, bound2.reshape(1), lam_init)
    else:
        k_cache, v_cache = cache
        o = _sample_attention(q, kb, vb, k_cache, v_cache, layer_j, lam_vec, subln_w, slopes2,
                              lam_init)
    y = _outproj(o.reshape(b * t, d), z, x2, w_out_bf, jnp.ones((1, d), F32), False,
                 _outproj_tile(b * t))
    return y.reshape(b, t, d), k_stack, v_stack


def kernel(x_prompt, x_sample, cache_k, cache_v, state_hgrn, norm_w, hgrn_w_in, hgrn_lb_logits,
           hgrn_onorm_w, hgrn_w_out, attn_w_in, attn_q_norm, attn_k_norm, attn_lambda, attn_subln,
           attn_w_out):
    depth = norm_w.shape[0]
    n_attn = cache_k.shape[0]
    bp, tp, d = x_prompt.shape
    bs, ts, _ = x_sample.shape
    past = cache_k.shape[2]
    cache_k_t = jnp.transpose(cache_k, (0, 1, 3, 4, 5, 2))
    cache_v2 = cache_v.reshape(n_attn, bs, past * DA_HEADS, 2 * DA_HEAD_DIM)
    yp, ys = x_prompt, x_sample
    kp = vp = ks_ = vs_ = None
    sp, ss = [], []
    for l in range(depth):
        j = l // N_MIXERS
        if l % N_MIXERS == 0:
            w_in = hgrn_w_in[j].astype(BF16)
            w_out = hgrn_w_out[j].astype(BF16)
            yp, s_p = _hgrn_layer(yp, None, norm_w[l], w_in, hgrn_lb_logits, hgrn_onorm_w[j],
                                  w_out, j)
            ys, s_s = _hgrn_layer(ys, state_hgrn[j], norm_w[l], w_in, hgrn_lb_logits,
                                  hgrn_onorm_w[j], w_out, j)
            sp.append(s_p)
            ss.append(s_s)
        else:
            w_in = attn_w_in[j].astype(BF16)
            w_out = attn_w_out[j].astype(BF16)
            yp, kp, vp = _attn_layer(yp, None, norm_w[l], w_in, attn_q_norm[j], attn_k_norm[j],
                                     attn_lambda[j], attn_subln[j], w_out, l, j, n_attn, kp, vp)
            ys, ks_, vs_ = _attn_layer(ys, (cache_k_t, cache_v2), norm_w[l], w_in,
                                       attn_q_norm[j], attn_k_norm[j], attn_lambda[j],
                                       attn_subln[j], w_out, l, j, n_attn, ks_, vs_)
    new_k_prompt = jnp.transpose(kp.reshape(n_attn, bp, 2, DA_HEADS, DA_HEAD_DIM, tp),
                                 (0, 1, 5, 2, 3, 4))
    return (yp, ys, new_k_prompt,
            vp.reshape(n_attn, bp, tp, DA_HEADS, 2 * DA_HEAD_DIM),
            ks_.reshape(n_attn, bs, ts, 2, DA_HEADS, DA_HEAD_DIM),
            vs_.reshape(n_attn, bs, ts, DA_HEADS, 2 * DA_HEAD_DIM),
            jnp.stack(sp), jnp.stack(ss))
```

```python
import functools
import math

import jax
import jax.numpy as jnp
from jax import lax
from jax.experimental import pallas as pl
from jax.experimental.pallas import tpu as pltpu

F32 = jnp.float32
BF16 = jnp.bfloat16

EPS = 1e-6
NEG_INF = -1e30
LOG2E = 1.4426950408889634
LANES = 128
MXU_COLS = 256
HG_HEADS = 8
HG_DK = 128
DA_HEADS = 8
DA_HEAD_DIM = 64
MASK_CHUNK = 64
N_MIXERS = 2

REC_CHUNK = 64
REC_CHUNKS_PER_STEP = 16
REC_STREAMS_PER_STEP = 1
REC_MATMUL_LEVELS = (1, 2, 4)
REC_ROW_LEVELS = (8, 16, 32)
ATTN_TILE = 512
FIXED_SHIFT_MAX_SCORE = 40.0
VMEM_LIMIT = 56 * 1024 * 1024


def _dot(a, b):
    return jnp.dot(a, b, preferred_element_type=F32)


def _dot_nt(a, b):
    return lax.dot_general(a, b, (((1,), (1,)), ((), ())), preferred_element_type=F32)


def _dot_tn(a, b):
    return lax.dot_general(a, b, (((0,), (0,)), ((), ())), preferred_element_type=F32)


def _rms_rows(x, w):
    ms = jnp.mean(x * x, axis=-1, keepdims=True)
    return x * lax.rsqrt(ms + EPS) * w


def _resident(shape):
    nd = len(shape)
    return pl.BlockSpec(shape, lambda *_: (0,) * nd, pipeline_mode=pl.Buffered(1))


def _params(*semantics):
    return pltpu.CompilerParams(dimension_semantics=semantics, vmem_limit_bytes=VMEM_LIMIT)


def _inproj_hgrn_kernel(x_ref, nw_ref, w_ref, q_ref, fx_ref, i_ref, z_ref):
    d = x_ref.shape[1]
    h = _rms_rows(x_ref[...], nw_ref[...]).astype(BF16)
    q_ref[...] = (_dot(h, w_ref[:, 0:d]) * (HG_DK ** -0.5)).astype(q_ref.dtype)
    for c, o_ref in ((1, fx_ref), (2, i_ref), (3, z_ref)):
        o_ref[...] = _dot(h, w_ref[:, c * d:(c + 1) * d]).astype(o_ref.dtype)


def _inproj_hgrn(x2, nw, w_bf, tm):
    m, d = x2.shape
    row = pl.BlockSpec((tm, d), lambda i: (i, 0))
    return pl.pallas_call(
        _inproj_hgrn_kernel,
        grid=(m // tm,),
        in_specs=[row, _resident((1, d)), _resident((d, 4 * d))],
        out_specs=[row, row, row, row],
        out_shape=[jax.ShapeDtypeStruct((m, d), BF16), jax.ShapeDtypeStruct((m, d), F32),
                   jax.ShapeDtypeStruct((m, d), BF16), jax.ShapeDtypeStruct((m, d), BF16)],
        compiler_params=_params("parallel"),
        name="inproj_hgrn",
    )(x2, nw, w_bf)


def _inproj_attn_kernel(k_transposed, n_alias, x_ref, nw_ref, w_ref, qnw_ref, knw_ref, g_ref, *rest):
    q_ref, kb_ref, vb_ref, z_ref, ks_ref, vs_ref = rest[n_alias:]
    d = x_ref.shape[1]
    h = _rms_rows(x_ref[...], nw_ref[...]).astype(BF16)
    gmat = g_ref[...]
    gw = gmat.shape[0]

    def group_norm(y, w):
        cols = []
        for t in range(d // gw):
            yt = y[:, t * gw:(t + 1) * gw]
            ms = _dot((yt * yt).astype(BF16), gmat)
            cols.append(yt * lax.rsqrt(ms + EPS))
        return jnp.concatenate(cols, axis=1) * w

    q = group_norm(_dot(h, w_ref[:, 0:d]), qnw_ref[...])
    q_ref[...] = (q * (DA_HEAD_DIM ** -0.5 * LOG2E)).astype(BF16)
    k = group_norm(_dot(h, w_ref[:, d:2 * d]), knw_ref[...])
    kb_ref[...] = k.astype(BF16)
    ks_ref[0, 0] = k.T if k_transposed else k
    v = _dot(h, w_ref[:, 2 * d:3 * d])
    for hd in range(DA_HEADS):
        vs_ref[0, pl.ds(hd, x_ref.shape[0], stride=DA_HEADS), :] = v[:, hd * LANES:(hd + 1) * LANES]
    vb_ref[...] = v.astype(BF16)
    z_ref[...] = _dot(h, w_ref[:, 3 * d:4 * d]).astype(BF16)


def _inproj_attn(x2, seq, nw, w_bf, qnw, knw, gmat, layer_j, n_layers, k_stack, v_stack,
                 k_transposed, tm):
    m, d = x2.shape
    batch = m // seq
    tiles_per_seq = seq // tm
    row = pl.BlockSpec((tm, d), lambda i: (i, 0))
    if k_transposed:
        ks_shape = (n_layers, batch, d, seq)
        ks_spec = pl.BlockSpec((1, 1, d, tm),
                               lambda i: (layer_j, i // tiles_per_seq, 0, i % tiles_per_seq))
    else:
        ks_shape = (n_layers, batch, seq, d)
        ks_spec = pl.BlockSpec((1, 1, tm, d),
                               lambda i: (layer_j, i // tiles_per_seq, i % tiles_per_seq, 0))
    vs_spec = pl.BlockSpec((1, tm * DA_HEADS, 2 * DA_HEAD_DIM), lambda i: (layer_j, i, 0))
    in_specs = [row, _resident((1, d)), _resident((d, 4 * d)), _resident((1, d)),
                _resident((1, d)), _resident(gmat.shape)]
    args = [x2, nw, w_bf, qnw, knw, gmat]
    aliases = {}
    if k_stack is not None:
        in_specs += [pl.BlockSpec(memory_space=pl.ANY)] * 2
        aliases = {len(args): 4, len(args) + 1: 5}
        args += [k_stack, v_stack]
    shp = lambda dt: jax.ShapeDtypeStruct((m, d), dt)
    return pl.pallas_call(
        functools.partial(_inproj_attn_kernel, k_transposed, len(aliases)),
        grid=(m // tm,),
        in_specs=in_specs,
        out_specs=[row, row, row, row, ks_spec, vs_spec],
        out_shape=[shp(BF16), shp(BF16), shp(BF16), shp(BF16),
                   jax.ShapeDtypeStruct(ks_shape, F32),
                   jax.ShapeDtypeStruct((n_layers, m * DA_HEADS, 2 * DA_HEAD_DIM), F32)],
        input_output_aliases=aliases,
        compiler_params=_params("parallel"),
        name="inproj_attn",
    )(*args)


def _outproj_kernel(full_norm, o_ref, z_ref, x_ref, w_ref, ow_ref, y_ref):
    o = o_ref[...].astype(F32)
    if full_norm:
        o = _rms_rows(o, ow_ref[...])
    z = z_ref[...].astype(F32)
    gated = (o * (z * jax.nn.sigmoid(z))).astype(BF16)
    y_ref[...] = x_ref[...] + _dot(gated, w_ref[...])


def _outproj(o2, z2, x2, w_bf, ow, full_norm, tm):
    m, d = x2.shape
    row = pl.BlockSpec((tm, d), lambda i: (i, 0))
    return pl.pallas_call(
        functools.partial(_outproj_kernel, full_norm),
        grid=(m // tm,),
        in_specs=[row, row, row, _resident((d, d)), _resident((1, d))],
        out_specs=row,
        out_shape=jax.ShapeDtypeStruct((m, d), F32),
        compiler_params=_params("parallel"),
        name="outproj",
    )(o2, z2, x2, w_bf, ow)


def _range_matrices(c):
    t = jnp.arange(c)[:, None]
    s = jnp.arange(c)[None, :]
    blocks = [s <= t]
    for w in REC_MATMUL_LEVELS:
        ref = (t // (2 * w)) * (2 * w) + w - 1
        upper = (t & w) != 0
        blocks.append(jnp.where(upper, (s > ref) & (s <= t), (s > t) & (s <= ref)))
    e = jnp.concatenate(blocks, axis=0).astype(BF16)
    return jnp.concatenate([e, e], axis=1)


def _hgrn_rec_kernel(layer_j, has_s0, q_ref, fx_ref, v_ref, lbl_ref, emat_ref, *rest):
    if has_s0:
        s0_ref, o_ref, sfin_ref, st_ref = rest
    else:
        o_ref, sfin_ref, st_ref = rest
    c_idx = pl.program_id(1)
    c = REC_CHUNK
    n_streams = q_ref.shape[0]
    n_chunks = q_ref.shape[1] // c
    d = q_ref.shape[2]

    @pl.when(c_idx == 0)
    def _():
        for si in range(n_streams):
            for h in range(HG_HEADS):
                if has_s0:
                    st_ref[si, h] = s0_ref[si, h].T
                else:
                    st_ref[si, h] = jnp.zeros((HG_DK, HG_DK), F32)

    lg = lbl_ref[...]
    ex = jnp.exp(lg - jnp.max(lg, axis=0, keepdims=True))
    p = ex / jnp.sum(ex, axis=0, keepdims=True)
    cs = p[0:1]
    for r in range(1, layer_j + 1):
        cs = cs + p[r:r + 1]
    lb = cs - p[0:1]

    group_lanes = MXU_COLS
    heads_per_group = group_lanes // HG_DK
    row = lax.broadcasted_iota(jnp.int32, (c, group_lanes), 0)
    tt = lax.broadcasted_iota(jnp.int32, (c, c), 0)
    ss = lax.broadcasted_iota(jnp.int32, (c, c), 1)
    diag_mask = tt == ss
    level_masks = [((tt // (2 * w)) == (ss // (2 * w))) & ((tt & w) != 0) & ((ss & w) == 0)
                   for w in REC_MATMUL_LEVELS + REC_ROW_LEVELS]
    heads = [slice(h * HG_DK, (h + 1) * HG_DK) for h in range(HG_HEADS)]

    def operands(si, rows_c, lanes):
        lb_c = lb[:, lanes]
        f = lb_c + (1.0 - lb_c) * jax.nn.sigmoid(fx_ref[si, rows_c, lanes])
        g = jnp.log2(f)
        kk = 1.0 - f
        g_hi = pltpu.bitcast(pltpu.bitcast(g, jnp.uint32) & jnp.uint32(0xFFFF0000), F32)
        g_pieces = jnp.concatenate([g_hi.astype(BF16), (g - g_hi).astype(BF16)], axis=0)
        ranges = _dot(emat_ref[...], g_pieces)
        b = ranges[0:c]

        q = q_ref[si, rows_c, lanes].astype(F32)
        q_state = (q * jnp.exp2(b)).astype(BF16)
        k_state = (kk * jnp.exp2(b[c - 1:c] - b)).astype(BF16)
        state_decay = jnp.exp2(b[c - 1:c])

        level_ops = []
        for li, w in enumerate(REC_MATMUL_LEVELS):
            x = jnp.exp2(ranges[(li + 1) * c:(li + 2) * c])
            level_ops.append((jnp.where((row & w) != 0, q, kk) * x).astype(BF16))
        for w in REC_ROW_LEVELS:
            pieces = []
            for blk in range(c // w):
                rows = slice(blk * w, (blk + 1) * w)
                mid = (blk // 2) * 2 * w + w - 1
                if blk % 2:
                    pieces.append(q[rows] * jnp.exp2(b[rows] - b[mid:mid + 1]))
                else:
                    pieces.append(kk[rows] * jnp.exp2(b[mid:mid + 1] - b[rows]))
            level_ops.append(jnp.concatenate(pieces, axis=0).astype(BF16))
        return q.astype(BF16), kk.astype(BF16), level_ops, q_state, k_state, state_decay

    def recur(si, rows_c, group_ops):
        per_head = []
        for ops in group_ops:
            per_head += [(ops, slice(k * HG_DK, (k + 1) * HG_DK)) for k in range(heads_per_group)]
        att = []
        for (q_bf, k_bf, level_ops, _, _, _), sl in per_head:
            a = jnp.where(diag_mask, _dot_nt(q_bf[:, sl], k_bf[:, sl]), 0.0)
            for mask, y in zip(level_masks, level_ops):
                a = jnp.where(mask, _dot_nt(y[:, sl], y[:, sl]), a)
            att.append(a.astype(BF16))
        for h, ((_, _, _, q_state, _, _), sl) in enumerate(per_head):
            o = (_dot(att[h], v_ref[si, rows_c, heads[h]])
                 + _dot_nt(q_state[:, sl], st_ref[si, h].astype(BF16)))
            o_ref[si, rows_c, heads[h]] = o.astype(o_ref.dtype)
        for h, ((_, _, _, _, k_state, state_decay), sl) in enumerate(per_head):
            st_ref[si, h] = (st_ref[si, h] * state_decay[:, sl]
                             + _dot_tn(v_ref[si, rows_c, heads[h]], k_state[:, sl]))

    def chunk_operands(si, rows_c):
        return [operands(si, rows_c, slice(g * group_lanes, (g + 1) * group_lanes))
                for g in range(d // group_lanes)]

    chunk_rows = [slice(n * c, (n + 1) * c) for n in range(n_chunks)]
    ops = [chunk_operands(si, chunk_rows[0]) for si in range(n_streams)]
    for n in range(n_chunks):
        ops_next = []
        if n + 1 < n_chunks:
            ops_next = [chunk_operands(si, chunk_rows[n + 1]) for si in range(n_streams)]
        for si in range(n_streams):
            recur(si, chunk_rows[n], ops[si])
        ops = ops_next

    @pl.when(c_idx == pl.num_programs(1) - 1)
    def _():
        for si in range(n_streams):
            for h in range(HG_HEADS):
                sfin_ref[si, h] = st_ref[si, h].T


def _hgrn_rec(q, fx, v, lb_logits, s0, layer_j):
    b, t, d = q.shape
    c = min(t, REC_CHUNK * REC_CHUNKS_PER_STEP)
    ns = REC_STREAMS_PER_STEP
    blk = pl.BlockSpec((ns, c, d), lambda i, j: (i, j, 0))
    st_blk = pl.BlockSpec((ns, HG_HEADS, HG_DK, HG_DK), lambda i, j: (i, 0, 0, 0))
    emat = _range_matrices(REC_CHUNK)
    in_specs = [blk, blk, blk, _resident(lb_logits.shape), _resident(emat.shape)]
    args = [q, fx, v, lb_logits, emat]
    if s0 is not None:
        in_specs.append(st_blk)
        args.append(s0)
    return pl.pallas_call(
        functools.partial(_hgrn_rec_kernel, layer_j, s0 is not None),
        grid=(b // ns, t // c),
        in_specs=in_specs,
        out_specs=[blk, st_blk],
        out_shape=[jax.ShapeDtypeStruct((b, t, d), BF16),
                   jax.ShapeDtypeStruct((b, HG_HEADS, HG_DK, HG_DK), F32)],
        scratch_shapes=[pltpu.VMEM((ns, HG_HEADS, HG_DK, HG_DK), F32)],
        compiler_params=_params("parallel", "arbitrary"),
        name="hgrn_rec",
    )(*args)


def _masked_queries(q_refs):
    tq = q_refs[0].shape[1]
    lane = lax.broadcasted_iota(jnp.int32, (tq, LANES), 1)
    zero = jnp.zeros((tq, LANES), BF16)
    return [[jnp.where(lane < DA_HEAD_DIM, qr[0], zero), jnp.where(lane >= DA_HEAD_DIM, qr[0], zero)]
            for qr in q_refs]


def _lambda_scalar(lam_ref, lam_init):
    lv = lam_ref[...]
    return (jnp.exp(jnp.sum(lv[0:1] * lv[1:2], axis=1, keepdims=True))
            - jnp.exp(jnp.sum(lv[2:3] * lv[3:4], axis=1, keepdims=True)) + lam_init)


def _finish_transposed(lam_ref, subw_ref, lam_init, l_sc, acc_sc, o_ref):
    lam = _lambda_scalar(lam_ref, lam_init)
    for r in range(2):
        o = acc_sc[2 * r] / l_sc[2 * r] - lam * (acc_sc[2 * r + 1] / l_sc[2 * r + 1])
        ms = jnp.mean(o * o, axis=0, keepdims=True)
        o = o * lax.rsqrt(ms + EPS) * subw_ref[...] * (1.0 - lam_init)
        o_ref[0, :, r * LANES:(r + 1) * LANES] = o.T.astype(o_ref.dtype)


def _attn_rows_kernel(lam_init, n_q, slopes_ref, shift_ref, lam_ref, subw_ref, q0_ref, q1_ref,
                      k0_ref, k1_ref, v_ref, o_ref, bias_sc):
    j = pl.program_id(0)
    bi = pl.program_id(1)
    i = pl.program_id(2)
    t = q0_ref.shape[1]

    @pl.when((bi == 0) & (i == 0))
    def _():
        off = lax.broadcasted_iota(jnp.int32, (n_q * t, t), 0) - (n_q - 1) * t
        col = lax.broadcasted_iota(jnp.int32, (n_q * t, t), 1)
        dist = jnp.abs(col - off).astype(F32)
        visible = (off // MASK_CHUNK) <= (col // MASK_CHUNK)
        for r in range(2):
            bias_sc[:, r * t:(r + 1) * t] = jnp.where(
                visible, -slopes_ref[2 * j + r] * dist - shift_ref[0], NEG_INF)

    qm = _masked_queries((q0_ref, q1_ref))
    q_pair = [jnp.concatenate(qm[mp], axis=0) for mp in range(2)]
    k_refs = (k0_ref, k1_ref)
    lam = _lambda_scalar(lam_ref, lam_init)

    def attend(tile_idx):
        rows = (tile_idx + 1) * t
        bias_row0 = (n_q - 1 - tile_idx) * t
        s = [_dot_nt(k_refs[mp][0, 0:rows, :], q_pair[mp]) for mp in range(2)]
        for r in range(2):
            cols = slice(r * t, (r + 1) * t)
            bias = bias_sc[bias_row0:bias_row0 + rows, cols]
            p = [jnp.exp2(s[mp][:, cols] + bias) for mp in range(2)]
            inv0 = 1.0 / jnp.sum(p[0], axis=0, keepdims=True)
            inv1 = lam / jnp.sum(p[1], axis=0, keepdims=True)
            num = _dot_tn(v_ref[0, 0:rows, r * LANES:(r + 1) * LANES],
                          jnp.concatenate([pm.astype(BF16) for pm in p], axis=1))
            o = num[:, :t] * inv0 - num[:, t:] * inv1
            ms = jnp.mean(o * o, axis=0, keepdims=True)
            o = o * lax.rsqrt(ms + EPS) * subw_ref[...] * (1.0 - lam_init)
            o_ref[0, :, r * LANES:(r + 1) * LANES] = o.T.astype(o_ref.dtype)

    for tile_idx in range(n_q):
        pl.when(i == tile_idx)(functools.partial(attend, tile_idx))


def _attn_online_kernel(lam_init, slopes_ref, lam_ref, subw_ref, q0_ref, q1_ref, k0_ref, k1_ref,
                        v_ref, o_ref, m_sc, l_sc, acc_sc):
    j = pl.program_id(1)
    i = pl.program_id(2)
    t = q0_ref.shape[1]
    qm = _masked_queries((q0_ref, q1_ref))
    k_refs = (k0_ref, k1_ref)
    m_sc[...] = jnp.full(m_sc.shape, -jnp.inf, F32)
    l_sc[...] = jnp.zeros(l_sc.shape, F32)
    acc_sc[...] = jnp.zeros(acc_sc.shape, F32)

    def tile(kt_idx, diagonal):
        k_start = pl.multiple_of(kt_idx * t, t)
        row = lax.broadcasted_iota(jnp.int32, (t, t), 0)
        col = lax.broadcasted_iota(jnp.int32, (t, t), 1)
        dist = jnp.abs((i - kt_idx) * t + (col - row)).astype(F32)
        if diagonal:
            visible = (row // MASK_CHUNK) <= (col // MASK_CHUNK)
        for r in range(2):
            bias = -slopes_ref[2 * j + r] * dist
            vt = v_ref[0, pl.ds(k_start, t), r * LANES:(r + 1) * LANES]
            for mp in range(2):
                idx = 2 * r + mp
                kt = k_refs[mp][0, pl.ds(k_start, t), :]
                s = _dot_nt(kt, qm[mp][r]) + bias
                if diagonal:
                    s = jnp.where(visible, s, NEG_INF)
                m_prev = m_sc[idx]
                m_new = jnp.maximum(m_prev, jnp.max(s, axis=0, keepdims=True))
                alpha = jnp.exp2(m_prev - m_new)
                p = jnp.exp2(s - m_new)
                l_sc[idx] = alpha * l_sc[idx] + jnp.sum(p, axis=0, keepdims=True)
                acc_sc[idx] = alpha * acc_sc[idx] + _dot_tn(vt, p.astype(BF16))
                m_sc[idx] = m_new

    def below(kt_idx, carry):
        tile(kt_idx, False)
        return carry

    lax.fori_loop(0, i, below, 0)
    tile(i, True)
    _finish_transposed(lam_ref, subw_ref, lam_init, l_sc, acc_sc, o_ref)


def _prompt_attention(q, k, v, lam_vec, subln_w, slopes2, score_bound2, lam_init):
    b, t, d = q.shape
    tile = ATTN_TILE
    half = d // (2 * LANES)
    q_spec = lambda off: pl.BlockSpec((1, tile, LANES), lambda bi, j, i: (bi, i, j + off))
    k_spec = lambda off: pl.BlockSpec((1, t, LANES), lambda bi, j, i: (bi, 0, j + off))
    v_spec = pl.BlockSpec((1, t, 2 * LANES), lambda bi, j, i: (bi, 0, j))
    o_spec = pl.BlockSpec((1, tile, 2 * LANES), lambda bi, j, i: (bi, i, j))
    smem = pl.BlockSpec(memory_space=pltpu.SMEM)
    lam_spec = pl.BlockSpec(lam_vec.shape, lambda bi, j, i: (0, 0))
    subw_spec = pl.BlockSpec((2 * DA_HEAD_DIM, 1), lambda bi, j, i: (0, 0))
    data_specs = [q_spec(0), q_spec(half), k_spec(0), k_spec(half), v_spec]
    subw = subln_w.reshape(2 * DA_HEAD_DIM, 1)
    stat = pltpu.VMEM((4, 1, tile), F32)
    acc = pltpu.VMEM((4, 2 * DA_HEAD_DIM, tile), F32)
    common = dict(grid=(b, half, t // tile), out_specs=o_spec,
                  out_shape=jax.ShapeDtypeStruct((b, t, d), BF16),
                  compiler_params=_params("parallel", "parallel", "arbitrary"))

    def fixed(shift):
        reorder = lambda spec: pl.BlockSpec(spec.block_shape,
                                            lambda j, bi, i, f=spec.index_map: f(bi, j, i))
        return pl.pallas_call(
            functools.partial(_attn_rows_kernel, lam_init, t // tile),
            grid=(half, b, t // tile),
            in_specs=[smem, smem, reorder(lam_spec), reorder(subw_spec)]
            + [reorder(spec) for spec in data_specs],
            out_specs=reorder(o_spec),
            out_shape=jax.ShapeDtypeStruct((b, t, d), BF16),
            scratch_shapes=[pltpu.VMEM((t, 2 * tile), F32)],
            compiler_params=_params("arbitrary", "arbitrary", "arbitrary"),
            name="diff_attn_fixed",
        )(slopes2, shift, lam_vec, subw, q, q, k, k, v)

    def online(_):
        return pl.pallas_call(
            functools.partial(_attn_online_kernel, lam_init),
            in_specs=[smem, lam_spec, subw_spec] + data_specs,
            scratch_shapes=[stat, stat, acc],
            name="diff_attn_online", **common,
        )(slopes2, lam_vec, subw, q, q, k, k, v)

    return lax.cond(score_bound2[0] <= FIXED_SHIFT_MAX_SCORE * LOG2E, fixed, online, score_bound2)


def _sample_attn_kernel(lam_init, past_visible, slopes_ref, lam_ref, subw_ref, q_ref, kc_ref,
                        kn_ref, vc_ref, vn_ref, o_ref):
    tq = q_ref.shape[1]
    past = kc_ref.shape[3]
    half = DA_HEADS // 2
    lam = _lambda_scalar(lam_ref, lam_init)
    lane = lax.broadcasted_iota(jnp.int32, (tq, LANES), 1)
    zero = jnp.zeros((tq, LANES), BF16)

    row_c = lax.broadcasted_iota(jnp.int32, (tq, past), 0)
    col_c = lax.broadcasted_iota(jnp.int32, (tq, past), 1)
    dist_c = (past + row_c - col_c).astype(F32)
    row_n = lax.broadcasted_iota(jnp.int32, (tq, tq), 0)
    col_n = lax.broadcasted_iota(jnp.int32, (tq, tq), 1)
    dist_n = jnp.abs(row_n - col_n).astype(F32)
    if not past_visible:
        vis_c = (col_c // MASK_CHUNK) <= ((past + row_c) // MASK_CHUNK)
        vis_n = ((past + col_n) // MASK_CHUNK) <= ((past + row_n) // MASK_CHUNK)

    for pair in range(half):
        tiles = [slice((mp * half + pair) * LANES, (mp * half + pair + 1) * LANES) for mp in range(2)]
        kc = [kc_ref[mp, 2 * pair:2 * pair + 2].reshape(LANES, past).astype(BF16) for mp in range(2)]
        for r in range(2):
            head = 2 * pair + r
            slope = slopes_ref[head]
            keep = (lane >= DA_HEAD_DIM) if r else (lane < DA_HEAD_DIM)
            vc = vc_ref[pl.ds(head, past, stride=DA_HEADS), :].astype(BF16)
            vn = vn_ref[0, :, head * LANES:(head + 1) * LANES]
            outs = []
            for mp in range(2):
                qm = jnp.where(keep, q_ref[0, :, tiles[mp]], zero)
                s_c = _dot(qm, kc[mp]) - slope * dist_c
                s_n = _dot_nt(qm, kn_ref[0, :, tiles[mp]]) - slope * dist_n
                if not past_visible:
                    s_c = jnp.where(vis_c, s_c, NEG_INF)
                    s_n = jnp.where(vis_n, s_n, NEG_INF)
                m = jnp.maximum(jnp.max(s_c, axis=1, keepdims=True),
                                jnp.max(s_n, axis=1, keepdims=True))
                p_c = jnp.exp2(s_c - m)
                p_n = jnp.exp2(s_n - m)
                l = jnp.sum(p_c, axis=1, keepdims=True) + jnp.sum(p_n, axis=1, keepdims=True)
                acc = _dot(p_c.astype(BF16), vc) + _dot(p_n.astype(BF16), vn)
                outs.append(acc / l)
            o = outs[0] - lam * outs[1]
            ms = jnp.mean(o * o, axis=1, keepdims=True)
            o = o * lax.rsqrt(ms + EPS) * subw_ref[...] * (1.0 - lam_init)
            o_ref[0, :, head * LANES:(head + 1) * LANES] = o.astype(o_ref.dtype)


def _sample_attention(q, k_new, v_new, k_cache, v_cache, layer_j, lam_vec, subln_w, slopes2,
                      lam_init):
    b, t_q, d = q.shape
    past = k_cache.shape[5]
    past_visible = past % MASK_CHUNK == 0 and t_q <= MASK_CHUNK
    row = pl.BlockSpec((1, t_q, d), lambda bi: (bi, 0, 0))
    return pl.pallas_call(
        functools.partial(_sample_attn_kernel, lam_init, past_visible),
        grid=(b,),
        in_specs=[pl.BlockSpec(memory_space=pltpu.SMEM),
                  pl.BlockSpec(lam_vec.shape, lambda bi: (0, 0)),
                  pl.BlockSpec((1, 2 * DA_HEAD_DIM), lambda bi: (0, 0)),
                  row,
                  pl.BlockSpec((None, None, 2, DA_HEADS, DA_HEAD_DIM, past),
                               lambda bi: (layer_j, bi, 0, 0, 0, 0)),
                  row,
                  pl.BlockSpec((None, None, past * DA_HEADS, 2 * DA_HEAD_DIM),
                               lambda bi: (layer_j, bi, 0, 0)),
                  row],
        out_specs=row,
        out_shape=jax.ShapeDtypeStruct((b, t_q, d), BF16),
        compiler_params=_params("parallel"),
        name="diff_attn_sample",
    )(slopes2, lam_vec, subln_w.reshape(1, 2 * DA_HEAD_DIM), q, k_cache, k_new, v_cache, v_new)


def _row_tile(m):
    return 512 if m % 512 == 0 else 256


def _outproj_tile(m):
    return 1024 if m % 2048 == 0 else _row_tile(m)


def _hgrn_layer(x, s0, nw, w_in_bf, lb_logits, onw, w_out_bf, layer_j):
    b, t, d = x.shape
    x2 = x.reshape(b * t, d)
    q, fx, iv, z = _inproj_hgrn(x2, nw.reshape(1, d), w_in_bf, _outproj_tile(b * t))
    o, s_new = _hgrn_rec(q.reshape(b, t, d), fx.reshape(b, t, d), iv.reshape(b, t, d),
                         lb_logits, s0, layer_j)
    y = _outproj(o.reshape(b * t, d), z, x2, w_out_bf, onw.reshape(1, d), True,
                 _outproj_tile(b * t))
    return y.reshape(b, t, d), s_new


def _attn_layer(x, cache, nw, w_in_bf, qn_w, kn_w, lam_vec, subln_w, w_out_bf, layer_idx,
                layer_j, n_layers, k_stack, v_stack):
    b, t, d = x.shape
    x2 = x.reshape(b * t, d)
    groups = d // DA_HEAD_DIM
    gmat = jnp.kron(jnp.eye(MXU_COLS // DA_HEAD_DIM, dtype=F32),
                    jnp.full((DA_HEAD_DIM, DA_HEAD_DIM), 1.0 / DA_HEAD_DIM, F32)).astype(BF16)
    prompt = cache is None
    tm = 512 if t % 512 == 0 else t
    q, kb, vb, z, k_stack, v_stack = _inproj_attn(
        x2, t, nw.reshape(1, d), w_in_bf, jnp.tile(qn_w, groups).reshape(1, d),
        jnp.tile(kn_w, groups).reshape(1, d), gmat, layer_j, n_layers, k_stack, v_stack, prompt, tm)
    lam_init = 0.8 - 0.6 * math.exp(-0.3 * layer_idx)
    slopes2 = jnp.exp2(-8.0 * jnp.arange(1, DA_HEADS + 1, dtype=F32) / DA_HEADS) * LOG2E
    q, kb, vb = (a.reshape(b, t, d) for a in (q, kb, vb))
    if prompt:
        bound2 = (DA_HEAD_DIM ** 0.5 * LOG2E * 1.02) * jnp.max(jnp.abs(qn_w)) * jnp.max(jnp.abs(kn_w))
        o = _prompt_attention(q, kb, vb, lam_vec, subln_w, slopes2, bound2.reshape(1), lam_init)
    else:
        k_cache, v_cache = cache
        o = _sample_attention(q, kb, vb, k_cache, v_cache, layer_j, lam_vec, subln_w, slopes2,
                              lam_init)
    y = _outproj(o.reshape(b * t, d), z, x2, w_out_bf, jnp.ones((1, d), F32), False,
                 _outproj_tile(b * t))
    return y.reshape(b, t, d), k_stack, v_stack


def kernel(x_prompt, x_sample, cache_k, cache_v, state_hgrn, norm_w, hgrn_w_in, hgrn_lb_logits,
           hgrn_onorm_w, hgrn_w_out, attn_w_in, attn_q_norm, attn_k_norm, attn_lambda, attn_subln,
           attn_w_out):
    depth = norm_w.shape[0]
    n_attn = cache_k.shape[0]
    bp, tp, d = x_prompt.shape
    bs, ts, _ = x_sample.shape
    past = cache_k.shape[2]
    cache_k_t = jnp.transpose(cache_k, (0, 1, 3, 4, 5, 2))
    cache_v2 = cache_v.reshape(n_attn, bs, past * DA_HEADS, 2 * DA_HEAD_DIM)
    yp, ys = x_prompt, x_sample
    kp = vp = ks_ = vs_ = None
    sp, ss = [], []
    for l in range(depth):
        j = l // N_MIXERS
        if l % N_MIXERS == 0:
            w_in = hgrn_w_in[j].astype(BF16)
            w_out = hgrn_w_out[j].astype(BF16)
            yp, s_p = _hgrn_layer(yp, None, norm_w[l], w_in, hgrn_lb_logits, hgrn_onorm_w[j],
                                  w_out, j)
            ys, s_s = _hgrn_layer(ys, state_hgrn[j], norm_w[l], w_in, hgrn_lb_logits,
                                  hgrn_onorm_w[j], w_out, j)
            sp.append(s_p)
            ss.append(s_s)
        else:
            w_in = attn_w_in[j].astype(BF16)
            w_out = attn_w_out[j].astype(BF16)
            yp, kp, vp = _attn_layer(yp, None, norm_w[l], w_in, attn_q_norm[j], attn_k_norm[j],
                                     attn_lambda[j], attn_subln[j], w_out, l, j, n_attn, kp, vp)
            ys, ks_, vs_ = _attn_layer(ys, (cache_k_t, cache_v2), norm_w[l], w_in,
                                       attn_q_norm[j], attn_k_norm[j], attn_lambda[j],
                                       attn_subln[j], w_out, l, j, n_attn, ks_, vs_)
    new_k_prompt = jnp.transpose(kp.reshape(n_attn, bp, 2, DA_HEADS, DA_HEAD_DIM, tp),
                                 (0, 1, 5, 2, 3, 4))
    return (yp, ys, new_k_prompt,
            vp.reshape(n_attn, bp, tp, DA_HEADS, 2 * DA_HEAD_DIM),
            ks_.reshape(n_attn, bs, ts, 2, DA_HEADS, DA_HEAD_DIM),
            vs_.reshape(n_attn, bs, ts, DA_HEADS, 2 * DA_HEAD_DIM),
            jnp.stack(sp), jnp.stack(ss))
```

```python
import functools
import math

import jax
import jax.numpy as jnp
from jax import lax
from jax.experimental import pallas as pl
from jax.experimental.pallas import tpu as pltpu

F32 = jnp.float32
BF16 = jnp.bfloat16

EPS = 1e-6
NEG_INF = -1e30
LOG2E = 1.4426950408889634
LANES = 128
MXU_COLS = 256
HG_HEADS = 8
HG_DK = 128
DA_HEADS = 8
DA_HEAD_DIM = 64
MASK_CHUNK = 64
N_MIXERS = 2

REC_CHUNK = 64
REC_CHUNKS_PER_STEP = 16
REC_MATMUL_LEVELS = (2, 4)
REC_ROW_LEVELS = (8, 16, 32)
ATTN_TILE = 512
FIXED_SHIFT_MAX_SCORE = 40.0
VMEM_LIMIT = 56 * 1024 * 1024


def _dot(a, b):
    return jnp.dot(a, b, preferred_element_type=F32)


def _dot_nt(a, b):
    return lax.dot_general(a, b, (((1,), (1,)), ((), ())), preferred_element_type=F32)


def _dot_tn(a, b):
    return lax.dot_general(a, b, (((0,), (0,)), ((), ())), preferred_element_type=F32)


def _rms_rows(x, w):
    ms = jnp.mean(x * x, axis=-1, keepdims=True)
    return x * lax.rsqrt(ms + EPS) * w


def _resident(shape):
    nd = len(shape)
    return pl.BlockSpec(shape, lambda *_: (0,) * nd, pipeline_mode=pl.Buffered(1))


def _params(*semantics):
    return pltpu.CompilerParams(dimension_semantics=semantics, vmem_limit_bytes=VMEM_LIMIT)


def _inproj_hgrn_kernel(x_ref, nw_ref, w_ref, q_ref, fx_ref, i_ref, z_ref):
    d = x_ref.shape[1]
    h = _rms_rows(x_ref[...], nw_ref[...]).astype(BF16)
    q_ref[...] = (_dot(h, w_ref[:, 0:d]) * (HG_DK ** -0.5)).astype(q_ref.dtype)
    for c, o_ref in ((1, fx_ref), (2, i_ref), (3, z_ref)):
        o_ref[...] = _dot(h, w_ref[:, c * d:(c + 1) * d]).astype(o_ref.dtype)


def _inproj_hgrn(x2, nw, w_bf, tm):
    m, d = x2.shape
    row = pl.BlockSpec((tm, d), lambda i: (i, 0))
    return pl.pallas_call(
        _inproj_hgrn_kernel,
        grid=(m // tm,),
        in_specs=[row, _resident((1, d)), _resident((d, 4 * d))],
        out_specs=[row, row, row, row],
        out_shape=[jax.ShapeDtypeStruct((m, d), BF16), jax.ShapeDtypeStruct((m, d), F32),
                   jax.ShapeDtypeStruct((m, d), BF16), jax.ShapeDtypeStruct((m, d), BF16)],
        compiler_params=_params("parallel"),
        name="inproj_hgrn",
    )(x2, nw, w_bf)


def _inproj_attn_kernel(k_transposed, n_alias, x_ref, nw_ref, w_ref, qnw_ref, knw_ref, g_ref, *rest):
    q_ref, kb_ref, vb_ref, z_ref, ks_ref, vs_ref = rest[n_alias:]
    d = x_ref.shape[1]
    h = _rms_rows(x_ref[...], nw_ref[...]).astype(BF16)
    gmat = g_ref[...]
    gw = gmat.shape[0]

    def group_norm(y, w):
        cols = []
        for t in range(d // gw):
            yt = y[:, t * gw:(t + 1) * gw]
            ms = _dot((yt * yt).astype(BF16), gmat)
            cols.append(yt * lax.rsqrt(ms + EPS))
        return jnp.concatenate(cols, axis=1) * w

    q = group_norm(_dot(h, w_ref[:, 0:d]), qnw_ref[...])
    q_ref[...] = (q * (DA_HEAD_DIM ** -0.5 * LOG2E)).astype(BF16)
    k = group_norm(_dot(h, w_ref[:, d:2 * d]), knw_ref[...])
    kb_ref[...] = k.astype(BF16)
    ks_ref[0, 0] = k.T if k_transposed else k
    v = _dot(h, w_ref[:, 2 * d:3 * d])
    for hd in range(DA_HEADS):
        vs_ref[0, pl.ds(hd, x_ref.shape[0], stride=DA_HEADS), :] = v[:, hd * LANES:(hd + 1) * LANES]
    vb_ref[...] = v.astype(BF16)
    z_ref[...] = _dot(h, w_ref[:, 3 * d:4 * d]).astype(BF16)


def _inproj_attn(x2, seq, nw, w_bf, qnw, knw, gmat, layer_j, n_layers, k_stack, v_stack,
                 k_transposed, tm):
    m, d = x2.shape
    batch = m // seq
    tiles_per_seq = seq // tm
    row = pl.BlockSpec((tm, d), lambda i: (i, 0))
    if k_transposed:
        ks_shape = (n_layers, batch, d, seq)
        ks_spec = pl.BlockSpec((1, 1, d, tm),
                               lambda i: (layer_j, i // tiles_per_seq, 0, i % tiles_per_seq))
    else:
        ks_shape = (n_layers, batch, seq, d)
        ks_spec = pl.BlockSpec((1, 1, tm, d),
                               lambda i: (layer_j, i // tiles_per_seq, i % tiles_per_seq, 0))
    vs_spec = pl.BlockSpec((1, tm * DA_HEADS, 2 * DA_HEAD_DIM), lambda i: (layer_j, i, 0))
    in_specs = [row, _resident((1, d)), _resident((d, 4 * d)), _resident((1, d)),
                _resident((1, d)), _resident(gmat.shape)]
    args = [x2, nw, w_bf, qnw, knw, gmat]
    aliases = {}
    if k_stack is not None:
        in_specs += [pl.BlockSpec(memory_space=pl.ANY)] * 2
        aliases = {len(args): 4, len(args) + 1: 5}
        args += [k_stack, v_stack]
    shp = lambda dt: jax.ShapeDtypeStruct((m, d), dt)
    return pl.pallas_call(
        functools.partial(_inproj_attn_kernel, k_transposed, len(aliases)),
        grid=(m // tm,),
        in_specs=in_specs,
        out_specs=[row, row, row, row, ks_spec, vs_spec],
        out_shape=[shp(BF16), shp(BF16), shp(BF16), shp(BF16),
                   jax.ShapeDtypeStruct(ks_shape, F32),
                   jax.ShapeDtypeStruct((n_layers, m * DA_HEADS, 2 * DA_HEAD_DIM), F32)],
        input_output_aliases=aliases,
        compiler_params=_params("parallel"),
        name="inproj_attn",
    )(*args)


def _outproj_kernel(full_norm, o_ref, z_ref, x_ref, w_ref, ow_ref, y_ref):
    o = o_ref[...].astype(F32)
    if full_norm:
        o = _rms_rows(o, ow_ref[...])
    z = z_ref[...].astype(F32)
    gated = (o * (z * jax.nn.sigmoid(z))).astype(BF16)
    y_ref[...] = x_ref[...] + _dot(gated, w_ref[...])


def _outproj(o2, z2, x2, w_bf, ow, full_norm, tm):
    m, d = x2.shape
    row = pl.BlockSpec((tm, d), lambda i: (i, 0))
    return pl.pallas_call(
        functools.partial(_outproj_kernel, full_norm),
        grid=(m // tm,),
        in_specs=[row, row, row, _resident((d, d)), _resident((1, d))],
        out_specs=row,
        out_shape=jax.ShapeDtypeStruct((m, d), F32),
        compiler_params=_params("parallel"),
        name="outproj",
    )(o2, z2, x2, w_bf, ow)


def _range_matrices(c):
    t = jnp.arange(c)[:, None]
    s = jnp.arange(c)[None, :]
    blocks = [s <= t]
    for w in REC_MATMUL_LEVELS:
        ref = (t // (2 * w)) * (2 * w) + w - 1
        upper = (t & w) != 0
        blocks.append(jnp.where(upper, (s > ref) & (s <= t), (s > t) & (s <= ref)))
    e = jnp.concatenate(blocks, axis=0).astype(BF16)
    return jnp.concatenate([e, e], axis=1)


def _hgrn_rec_kernel(layer_j, has_s0, q_ref, fx_ref, v_ref, lbl_ref, emat_ref, *rest):
    if has_s0:
        s0_ref, o_ref, sfin_ref, st_ref = rest
    else:
        o_ref, sfin_ref, st_ref = rest
    c_idx = pl.program_id(1)
    c = REC_CHUNK
    n_chunks = q_ref.shape[1] // c
    d = q_ref.shape[2]

    @pl.when(c_idx == 0)
    def _():
        for h in range(HG_HEADS):
            if has_s0:
                st_ref[h] = s0_ref[0, h].T
            else:
                st_ref[h] = jnp.zeros((HG_DK, HG_DK), F32)

    lg = lbl_ref[...]
    ex = jnp.exp(lg - jnp.max(lg, axis=0, keepdims=True))
    p = ex / jnp.sum(ex, axis=0, keepdims=True)
    cs = p[0:1]
    for r in range(1, layer_j + 1):
        cs = cs + p[r:r + 1]
    lb = cs - p[0:1]

    row = lax.broadcasted_iota(jnp.int32, (c, d), 0)
    tt = lax.broadcasted_iota(jnp.int32, (c, c), 0)
    ss = lax.broadcasted_iota(jnp.int32, (c, c), 1)
    diag_mask = tt == ss
    level_masks = [((tt // (2 * w)) == (ss // (2 * w))) & ((tt & w) != 0) & ((ss & w) == 0)
                   for w in (1,) + REC_MATMUL_LEVELS + REC_ROW_LEVELS]
    heads = [slice(h * HG_DK, (h + 1) * HG_DK) for h in range(HG_HEADS)]

    def operands(rows_c):
        f = lb + (1.0 - lb) * jax.nn.sigmoid(fx_ref[0, rows_c, :])
        g = jnp.log2(f)
        kk = 1.0 - f
        g_hi = pltpu.bitcast(pltpu.bitcast(g, jnp.uint32) & jnp.uint32(0xFFFF0000), F32)
        g_pieces = jnp.concatenate([g_hi.astype(BF16), (g - g_hi).astype(BF16)], axis=0)
        ranges = _dot(emat_ref[...], g_pieces)
        b = ranges[0:c]

        q = q_ref[0, rows_c, :].astype(F32)
        q_state = (q * jnp.exp2(b)).astype(BF16)
        k_state = (kk * jnp.exp2(b[c - 1:c] - b)).astype(BF16)
        state_decay = jnp.exp2(b[c - 1:c])

        level_ops = [jnp.where((row & 1) != 0, q * f, kk).astype(BF16)]
        for li, w in enumerate(REC_MATMUL_LEVELS):
            x = jnp.exp2(ranges[(li + 1) * c:(li + 2) * c])
            level_ops.append((jnp.where((row & w) != 0, q, kk) * x).astype(BF16))
        for w in REC_ROW_LEVELS:
            pieces = []
            for blk in range(c // w):
                rows = slice(blk * w, (blk + 1) * w)
                mid = (blk // 2) * 2 * w + w - 1
                if blk % 2:
                    pieces.append(q[rows] * jnp.exp2(b[rows] - b[mid:mid + 1]))
                else:
                    pieces.append(kk[rows] * jnp.exp2(b[mid:mid + 1] - b[rows]))
            level_ops.append(jnp.concatenate(pieces, axis=0).astype(BF16))
        return q.astype(BF16), kk.astype(BF16), level_ops, q_state, k_state, state_decay

    def recur(rows_c, ops):
        q_bf, k_bf, level_ops, q_state, k_state, state_decay = ops
        att = []
        for sl in heads:
            a = jnp.where(diag_mask, _dot_nt(q_bf[:, sl], k_bf[:, sl]), 0.0)
            for mask, y in zip(level_masks, level_ops):
                a = jnp.where(mask, _dot_nt(y[:, sl], y[:, sl]), a)
            att.append(a.astype(BF16))
        for h, sl in enumerate(heads):
            o = (_dot(att[h], v_ref[0, rows_c, sl])
                 + _dot_nt(q_state[:, sl], st_ref[h].astype(BF16)))
            o_ref[0, rows_c, sl] = o.astype(o_ref.dtype)
        for h, sl in enumerate(heads):
            st_ref[h] = (st_ref[h] * state_decay[:, sl]
                         + _dot_tn(v_ref[0, rows_c, sl], k_state[:, sl]))

    chunk_rows = [slice(n * c, (n + 1) * c) for n in range(n_chunks)]
    ops = operands(chunk_rows[0])
    for n in range(n_chunks):
        ops_next = operands(chunk_rows[n + 1]) if n + 1 < n_chunks else None
        recur(chunk_rows[n], ops)
        ops = ops_next

    @pl.when(c_idx == pl.num_programs(1) - 1)
    def _():
        for h in range(HG_HEADS):
            sfin_ref[0, h] = st_ref[h].T


def _hgrn_rec(q, fx, v, lb_logits, s0, layer_j):
    b, t, d = q.shape
    c = min(t, REC_CHUNK * REC_CHUNKS_PER_STEP)
    blk = pl.BlockSpec((1, c, d), lambda i, j: (i, j, 0))
    st_blk = pl.BlockSpec((1, HG_HEADS, HG_DK, HG_DK), lambda i, j: (i, 0, 0, 0))
    emat = _range_matrices(REC_CHUNK)
    in_specs = [blk, blk, blk, _resident(lb_logits.shape), _resident(emat.shape)]
    args = [q, fx, v, lb_logits, emat]
    if s0 is not None:
        in_specs.append(st_blk)
        args.append(s0)
    return pl.pallas_call(
        functools.partial(_hgrn_rec_kernel, layer_j, s0 is not None),
        grid=(b, t // c),
        in_specs=in_specs,
        out_specs=[blk, st_blk],
        out_shape=[jax.ShapeDtypeStruct((b, t, d), BF16),
                   jax.ShapeDtypeStruct((b, HG_HEADS, HG_DK, HG_DK), F32)],
        scratch_shapes=[pltpu.VMEM((HG_HEADS, HG_DK, HG_DK), F32)],
        compiler_params=_params("parallel", "arbitrary"),
        name="hgrn_rec",
    )(*args)


def _masked_queries(q_refs):
    tq = q_refs[0].shape[1]
    lane = lax.broadcasted_iota(jnp.int32, (tq, LANES), 1)
    zero = jnp.zeros((tq, LANES), BF16)
    return [[jnp.where(lane < DA_HEAD_DIM, qr[0], zero), jnp.where(lane >= DA_HEAD_DIM, qr[0], zero)]
            for qr in q_refs]


def _lambda_scalar(lam_ref, lam_init):
    lv = lam_ref[...]
    return (jnp.exp(jnp.sum(lv[0:1] * lv[1:2], axis=1, keepdims=True))
            - jnp.exp(jnp.sum(lv[2:3] * lv[3:4], axis=1, keepdims=True)) + lam_init)


def _finish_transposed(lam_ref, subw_ref, lam_init, l_sc, acc_sc, o_ref):
    lam = _lambda_scalar(lam_ref, lam_init)
    for r in range(2):
        o = acc_sc[2 * r] / l_sc[2 * r] - lam * (acc_sc[2 * r + 1] / l_sc[2 * r + 1])
        ms = jnp.mean(o * o, axis=0, keepdims=True)
        o = o * lax.rsqrt(ms + EPS) * subw_ref[...] * (1.0 - lam_init)
        o_ref[0, :, r * LANES:(r + 1) * LANES] = o.T.astype(o_ref.dtype)


def _attn_rows_kernel(lam_init, n_q, slopes_ref, shift_ref, lam_ref, subw_ref, q0_ref, q1_ref,
                      k0_ref, k1_ref, v_ref, o_ref, bias_sc):
    j = pl.program_id(0)
    bi = pl.program_id(1)
    i = pl.program_id(2)
    t = q0_ref.shape[1]

    @pl.when((bi == 0) & (i == 0))
    def _():
        off = lax.broadcasted_iota(jnp.int32, (n_q * t, t), 0) - (n_q - 1) * t
        col = lax.broadcasted_iota(jnp.int32, (n_q * t, t), 1)
        dist = jnp.abs(col - off).astype(F32)
        visible = (off // MASK_CHUNK) <= (col // MASK_CHUNK)
        for r in range(2):
            bias_sc[:, r * t:(r + 1) * t] = jnp.where(
                visible, -slopes_ref[2 * j + r] * dist - shift_ref[0], NEG_INF)

    qm = _masked_queries((q0_ref, q1_ref))
    q_pair = [jnp.concatenate(qm[mp], axis=0) for mp in range(2)]
    k_refs = (k0_ref, k1_ref)
    lam = _lambda_scalar(lam_ref, lam_init)

    def attend(tile_idx):
        rows = (tile_idx + 1) * t
        bias_row0 = (n_q - 1 - tile_idx) * t
        s = [_dot_nt(k_refs[mp][0, 0:rows, :], q_pair[mp]) for mp in range(2)]
        for r in range(2):
            cols = slice(r * t, (r + 1) * t)
            bias = bias_sc[bias_row0:bias_row0 + rows, cols]
            p = [jnp.exp2(s[mp][:, cols] + bias) for mp in range(2)]
            inv0 = 1.0 / jnp.sum(p[0], axis=0, keepdims=True)
            inv1 = lam / jnp.sum(p[1], axis=0, keepdims=True)
            num = _dot_tn(v_ref[0, 0:rows, r * LANES:(r + 1) * LANES],
                          jnp.concatenate([pm.astype(BF16) for pm in p], axis=1))
            o = num[:, :t] * inv0 - num[:, t:] * inv1
            ms = jnp.mean(o * o, axis=0, keepdims=True)
            o = o * lax.rsqrt(ms + EPS) * subw_ref[...] * (1.0 - lam_init)
            o_ref[0, :, r * LANES:(r + 1) * LANES] = o.T.astype(o_ref.dtype)

    for tile_idx in range(n_q):
        pl.when(i == tile_idx)(functools.partial(attend, tile_idx))


def _attn_online_kernel(lam_init, slopes_ref, lam_ref, subw_ref, q0_ref, q1_ref, k0_ref, k1_ref,
                        v_ref, o_ref, m_sc, l_sc, acc_sc):
    j = pl.program_id(1)
    i = pl.program_id(2)
    t = q0_ref.shape[1]
    qm = _masked_queries((q0_ref, q1_ref))
    k_refs = (k0_ref, k1_ref)
    m_sc[...] = jnp.full(m_sc.shape, -jnp.inf, F32)
    l_sc[...] = jnp.zeros(l_sc.shape, F32)
    acc_sc[...] = jnp.zeros(acc_sc.shape, F32)

    def tile(kt_idx, diagonal):
        k_start = pl.multiple_of(kt_idx * t, t)
        row = lax.broadcasted_iota(jnp.int32, (t, t), 0)
        col = lax.broadcasted_iota(jnp.int32, (t, t), 1)
        dist = jnp.abs((i - kt_idx) * t + (col - row)).astype(F32)
        if diagonal:
            visible = (row // MASK_CHUNK) <= (col // MASK_CHUNK)
        for r in range(2):
            bias = -slopes_ref[2 * j + r] * dist
            vt = v_ref[0, pl.ds(k_start, t), r * LANES:(r + 1) * LANES]
            for mp in range(2):
                idx = 2 * r + mp
                kt = k_refs[mp][0, pl.ds(k_start, t), :]
                s = _dot_nt(kt, qm[mp][r]) + bias
                if diagonal:
                    s = jnp.where(visible, s, NEG_INF)
                m_prev = m_sc[idx]
                m_new = jnp.maximum(m_prev, jnp.max(s, axis=0, keepdims=True))
                alpha = jnp.exp2(m_prev - m_new)
                p = jnp.exp2(s - m_new)
                l_sc[idx] = alpha * l_sc[idx] + jnp.sum(p, axis=0, keepdims=True)
                acc_sc[idx] = alpha * acc_sc[idx] + _dot_tn(vt, p.astype(BF16))
                m_sc[idx] = m_new

    def below(kt_idx, carry):
        tile(kt_idx, False)
        return carry

    lax.fori_loop(0, i, below, 0)
    tile(i, True)
    _finish_transposed(lam_ref, subw_ref, lam_init, l_sc, acc_sc, o_ref)


def _prompt_attention(q, k, v, lam_vec, subln_w, slopes2, score_bound2, lam_init):
    b, t, d = q.shape
    tile = ATTN_TILE
    half = d // (2 * LANES)
    q_spec = lambda off: pl.BlockSpec((1, tile, LANES), lambda bi, j, i: (bi, i, j + off))
    k_spec = lambda off: pl.BlockSpec((1, t, LANES), lambda bi, j, i: (bi, 0, j + off))
    v_spec = pl.BlockSpec((1, t, 2 * LANES), lambda bi, j, i: (bi, 0, j))
    o_spec = pl.BlockSpec((1, tile, 2 * LANES), lambda bi, j, i: (bi, i, j))
    smem = pl.BlockSpec(memory_space=pltpu.SMEM)
    lam_spec = pl.BlockSpec(lam_vec.shape, lambda bi, j, i: (0, 0))
    subw_spec = pl.BlockSpec((2 * DA_HEAD_DIM, 1), lambda bi, j, i: (0, 0))
    data_specs = [q_spec(0), q_spec(half), k_spec(0), k_spec(half), v_spec]
    subw = subln_w.reshape(2 * DA_HEAD_DIM, 1)
    stat = pltpu.VMEM((4, 1, tile), F32)
    acc = pltpu.VMEM((4, 2 * DA_HEAD_DIM, tile), F32)
    common = dict(grid=(b, half, t // tile), out_specs=o_spec,
                  out_shape=jax.ShapeDtypeStruct((b, t, d), BF16),
                  compiler_params=_params("parallel", "parallel", "arbitrary"))

    def fixed(shift):
        reorder = lambda spec: pl.BlockSpec(spec.block_shape,
                                            lambda j, bi, i, f=spec.index_map: f(bi, j, i))
        return pl.pallas_call(
            functools.partial(_attn_rows_kernel, lam_init, t // tile),
            grid=(half, b, t // tile),
            in_specs=[smem, smem, reorder(lam_spec), reorder(subw_spec)]
            + [reorder(spec) for spec in data_specs],
            out_specs=reorder(o_spec),
            out_shape=jax.ShapeDtypeStruct((b, t, d), BF16),
            scratch_shapes=[pltpu.VMEM((t, 2 * tile), F32)],
            compiler_params=_params("arbitrary", "arbitrary", "arbitrary"),
            name="diff_attn_fixed",
        )(slopes2, shift, lam_vec, subw, q, q, k, k, v)

    def online(_):
        return pl.pallas_call(
            functools.partial(_attn_online_kernel, lam_init),
            in_specs=[smem, lam_spec, subw_spec] + data_specs,
            scratch_shapes=[stat, stat, acc],
            name="diff_attn_online", **common,
        )(slopes2, lam_vec, subw, q, q, k, k, v)

    return lax.cond(score_bound2[0] <= FIXED_SHIFT_MAX_SCORE * LOG2E, fixed, online, score_bound2)


def _sample_attn_kernel(lam_init, past_visible, slopes_ref, lam_ref, subw_ref, q_ref, kc_ref,
                        kn_ref, vc_ref, vn_ref, o_ref):
    tq = q_ref.shape[1]
    past = kc_ref.shape[3]
    half = DA_HEADS // 2
    lam = _lambda_scalar(lam_ref, lam_init)
    lane = lax.broadcasted_iota(jnp.int32, (tq, LANES), 1)
    zero = jnp.zeros((tq, LANES), BF16)

    row_c = lax.broadcasted_iota(jnp.int32, (tq, past), 0)
    col_c = lax.broadcasted_iota(jnp.int32, (tq, past), 1)
    dist_c = (past + row_c - col_c).astype(F32)
    row_n = lax.broadcasted_iota(jnp.int32, (tq, tq), 0)
    col_n = lax.broadcasted_iota(jnp.int32, (tq, tq), 1)
    dist_n = jnp.abs(row_n - col_n).astype(F32)
    if not past_visible:
        vis_c = (col_c // MASK_CHUNK) <= ((past + row_c) // MASK_CHUNK)
        vis_n = ((past + col_n) // MASK_CHUNK) <= ((past + row_n) // MASK_CHUNK)

    for pair in range(half):
        tiles = [slice((mp * half + pair) * LANES, (mp * half + pair + 1) * LANES) for mp in range(2)]
        kc = [kc_ref[mp, 2 * pair:2 * pair + 2].reshape(LANES, past).astype(BF16) for mp in range(2)]
        for r in range(2):
            head = 2 * pair + r
            slope = slopes_ref[head]
            keep = (lane >= DA_HEAD_DIM) if r else (lane < DA_HEAD_DIM)
            vc = vc_ref[pl.ds(head, past, stride=DA_HEADS), :].astype(BF16)
            vn = vn_ref[0, :, head * LANES:(head + 1) * LANES]
            outs = []
            for mp in range(2):
                qm = jnp.where(keep, q_ref[0, :, tiles[mp]], zero)
                s_c = _dot(qm, kc[mp]) - slope * dist_c
                s_n = _dot_nt(qm, kn_ref[0, :, tiles[mp]]) - slope * dist_n
                if not past_visible:
                    s_c = jnp.where(vis_c, s_c, NEG_INF)
                    s_n = jnp.where(vis_n, s_n, NEG_INF)
                m = jnp.maximum(jnp.max(s_c, axis=1, keepdims=True),
                                jnp.max(s_n, axis=1, keepdims=True))
                p_c = jnp.exp2(s_c - m)
                p_n = jnp.exp2(s_n - m)
                l = jnp.sum(p_c, axis=1, keepdims=True) + jnp.sum(p_n, axis=1, keepdims=True)
                acc = _dot(p_c.astype(BF16), vc) + _dot(p_n.astype(BF16), vn)
                outs.append(acc / l)
            o = outs[0] - lam * outs[1]
            ms = jnp.mean(o * o, axis=1, keepdims=True)
            o = o * lax.rsqrt(ms + EPS) * subw_ref[...] * (1.0 - lam_init)
            o_ref[0, :, head * LANES:(head + 1) * LANES] = o.astype(o_ref.dtype)


def _sample_attention(q, k_new, v_new, k_cache, v_cache, layer_j, lam_vec, subln_w, slopes2,
                      lam_init):
    b, t_q, d = q.shape
    past = k_cache.shape[5]
    past_visible = past % MASK_CHUNK == 0 and t_q <= MASK_CHUNK
    row = pl.BlockSpec((1, t_q, d), lambda bi: (bi, 0, 0))
    return pl.pallas_call(
        functools.partial(_sample_attn_kernel, lam_init, past_visible),
        grid=(b,),
        in_specs=[pl.BlockSpec(memory_space=pltpu.SMEM),
                  pl.BlockSpec(lam_vec.shape, lambda bi: (0, 0)),
                  pl.BlockSpec((1, 2 * DA_HEAD_DIM), lambda bi: (0, 0)),
                  row,
                  pl.BlockSpec((None, None, 2, DA_HEADS, DA_HEAD_DIM, past),
                               lambda bi: (layer_j, bi, 0, 0, 0, 0)),
                  row,
                  pl.BlockSpec((None, None, past * DA_HEADS, 2 * DA_HEAD_DIM),
                               lambda bi: (layer_j, bi, 0, 0)),
                  row],
        out_specs=row,
        out_shape=jax.ShapeDtypeStruct((b, t_q, d), BF16),
        compiler_params=_params("parallel"),
        name="diff_attn_sample",
    )(slopes2, lam_vec, subln_w.reshape(1, 2 * DA_HEAD_DIM), q, k_cache, k_new, v_cache, v_new)


def _row_tile(m):
    return 512 if m % 512 == 0 else 256


def _outproj_tile(m):
    return 1024 if m % 2048 == 0 else _row_tile(m)


def _hgrn_layer(x, s0, nw, w_in_bf, lb_logits, onw, w_out_bf, layer_j):
    b, t, d = x.shape
    x2 = x.reshape(b * t, d)
    q, fx, iv, z = _inproj_hgrn(x2, nw.reshape(1, d), w_in_bf, _outproj_tile(b * t))
    o, s_new = _hgrn_rec(q.reshape(b, t, d), fx.reshape(b, t, d), iv.reshape(b, t, d),
                         lb_logits, s0, layer_j)
    y = _outproj(o.reshape(b * t, d), z, x2, w_out_bf, onw.reshape(1, d), True,
                 _outproj_tile(b * t))
    return y.reshape(b, t, d), s_new


def _attn_layer(x, cache, nw, w_in_bf, qn_w, kn_w, lam_vec, subln_w, w_out_bf, layer_idx,
                layer_j, n_layers, k_stack, v_stack):
    b, t, d = x.shape
    x2 = x.reshape(b * t, d)
    groups = d // DA_HEAD_DIM
    gmat = jnp.kron(jnp.eye(MXU_COLS // DA_HEAD_DIM, dtype=F32),
                    jnp.full((DA_HEAD_DIM, DA_HEAD_DIM), 1.0 / DA_HEAD_DIM, F32)).astype(BF16)
    prompt = cache is None
    tm = 512 if t % 512 == 0 else t
    q, kb, vb, z, k_stack, v_stack = _inproj_attn(
        x2, t, nw.reshape(1, d), w_in_bf, jnp.tile(qn_w, groups).reshape(1, d),
        jnp.tile(kn_w, groups).reshape(1, d), gmat, layer_j, n_layers, k_stack, v_stack, prompt, tm)
    lam_init = 0.8 - 0.6 * math.exp(-0.3 * layer_idx)
    slopes2 = jnp.exp2(-8.0 * jnp.arange(1, DA_HEADS + 1, dtype=F32) / DA_HEADS) * LOG2E
    q, kb, vb = (a.reshape(b, t, d) for a in (q, kb, vb))
    if prompt:
        bound2 = (DA_HEAD_DIM ** 0.5 * LOG2E * 1.02) * jnp.max(jnp.abs(qn_w)) * jnp.max(jnp.abs(kn_w))
        o = _prompt_attention(q, kb, vb, lam_vec, subln_w, slopes2, bound2.reshape(1), lam_init)
    else:
        k_cache, v_cache = cache
        o = _sample_attention(q, kb, vb, k_cache, v_cache, layer_j, lam_vec, subln_w, slopes2,
                              lam_init)
    y = _outproj(o.reshape(b * t, d), z, x2, w_out_bf, jnp.ones((1, d), F32), False,
                 _outproj_tile(b * t))
    return y.reshape(b, t, d), k_stack, v_stack


def kernel(x_prompt, x_sample, cache_k, cache_v, state_hgrn, norm_w, hgrn_w_in, hgrn_lb_logits,
           hgrn_onorm_w, hgrn_w_out, attn_w_in, attn_q_norm, attn_k_norm, attn_lambda, attn_subln,
           attn_w_out):
    depth = norm_w.shape[0]
    n_attn = cache_k.shape[0]
    bp, tp, d = x_prompt.shape
    bs, ts, _ = x_sample.shape
    past = cache_k.shape[2]
    cache_k_t = jnp.transpose(cache_k, (0, 1, 3, 4, 5, 2))
    cache_v2 = cache_v.reshape(n_attn, bs, past * DA_HEADS, 2 * DA_HEAD_DIM)
    yp, ys = x_prompt, x_sample
    kp = vp = ks_ = vs_ = None
    sp, ss = [], []
    for l in range(depth):
        j = l // N_MIXERS
        if l % N_MIXERS == 0:
            w_in = hgrn_w_in[j].astype(BF16)
            w_out = hgrn_w_out[j].astype(BF16)
            yp, s_p = _hgrn_layer(yp, None, norm_w[l], w_in, hgrn_lb_logits, hgrn_onorm_w[j],
                                  w_out, j)
            ys, s_s = _hgrn_layer(ys, state_hgrn[j], norm_w[l], w_in, hgrn_lb_logits,
                                  hgrn_onorm_w[j], w_out, j)
            sp.append(s_p)
            ss.append(s_s)
        else:
            w_in = attn_w_in[j].astype(BF16)
            w_out = attn_w_out[j].astype(BF16)
            yp, kp, vp = _attn_layer(yp, None, norm_w[l], w_in, attn_q_norm[j], attn_k_norm[j],
                                     attn_lambda[j], attn_subln[j], w_out, l, j, n_attn, kp, vp)
            ys, ks_, vs_ = _attn_layer(ys, (cache_k_t, cache_v2), norm_w[l], w_in,
                                       attn_q_norm[j], attn_k_norm[j], attn_lambda[j],
                                       attn_subln[j], w_out, l, j, n_attn, ks_, vs_)
    new_k_prompt = jnp.transpose(kp.reshape(n_attn, bp, 2, DA_HEADS, DA_HEAD_DIM, tp),
                                 (0, 1, 5, 2, 3, 4))
    return (yp, ys, new_k_prompt,
            vp.reshape(n_attn, bp, tp, DA_HEADS, 2 * DA_HEAD_DIM),
            ks_.reshape(n_attn, bs, ts, 2, DA_HEADS, DA_HEAD_DIM),
            vs_.reshape(n_attn, bs, ts, DA_HEADS, 2 * DA_HEAD_DIM),
            jnp.stack(sp), jnp.stack(ss))
```

```python
import functools
import math

import jax
import jax.numpy as jnp
from jax import lax
from jax.experimental import pallas as pl
from jax.experimental.pallas import tpu as pltpu

F32 = jnp.float32
BF16 = jnp.bfloat16

EPS = 1e-6
NEG_INF = -1e30
LOG2E = 1.4426950408889634
LANES = 128
MXU_COLS = 256
HG_HEADS = 8
HG_DK = 128
DA_HEADS = 8
DA_HEAD_DIM = 64
MASK_CHUNK = 64
N_MIXERS = 2

REC_CHUNK = 64
REC_CHUNKS_PER_STEP = 16
REC_MATMUL_LEVELS = (2, 4)
REC_ROW_LEVELS = (8, 16, 32)
ATTN_TILE = 512
FIXED_SHIFT_MAX_SCORE = 40.0
VMEM_LIMIT = 56 * 1024 * 1024


def _dot(a, b):
    return jnp.dot(a, b, preferred_element_type=F32)


def _dot_nt(a, b):
    return lax.dot_general(a, b, (((1,), (1,)), ((), ())), preferred_element_type=F32)


def _dot_tn(a, b):
    return lax.dot_general(a, b, (((0,), (0,)), ((), ())), preferred_element_type=F32)


def _rms_rows(x, w):
    ms = jnp.mean(x * x, axis=-1, keepdims=True)
    return x * lax.rsqrt(ms + EPS) * w


def _resident(shape):
    nd = len(shape)
    return pl.BlockSpec(shape, lambda *_: (0,) * nd, pipeline_mode=pl.Buffered(1))


def _params(*semantics):
    return pltpu.CompilerParams(dimension_semantics=semantics, vmem_limit_bytes=VMEM_LIMIT)


def _inproj_hgrn_kernel(x_ref, nw_ref, w_ref, q_ref, fx_ref, i_ref, z_ref):
    d = x_ref.shape[1]
    h = _rms_rows(x_ref[...], nw_ref[...]).astype(BF16)
    q_ref[...] = (_dot(h, w_ref[:, 0:d]) * (HG_DK ** -0.5)).astype(q_ref.dtype)
    for c, o_ref in ((1, fx_ref), (2, i_ref), (3, z_ref)):
        o_ref[...] = _dot(h, w_ref[:, c * d:(c + 1) * d]).astype(o_ref.dtype)


def _inproj_hgrn(x2, nw, w_bf, tm):
    m, d = x2.shape
    row = pl.BlockSpec((tm, d), lambda i: (i, 0))
    return pl.pallas_call(
        _inproj_hgrn_kernel,
        grid=(m // tm,),
        in_specs=[row, _resident((1, d)), _resident((d, 4 * d))],
        out_specs=[row, row, row, row],
        out_shape=[jax.ShapeDtypeStruct((m, d), BF16), jax.ShapeDtypeStruct((m, d), F32),
                   jax.ShapeDtypeStruct((m, d), BF16), jax.ShapeDtypeStruct((m, d), BF16)],
        compiler_params=_params("parallel"),
        name="inproj_hgrn",
    )(x2, nw, w_bf)


def _inproj_attn_kernel(k_transposed, n_alias, x_ref, nw_ref, w_ref, qnw_ref, knw_ref, g_ref, *rest):
    q_ref, kb_ref, vb_ref, z_ref, ks_ref, vs_ref = rest[n_alias:]
    d = x_ref.shape[1]
    h = _rms_rows(x_ref[...], nw_ref[...]).astype(BF16)
    gmat = g_ref[...]
    gw = gmat.shape[0]

    def group_norm(y, w):
        cols = []
        for t in range(d // gw):
            yt = y[:, t * gw:(t + 1) * gw]
            ms = _dot((yt * yt).astype(BF16), gmat)
            cols.append(yt * lax.rsqrt(ms + EPS))
        return jnp.concatenate(cols, axis=1) * w

    q_ref[...] = group_norm(_dot(h, w_ref[:, 0:d]), qnw_ref[...]).astype(BF16)
    k = group_norm(_dot(h, w_ref[:, d:2 * d]), knw_ref[...])
    kb_ref[...] = k.astype(BF16)
    ks_ref[0, 0] = k.T if k_transposed else k
    v = _dot(h, w_ref[:, 2 * d:3 * d])
    for hd in range(DA_HEADS):
        vs_ref[0, pl.ds(hd, x_ref.shape[0], stride=DA_HEADS), :] = v[:, hd * LANES:(hd + 1) * LANES]
    vb_ref[...] = v.astype(BF16)
    z_ref[...] = _dot(h, w_ref[:, 3 * d:4 * d]).astype(BF16)


def _inproj_attn(x2, seq, nw, w_bf, qnw, knw, gmat, layer_j, n_layers, k_stack, v_stack,
                 k_transposed, tm):
    m, d = x2.shape
    batch = m // seq
    tiles_per_seq = seq // tm
    row = pl.BlockSpec((tm, d), lambda i: (i, 0))
    if k_transposed:
        ks_shape = (n_layers, batch, d, seq)
        ks_spec = pl.BlockSpec((1, 1, d, tm),
                               lambda i: (layer_j, i // tiles_per_seq, 0, i % tiles_per_seq))
    else:
        ks_shape = (n_layers, batch, seq, d)
        ks_spec = pl.BlockSpec((1, 1, tm, d),
                               lambda i: (layer_j, i // tiles_per_seq, i % tiles_per_seq, 0))
    vs_spec = pl.BlockSpec((1, tm * DA_HEADS, 2 * DA_HEAD_DIM), lambda i: (layer_j, i, 0))
    in_specs = [row, _resident((1, d)), _resident((d, 4 * d)), _resident((1, d)),
                _resident((1, d)), _resident(gmat.shape)]
    args = [x2, nw, w_bf, qnw, knw, gmat]
    aliases = {}
    if k_stack is not None:
        in_specs += [pl.BlockSpec(memory_space=pl.ANY)] * 2
        aliases = {len(args): 4, len(args) + 1: 5}
        args += [k_stack, v_stack]
    shp = lambda dt: jax.ShapeDtypeStruct((m, d), dt)
    return pl.pallas_call(
        functools.partial(_inproj_attn_kernel, k_transposed, len(aliases)),
        grid=(m // tm,),
        in_specs=in_specs,
        out_specs=[row, row, row, row, ks_spec, vs_spec],
        out_shape=[shp(BF16), shp(BF16), shp(BF16), shp(BF16),
                   jax.ShapeDtypeStruct(ks_shape, F32),
                   jax.ShapeDtypeStruct((n_layers, m * DA_HEADS, 2 * DA_HEAD_DIM), F32)],
        input_output_aliases=aliases,
        compiler_params=_params("parallel"),
        name="inproj_attn",
    )(*args)


def _outproj_kernel(full_norm, o_ref, z_ref, x_ref, w_ref, ow_ref, y_ref):
    o = o_ref[...].astype(F32)
    if full_norm:
        o = _rms_rows(o, ow_ref[...])
    z = z_ref[...].astype(F32)
    gated = (o * (z * jax.nn.sigmoid(z))).astype(BF16)
    y_ref[...] = x_ref[...] + _dot(gated, w_ref[...])


def _outproj(o2, z2, x2, w_bf, ow, full_norm, tm):
    m, d = x2.shape
    row = pl.BlockSpec((tm, d), lambda i: (i, 0))
    return pl.pallas_call(
        functools.partial(_outproj_kernel, full_norm),
        grid=(m // tm,),
        in_specs=[row, row, row, _resident((d, d)), _resident((1, d))],
        out_specs=row,
        out_shape=jax.ShapeDtypeStruct((m, d), F32),
        compiler_params=_params("parallel"),
        name="outproj",
    )(o2, z2, x2, w_bf, ow)


def _range_matrices(c):
    t = jnp.arange(c)[:, None]
    s = jnp.arange(c)[None, :]
    blocks = [s <= t]
    for w in REC_MATMUL_LEVELS:
        ref = (t // (2 * w)) * (2 * w) + w - 1
        upper = (t & w) != 0
        blocks.append(jnp.where(upper, (s > ref) & (s <= t), (s > t) & (s <= ref)))
    e = jnp.concatenate(blocks, axis=0).astype(BF16)
    return jnp.concatenate([e, e], axis=1)


def _hgrn_rec_kernel(layer_j, has_s0, q_ref, fx_ref, v_ref, lbl_ref, emat_ref, *rest):
    if has_s0:
        s0_ref, o_ref, sfin_ref, st_ref = rest
    else:
        o_ref, sfin_ref, st_ref = rest
    c_idx = pl.program_id(1)
    c = REC_CHUNK
    n_chunks = q_ref.shape[1] // c
    d = q_ref.shape[2]

    @pl.when(c_idx == 0)
    def _():
        for h in range(HG_HEADS):
            if has_s0:
                st_ref[h] = s0_ref[0, h].T
            else:
                st_ref[h] = jnp.zeros((HG_DK, HG_DK), F32)

    lg = lbl_ref[...]
    ex = jnp.exp(lg - jnp.max(lg, axis=0, keepdims=True))
    p = ex / jnp.sum(ex, axis=0, keepdims=True)
    cs = p[0:1]
    for r in range(1, layer_j + 1):
        cs = cs + p[r:r + 1]
    lb = cs - p[0:1]

    row = lax.broadcasted_iota(jnp.int32, (c, d), 0)
    tt = lax.broadcasted_iota(jnp.int32, (c, c), 0)
    ss = lax.broadcasted_iota(jnp.int32, (c, c), 1)
    diag_mask = tt == ss
    level_masks = [((tt // (2 * w)) == (ss // (2 * w))) & ((tt & w) != 0) & ((ss & w) == 0)
                   for w in (1,) + REC_MATMUL_LEVELS + REC_ROW_LEVELS]
    heads = [slice(h * HG_DK, (h + 1) * HG_DK) for h in range(HG_HEADS)]

    def operands(rows_c):
        f = lb + (1.0 - lb) * jax.nn.sigmoid(fx_ref[0, rows_c, :])
        g = jnp.log2(f)
        kk = 1.0 - f
        g_hi = pltpu.bitcast(pltpu.bitcast(g, jnp.uint32) & jnp.uint32(0xFFFF0000), F32)
        g_pieces = jnp.concatenate([g_hi.astype(BF16), (g - g_hi).astype(BF16)], axis=0)
        ranges = _dot(emat_ref[...], g_pieces)
        b = ranges[0:c]

        q = q_ref[0, rows_c, :].astype(F32)
        q_state = (q * jnp.exp2(b)).astype(BF16)
        k_state = (kk * jnp.exp2(b[c - 1:c] - b)).astype(BF16)
        state_decay = jnp.exp2(b[c - 1:c])

        level_ops = [jnp.where((row & 1) != 0, q * f, kk).astype(BF16)]
        for li, w in enumerate(REC_MATMUL_LEVELS):
            x = jnp.exp2(ranges[(li + 1) * c:(li + 2) * c])
            level_ops.append((jnp.where((row & w) != 0, q, kk) * x).astype(BF16))
        for w in REC_ROW_LEVELS:
            pieces = []
            for blk in range(c // w):
                rows = slice(blk * w, (blk + 1) * w)
                mid = (blk // 2) * 2 * w + w - 1
                if blk % 2:
                    pieces.append(q[rows] * jnp.exp2(b[rows] - b[mid:mid + 1]))
                else:
                    pieces.append(kk[rows] * jnp.exp2(b[mid:mid + 1] - b[rows]))
            level_ops.append(jnp.concatenate(pieces, axis=0).astype(BF16))
        return q.astype(BF16), kk.astype(BF16), level_ops, q_state, k_state, state_decay

    def recur(rows_c, ops):
        q_bf, k_bf, level_ops, q_state, k_state, state_decay = ops
        att = []
        for sl in heads:
            a = jnp.where(diag_mask, _dot_nt(q_bf[:, sl], k_bf[:, sl]), 0.0)
            for mask, y in zip(level_masks, level_ops):
                a = jnp.where(mask, _dot_nt(y[:, sl], y[:, sl]), a)
            att.append(a.astype(BF16))
        for h, sl in enumerate(heads):
            o = (_dot(att[h], v_ref[0, rows_c, sl])
                 + _dot_nt(q_state[:, sl], st_ref[h].astype(BF16)))
            o_ref[0, rows_c, sl] = o.astype(o_ref.dtype)
        for h, sl in enumerate(heads):
            st_ref[h] = (st_ref[h] * state_decay[:, sl]
                         + _dot_tn(v_ref[0, rows_c, sl], k_state[:, sl]))

    chunk_rows = [slice(n * c, (n + 1) * c) for n in range(n_chunks)]
    ops = operands(chunk_rows[0])
    for n in range(n_chunks):
        ops_next = operands(chunk_rows[n + 1]) if n + 1 < n_chunks else None
        recur(chunk_rows[n], ops)
        ops = ops_next

    @pl.when(c_idx == pl.num_programs(1) - 1)
    def _():
        for h in range(HG_HEADS):
            sfin_ref[0, h] = st_ref[h].T


def _hgrn_rec(q, fx, v, lb_logits, s0, layer_j):
    b, t, d = q.shape
    c = min(t, REC_CHUNK * REC_CHUNKS_PER_STEP)
    blk = pl.BlockSpec((1, c, d), lambda i, j: (i, j, 0))
    st_blk = pl.BlockSpec((1, HG_HEADS, HG_DK, HG_DK), lambda i, j: (i, 0, 0, 0))
    emat = _range_matrices(REC_CHUNK)
    in_specs = [blk, blk, blk, _resident(lb_logits.shape), _resident(emat.shape)]
    args = [q, fx, v, lb_logits, emat]
    if s0 is not None:
        in_specs.append(st_blk)
        args.append(s0)
    return pl.pallas_call(
        functools.partial(_hgrn_rec_kernel, layer_j, s0 is not None),
        grid=(b, t // c),
        in_specs=in_specs,
        out_specs=[blk, st_blk],
        out_shape=[jax.ShapeDtypeStruct((b, t, d), BF16),
                   jax.ShapeDtypeStruct((b, HG_HEADS, HG_DK, HG_DK), F32)],
        scratch_shapes=[pltpu.VMEM((HG_HEADS, HG_DK, HG_DK), F32)],
        compiler_params=_params("parallel", "arbitrary"),
        name="hgrn_rec",
    )(*args)


def _masked_queries(q_refs):
    tq = q_refs[0].shape[1]
    lane = lax.broadcasted_iota(jnp.int32, (tq, LANES), 1)
    zero = jnp.zeros((tq, LANES), BF16)
    return [[jnp.where(lane < DA_HEAD_DIM, qr[0], zero), jnp.where(lane >= DA_HEAD_DIM, qr[0], zero)]
            for qr in q_refs]


def _lambda_scalar(lam_ref, lam_init):
    lv = lam_ref[...]
    return (jnp.exp(jnp.sum(lv[0:1] * lv[1:2], axis=1, keepdims=True))
            - jnp.exp(jnp.sum(lv[2:3] * lv[3:4], axis=1, keepdims=True)) + lam_init)


def _finish_transposed(lam_ref, subw_ref, lam_init, l_sc, acc_sc, o_ref):
    lam = _lambda_scalar(lam_ref, lam_init)
    for r in range(2):
        o = acc_sc[2 * r] / l_sc[2 * r] - lam * (acc_sc[2 * r + 1] / l_sc[2 * r + 1])
        ms = jnp.mean(o * o, axis=0, keepdims=True)
        o = o * lax.rsqrt(ms + EPS) * subw_ref[...] * (1.0 - lam_init)
        o_ref[0, :, r * LANES:(r + 1) * LANES] = o.T.astype(o_ref.dtype)


def _attn_rows_kernel(lam_init, n_q, slopes_ref, shift_ref, lam_ref, subw_ref, q0_ref, q1_ref,
                      k0_ref, k1_ref, v_ref, o_ref, bias_sc):
    j = pl.program_id(0)
    bi = pl.program_id(1)
    i = pl.program_id(2)
    t = q0_ref.shape[1]

    @pl.when((bi == 0) & (i == 0))
    def _():
        off = lax.broadcasted_iota(jnp.int32, (n_q * t, t), 0) - (n_q - 1) * t
        col = lax.broadcasted_iota(jnp.int32, (n_q * t, t), 1)
        dist = jnp.abs(col - off).astype(F32)
        visible = (off // MASK_CHUNK) <= (col // MASK_CHUNK)
        for r in range(2):
            bias_sc[:, r * t:(r + 1) * t] = jnp.where(
                visible, -slopes_ref[2 * j + r] * dist - shift_ref[0], NEG_INF)

    qm = _masked_queries((q0_ref, q1_ref))
    q_pair = [jnp.concatenate(qm[mp], axis=0) for mp in range(2)]
    k_refs = (k0_ref, k1_ref)
    lam = _lambda_scalar(lam_ref, lam_init)

    def attend(tile_idx):
        rows = (tile_idx + 1) * t
        bias_row0 = (n_q - 1 - tile_idx) * t
        s = [_dot_nt(k_refs[mp][0, 0:rows, :], q_pair[mp]) for mp in range(2)]
        for r in range(2):
            cols = slice(r * t, (r + 1) * t)
            bias = bias_sc[bias_row0:bias_row0 + rows, cols]
            p = [jnp.exp2(s[mp][:, cols] + bias) for mp in range(2)]
            inv0 = 1.0 / jnp.sum(p[0], axis=0, keepdims=True)
            inv1 = lam / jnp.sum(p[1], axis=0, keepdims=True)
            num = _dot_tn(v_ref[0, 0:rows, r * LANES:(r + 1) * LANES],
                          jnp.concatenate([pm.astype(BF16) for pm in p], axis=1))
            o = num[:, :t] * inv0 - num[:, t:] * inv1
            ms = jnp.mean(o * o, axis=0, keepdims=True)
            o = o * lax.rsqrt(ms + EPS) * subw_ref[...] * (1.0 - lam_init)
            o_ref[0, :, r * LANES:(r + 1) * LANES] = o.T.astype(o_ref.dtype)

    for tile_idx in range(n_q):
        pl.when(i == tile_idx)(functools.partial(attend, tile_idx))


def _attn_online_kernel(lam_init, slopes_ref, lam_ref, subw_ref, q0_ref, q1_ref, k0_ref, k1_ref,
                        v_ref, o_ref, m_sc, l_sc, acc_sc):
    j = pl.program_id(1)
    i = pl.program_id(2)
    t = q0_ref.shape[1]
    qm = _masked_queries((q0_ref, q1_ref))
    k_refs = (k0_ref, k1_ref)
    m_sc[...] = jnp.full(m_sc.shape, -jnp.inf, F32)
    l_sc[...] = jnp.zeros(l_sc.shape, F32)
    acc_sc[...] = jnp.zeros(acc_sc.shape, F32)

    def tile(kt_idx, diagonal):
        k_start = pl.multiple_of(kt_idx * t, t)
        row = lax.broadcasted_iota(jnp.int32, (t, t), 0)
        col = lax.broadcasted_iota(jnp.int32, (t, t), 1)
        dist = jnp.abs((i - kt_idx) * t + (col - row)).astype(F32)
        if diagonal:
            visible = (row // MASK_CHUNK) <= (col // MASK_CHUNK)
        for r in range(2):
            bias = -slopes_ref[2 * j + r] * dist
            vt = v_ref[0, pl.ds(k_start, t), r * LANES:(r + 1) * LANES]
            for mp in range(2):
                idx = 2 * r + mp
                kt = k_refs[mp][0, pl.ds(k_start, t), :]
                s = _dot_nt(kt, qm[mp][r]) + bias
                if diagonal:
                    s = jnp.where(visible, s, NEG_INF)
                m_prev = m_sc[idx]
                m_new = jnp.maximum(m_prev, jnp.max(s, axis=0, keepdims=True))
                alpha = jnp.exp2(m_prev - m_new)
                p = jnp.exp2(s - m_new)
                l_sc[idx] = alpha * l_sc[idx] + jnp.sum(p, axis=0, keepdims=True)
                acc_sc[idx] = alpha * acc_sc[idx] + _dot_tn(vt, p.astype(BF16))
                m_sc[idx] = m_new

    def below(kt_idx, carry):
        tile(kt_idx, False)
        return carry

    lax.fori_loop(0, i, below, 0)
    tile(i, True)
    _finish_transposed(lam_ref, subw_ref, lam_init, l_sc, acc_sc, o_ref)


def _prompt_attention(q, k, v, lam_vec, subln_w, slopes2, score_bound2, lam_init):
    b, t, d = q.shape
    tile = ATTN_TILE
    half = d // (2 * LANES)
    q_spec = lambda off: pl.BlockSpec((1, tile, LANES), lambda bi, j, i: (bi, i, j + off))
    k_spec = lambda off: pl.BlockSpec((1, t, LANES), lambda bi, j, i: (bi, 0, j + off))
    v_spec = pl.BlockSpec((1, t, 2 * LANES), lambda bi, j, i: (bi, 0, j))
    o_spec = pl.BlockSpec((1, tile, 2 * LANES), lambda bi, j, i: (bi, i, j))
    smem = pl.BlockSpec(memory_space=pltpu.SMEM)
    lam_spec = pl.BlockSpec(lam_vec.shape, lambda bi, j, i: (0, 0))
    subw_spec = pl.BlockSpec((2 * DA_HEAD_DIM, 1), lambda bi, j, i: (0, 0))
    data_specs = [q_spec(0), q_spec(half), k_spec(0), k_spec(half), v_spec]
    subw = subln_w.reshape(2 * DA_HEAD_DIM, 1)
    stat = pltpu.VMEM((4, 1, tile), F32)
    acc = pltpu.VMEM((4, 2 * DA_HEAD_DIM, tile), F32)
    common = dict(grid=(b, half, t // tile), out_specs=o_spec,
                  out_shape=jax.ShapeDtypeStruct((b, t, d), BF16),
                  compiler_params=_params("parallel", "parallel", "arbitrary"))

    def fixed(shift):
        reorder = lambda spec: pl.BlockSpec(spec.block_shape,
                                            lambda j, bi, i, f=spec.index_map: f(bi, j, i))
        return pl.pallas_call(
            functools.partial(_attn_rows_kernel, lam_init, t // tile),
            grid=(half, b, t // tile),
            in_specs=[smem, smem, reorder(lam_spec), reorder(subw_spec)]
            + [reorder(spec) for spec in data_specs],
            out_specs=reorder(o_spec),
            out_shape=jax.ShapeDtypeStruct((b, t, d), BF16),
            scratch_shapes=[pltpu.VMEM((t, 2 * tile), F32)],
            compiler_params=_params("arbitrary", "arbitrary", "arbitrary"),
            name="diff_attn_fixed",
        )(slopes2, shift, lam_vec, subw, q, q, k, k, v)

    def online(_):
        return pl.pallas_call(
            functools.partial(_attn_online_kernel, lam_init),
            in_specs=[smem, lam_spec, subw_spec] + data_specs,
            scratch_shapes=[stat, stat, acc],
            name="diff_attn_online", **common,
        )(slopes2, lam_vec, subw, q, q, k, k, v)

    return lax.cond(score_bound2[0] <= FIXED_SHIFT_MAX_SCORE * LOG2E, fixed, online, score_bound2)


def _sample_attn_kernel(lam_init, past_visible, slopes_ref, lam_ref, subw_ref, q_ref, kc_ref,
                        kn_ref, vc_ref, vn_ref, o_ref):
    tq = q_ref.shape[1]
    past = kc_ref.shape[3]
    half = DA_HEADS // 2
    lam = _lambda_scalar(lam_ref, lam_init)
    lane = lax.broadcasted_iota(jnp.int32, (tq, LANES), 1)
    zero = jnp.zeros((tq, LANES), BF16)

    row_c = lax.broadcasted_iota(jnp.int32, (tq, past), 0)
    col_c = lax.broadcasted_iota(jnp.int32, (tq, past), 1)
    dist_c = (past + row_c - col_c).astype(F32)
    row_n = lax.broadcasted_iota(jnp.int32, (tq, tq), 0)
    col_n = lax.broadcasted_iota(jnp.int32, (tq, tq), 1)
    dist_n = jnp.abs(row_n - col_n).astype(F32)
    if not past_visible:
        vis_c = (col_c // MASK_CHUNK) <= ((past + row_c) // MASK_CHUNK)
        vis_n = ((past + col_n) // MASK_CHUNK) <= ((past + row_n) // MASK_CHUNK)

    for pair in range(half):
        tiles = [slice((mp * half + pair) * LANES, (mp * half + pair + 1) * LANES) for mp in range(2)]
        kc = [kc_ref[mp, 2 * pair:2 * pair + 2].reshape(LANES, past).astype(BF16) for mp in range(2)]
        for r in range(2):
            head = 2 * pair + r
            slope = slopes_ref[head]
            keep = (lane >= DA_HEAD_DIM) if r else (lane < DA_HEAD_DIM)
            vc = vc_ref[pl.ds(head, past, stride=DA_HEADS), :].astype(BF16)
            vn = vn_ref[0, :, head * LANES:(head + 1) * LANES]
            outs = []
            for mp in range(2):
                qm = jnp.where(keep, q_ref[0, :, tiles[mp]], zero)
                s_c = _dot(qm, kc[mp]) - slope * dist_c
                s_n = _dot_nt(qm, kn_ref[0, :, tiles[mp]]) - slope * dist_n
                if not past_visible:
                    s_c = jnp.where(vis_c, s_c, NEG_INF)
                    s_n = jnp.where(vis_n, s_n, NEG_INF)
                m = jnp.maximum(jnp.max(s_c, axis=1, keepdims=True),
                                jnp.max(s_n, axis=1, keepdims=True))
                p_c = jnp.exp2(s_c - m)
                p_n = jnp.exp2(s_n - m)
                l = jnp.sum(p_c, axis=1, keepdims=True) + jnp.sum(p_n, axis=1, keepdims=True)
                acc = _dot(p_c.astype(BF16), vc) + _dot(p_n.astype(BF16), vn)
                outs.append(acc / l)
            o = outs[0] - lam * outs[1]
            ms = jnp.mean(o * o, axis=1, keepdims=True)
            o = o * lax.rsqrt(ms + EPS) * subw_ref[...] * (1.0 - lam_init)
            o_ref[0, :, head * LANES:(head + 1) * LANES] = o.astype(o_ref.dtype)


def _sample_attention(q, k_new, v_new, k_cache, v_cache, layer_j, lam_vec, subln_w, slopes2,
                      lam_init):
    b, t_q, d = q.shape
    past = k_cache.shape[5]
    past_visible = past % MASK_CHUNK == 0 and t_q <= MASK_CHUNK
    row = pl.BlockSpec((1, t_q, d), lambda bi: (bi, 0, 0))
    return pl.pallas_call(
        functools.partial(_sample_attn_kernel, lam_init, past_visible),
        grid=(b,),
        in_specs=[pl.BlockSpec(memory_space=pltpu.SMEM),
                  pl.BlockSpec(lam_vec.shape, lambda bi: (0, 0)),
                  pl.BlockSpec((1, 2 * DA_HEAD_DIM), lambda bi: (0, 0)),
                  row,
                  pl.BlockSpec((None, None, 2, DA_HEADS, DA_HEAD_DIM, past),
                               lambda bi: (layer_j, bi, 0, 0, 0, 0)),
                  row,
                  pl.BlockSpec((None, None, past * DA_HEADS, 2 * DA_HEAD_DIM),
                               lambda bi: (layer_j, bi, 0, 0)),
                  row],
        out_specs=row,
        out_shape=jax.ShapeDtypeStruct((b, t_q, d), BF16),
        compiler_params=_params("parallel"),
        name="diff_attn_sample",
    )(slopes2, lam_vec, subln_w.reshape(1, 2 * DA_HEAD_DIM), q, k_cache, k_new, v_cache, v_new)


def _row_tile(m):
    return 512 if m % 512 == 0 else 256


def _tall_tile(m):
    return 1024 if m % 2048 == 0 else _row_tile(m)


def _hgrn_layer(x, s0, nw, w_in_bf, lb_logits, onw, w_out_bf, layer_j):
    b, t, d = x.shape
    x2 = x.reshape(b * t, d)
    q, fx, iv, z = _inproj_hgrn(x2, nw.reshape(1, d), w_in_bf, _tall_tile(b * t))
    o, s_new = _hgrn_rec(q.reshape(b, t, d), fx.reshape(b, t, d), iv.reshape(b, t, d),
                         lb_logits, s0, layer_j)
    y = _outproj(o.reshape(b * t, d), z, x2, w_out_bf, onw.reshape(1, d), True,
                 _tall_tile(b * t))
    return y.reshape(b, t, d), s_new


def _attn_layer(x, cache, nw, w_in_bf, qn_w, kn_w, lam_vec, subln_w, w_out_bf, layer_idx,
                layer_j, n_layers, k_stack, v_stack):
    b, t, d = x.shape
    x2 = x.reshape(b * t, d)
    groups = d // DA_HEAD_DIM
    gmat = jnp.kron(jnp.eye(MXU_COLS // DA_HEAD_DIM, dtype=F32),
                    jnp.full((DA_HEAD_DIM, DA_HEAD_DIM), 1.0 / DA_HEAD_DIM, F32)).astype(BF16)
    prompt = cache is None
    tm = 512 if t % 512 == 0 else t
    q, kb, vb, z, k_stack, v_stack = _inproj_attn(
        x2, t, nw.reshape(1, d), w_in_bf,
        jnp.tile(qn_w * (DA_HEAD_DIM ** -0.5 * LOG2E), groups).reshape(1, d),
        jnp.tile(kn_w, groups).reshape(1, d), gmat, layer_j, n_layers, k_stack, v_stack, prompt, tm)
    lam_init = 0.8 - 0.6 * math.exp(-0.3 * layer_idx)
    slopes2 = jnp.exp2(-8.0 * jnp.arange(1, DA_HEADS + 1, dtype=F32) / DA_HEADS) * LOG2E
    q, kb, vb = (a.reshape(b, t, d) for a in (q, kb, vb))
    if prompt:
        bound2 = (DA_HEAD_DIM ** 0.5 * LOG2E * 1.02) * jnp.max(jnp.abs(qn_w)) * jnp.max(jnp.abs(kn_w))
        o = _prompt_attention(q, kb, vb, lam_vec, subln_w, slopes2, bound2.reshape(1), lam_init)
    else:
        k_cache, v_cache = cache
        o = _sample_attention(q, kb, vb, k_cache, v_cache, layer_j, lam_vec, subln_w, slopes2,
                              lam_init)
    y = _outproj(o.reshape(b * t, d), z, x2, w_out_bf, jnp.ones((1, d), F32), False,
                 _tall_tile(b * t))
    return y.reshape(b, t, d), k_stack, v_stack


def kernel(x_prompt, x_sample, cache_k, cache_v, state_hgrn, norm_w, hgrn_w_in, hgrn_lb_logits,
           hgrn_onorm_w, hgrn_w_out, attn_w_in, attn_q_norm, attn_k_norm, attn_lambda, attn_subln,
           attn_w_out):
    depth = norm_w.shape[0]
    n_attn = cache_k.shape[0]
    bp, tp, d = x_prompt.shape
    bs, ts, _ = x_sample.shape
    past = cache_k.shape[2]
    cache_k_t = jnp.transpose(cache_k, (0, 1, 3, 4, 5, 2))
    cache_v2 = cache_v.reshape(n_attn, bs, past * DA_HEADS, 2 * DA_HEAD_DIM)
    yp, ys = x_prompt, x_sample
    kp = vp = ks_ = vs_ = None
    sp, ss = [], []
    for l in range(depth):
        j = l // N_MIXERS
        if l % N_MIXERS == 0:
            w_in = hgrn_w_in[j].astype(BF16)
            w_out = hgrn_w_out[j].astype(BF16)
            yp, s_p = _hgrn_layer(yp, None, norm_w[l], w_in, hgrn_lb_logits, hgrn_onorm_w[j],
                                  w_out, j)
            ys, s_s = _hgrn_layer(ys, state_hgrn[j], norm_w[l], w_in, hgrn_lb_logits,
                                  hgrn_onorm_w[j], w_out, j)
            sp.append(s_p)
            ss.append(s_s)
        else:
            w_in = attn_w_in[j].astype(BF16)
            w_out = attn_w_out[j].astype(BF16)
            yp, kp, vp = _attn_layer(yp, None, norm_w[l], w_in, attn_q_norm[j], attn_k_norm[j],
                                     attn_lambda[j], attn_subln[j], w_out, l, j, n_attn, kp, vp)
            ys, ks_, vs_ = _attn_layer(ys, (cache_k_t, cache_v2), norm_w[l], w_in,
                                       attn_q_norm[j], attn_k_norm[j], attn_lambda[j],
                                       attn_subln[j], w_out, l, j, n_attn, ks_, vs_)
    new_k_prompt = jnp.transpose(kp.reshape(n_attn, bp, 2, DA_HEADS, DA_HEAD_DIM, tp),
                                 (0, 1, 5, 2, 3, 4))
    return (yp, ys, new_k_prompt,
            vp.reshape(n_attn, bp, tp, DA_HEADS, 2 * DA_HEAD_DIM),
            ks_.reshape(n_attn, bs, ts, 2, DA_HEADS, DA_HEAD_DIM),
            vs_.reshape(n_attn, bs, ts, DA_HEADS, 2 * DA_HEAD_DIM),
            jnp.stack(sp), jnp.stack(ss))
```

```python
import functools
import math

import jax
import jax.numpy as jnp
from jax import lax
from jax.experimental import pallas as pl
from jax.experimental.pallas import tpu as pltpu

F32 = jnp.float32
BF16 = jnp.bfloat16

EPS = 1e-6
NEG_INF = -1e30
LOG2E = 1.4426950408889634
LANES = 128
MXU_COLS = 256
HG_HEADS = 8
HG_DK = 128
DA_HEADS = 8
DA_HEAD_DIM = 64
MASK_CHUNK = 64
N_MIXERS = 2

REC_CHUNK = 64
REC_CHUNKS_PER_STEP = 16
REC_MATMUL_LEVELS = (2, 4)
REC_ROW_LEVELS = (8, 16, 32)
ATTN_TILE = 512
FIXED_SHIFT_MAX_SCORE = 40.0
VMEM_LIMIT = 56 * 1024 * 1024


def _dot(a, b):
    return jnp.dot(a, b, preferred_element_type=F32)


def _dot_nt(a, b):
    return lax.dot_general(a, b, (((1,), (1,)), ((), ())), preferred_element_type=F32)


def _dot_tn(a, b):
    return lax.dot_general(a, b, (((0,), (0,)), ((), ())), preferred_element_type=F32)


def _rms_rows(x, w):
    ms = jnp.mean(x * x, axis=-1, keepdims=True)
    return x * lax.rsqrt(ms + EPS) * w


def _resident(shape):
    nd = len(shape)
    return pl.BlockSpec(shape, lambda *_: (0,) * nd, pipeline_mode=pl.Buffered(1))


def _params(*semantics):
    return pltpu.CompilerParams(dimension_semantics=semantics, vmem_limit_bytes=VMEM_LIMIT)


def _inproj_hgrn_kernel(x_ref, nw_ref, w_ref, q_ref, fx_ref, i_ref, z_ref):
    d = x_ref.shape[1]
    h = _rms_rows(x_ref[...], nw_ref[...]).astype(BF16)
    q_ref[...] = (_dot(h, w_ref[:, 0:d]) * (HG_DK ** -0.5)).astype(q_ref.dtype)
    for c, o_ref in ((1, fx_ref), (2, i_ref), (3, z_ref)):
        o_ref[...] = _dot(h, w_ref[:, c * d:(c + 1) * d]).astype(o_ref.dtype)


def _inproj_hgrn(x2, nw, w_bf, tm):
    m, d = x2.shape
    row = pl.BlockSpec((tm, d), lambda i: (i, 0))
    return pl.pallas_call(
        _inproj_hgrn_kernel,
        grid=(m // tm,),
        in_specs=[row, _resident((1, d)), _resident((d, 4 * d))],
        out_specs=[row, row, row, row],
        out_shape=[jax.ShapeDtypeStruct((m, d), BF16), jax.ShapeDtypeStruct((m, d), F32),
                   jax.ShapeDtypeStruct((m, d), BF16), jax.ShapeDtypeStruct((m, d), BF16)],
        compiler_params=_params("parallel"),
        name="inproj_hgrn",
    )(x2, nw, w_bf)


def _inproj_attn_kernel(k_transposed, n_alias, x_ref, nw_ref, w_ref, qnw_ref, knw_ref, g_ref, *rest):
    q_ref, kb_ref, vb_ref, z_ref, ks_ref, vs_ref = rest[n_alias:]
    d = x_ref.shape[1]
    h = _rms_rows(x_ref[...], nw_ref[...]).astype(BF16)
    gmat = g_ref[...]
    gw = gmat.shape[0]

    def group_norm(y, w):
        cols = []
        for t in range(d // gw):
            yt = y[:, t * gw:(t + 1) * gw]
            ms = _dot((yt * yt).astype(BF16), gmat)
            cols.append(yt * lax.rsqrt(ms + EPS))
        return jnp.concatenate(cols, axis=1) * w

    q = group_norm(_dot(h, w_ref[:, 0:d]), qnw_ref[...])
    q_ref[...] = (q * (DA_HEAD_DIM ** -0.5 * LOG2E)).astype(BF16)
    k = group_norm(_dot(h, w_ref[:, d:2 * d]), knw_ref[...])
    kb_ref[...] = k.astype(BF16)
    if k_transposed:
        ks_ref[0, 0] = k.T
    else:
        ks_ref[0] = k
    v = _dot(h, w_ref[:, 2 * d:3 * d])
    for hd in range(DA_HEADS):
        vs_ref[0, pl.ds(hd, x_ref.shape[0], stride=DA_HEADS), :] = v[:, hd * LANES:(hd + 1) * LANES]
    vb_ref[...] = v.astype(BF16)
    z_ref[...] = _dot(h, w_ref[:, 3 * d:4 * d]).astype(BF16)


def _inproj_attn(x2, seq, nw, w_bf, qnw, knw, gmat, layer_j, n_layers, k_stack, v_stack,
                 k_transposed, tm):
    m, d = x2.shape
    row = pl.BlockSpec((tm, d), lambda i: (i, 0))
    if k_transposed:
        tiles_per_seq = seq // tm
        ks_shape = (n_layers, m // seq, d, seq)
        ks_spec = pl.BlockSpec((1, 1, d, tm),
                               lambda i: (layer_j, i // tiles_per_seq, 0, i % tiles_per_seq))
    else:
        ks_shape = (n_layers, m, d)
        ks_spec = pl.BlockSpec((1, tm, d), lambda i: (layer_j, i, 0))
    vs_spec = pl.BlockSpec((1, tm * DA_HEADS, 2 * DA_HEAD_DIM), lambda i: (layer_j, i, 0))
    in_specs = [row, _resident((1, d)), _resident((d, 4 * d)), _resident((1, d)),
                _resident((1, d)), _resident(gmat.shape)]
    args = [x2, nw, w_bf, qnw, knw, gmat]
    aliases = {}
    if k_stack is not None:
        in_specs += [pl.BlockSpec(memory_space=pl.ANY)] * 2
        aliases = {len(args): 4, len(args) + 1: 5}
        args += [k_stack, v_stack]
    shp = lambda dt: jax.ShapeDtypeStruct((m, d), dt)
    return pl.pallas_call(
        functools.partial(_inproj_attn_kernel, k_transposed, len(aliases)),
        grid=(m // tm,),
        in_specs=in_specs,
        out_specs=[row, row, row, row, ks_spec, vs_spec],
        out_shape=[shp(BF16), shp(BF16), shp(BF16), shp(BF16),
                   jax.ShapeDtypeStruct(ks_shape, F32),
                   jax.ShapeDtypeStruct((n_layers, m * DA_HEADS, 2 * DA_HEAD_DIM), F32)],
        input_output_aliases=aliases,
        compiler_params=_params("parallel"),
        name="inproj_attn",
    )(*args)


def _outproj_kernel(full_norm, o_ref, z_ref, x_ref, w_ref, ow_ref, y_ref):
    o = o_ref[...].astype(F32)
    if full_norm:
        o = _rms_rows(o, ow_ref[...])
    z = z_ref[...].astype(F32)
    gated = (o * (z * jax.nn.sigmoid(z))).astype(BF16)
    y_ref[...] = x_ref[...] + _dot(gated, w_ref[...])


def _outproj(o2, z2, x2, w_bf, ow, full_norm, tm):
    m, d = x2.shape
    row = pl.BlockSpec((tm, d), lambda i: (i, 0))
    return pl.pallas_call(
        functools.partial(_outproj_kernel, full_norm),
        grid=(m // tm,),
        in_specs=[row, row, row, _resident((d, d)), _resident((1, d))],
        out_specs=row,
        out_shape=jax.ShapeDtypeStruct((m, d), F32),
        compiler_params=_params("parallel"),
        name="outproj",
    )(o2, z2, x2, w_bf, ow)


def _range_matrices(c):
    t = jnp.arange(c)[:, None]
    s = jnp.arange(c)[None, :]
    blocks = [s <= t]
    for w in REC_MATMUL_LEVELS:
        ref = (t // (2 * w)) * (2 * w) + w - 1
        upper = (t & w) != 0
        blocks.append(jnp.where(upper, (s > ref) & (s <= t), (s > t) & (s <= ref)))
    e = jnp.concatenate(blocks, axis=0).astype(BF16)
    return jnp.concatenate([e, e], axis=1)


def _hgrn_rec_kernel(layer_j, has_s0, q_ref, fx_ref, v_ref, lbl_ref, emat_ref, *rest):
    if has_s0:
        s0_ref, o_ref, sfin_ref, st_ref = rest
    else:
        o_ref, sfin_ref, st_ref = rest
    c_idx = pl.program_id(1)
    c = REC_CHUNK
    n_chunks = q_ref.shape[1] // c
    d = q_ref.shape[2]

    @pl.when(c_idx == 0)
    def _():
        for h in range(HG_HEADS):
            if has_s0:
                st_ref[h] = s0_ref[0, h].T
            else:
                st_ref[h] = jnp.zeros((HG_DK, HG_DK), F32)

    lg = lbl_ref[...]
    ex = jnp.exp(lg - jnp.max(lg, axis=0, keepdims=True))
    p = ex / jnp.sum(ex, axis=0, keepdims=True)
    cs = p[0:1]
    for r in range(1, layer_j + 1):
        cs = cs + p[r:r + 1]
    lb = cs - p[0:1]

    row = lax.broadcasted_iota(jnp.int32, (c, d), 0)
    tt = lax.broadcasted_iota(jnp.int32, (c, c), 0)
    ss = lax.broadcasted_iota(jnp.int32, (c, c), 1)
    diag_mask = tt == ss
    level_masks = [((tt // (2 * w)) == (ss // (2 * w))) & ((tt & w) != 0) & ((ss & w) == 0)
                   for w in (1,) + REC_MATMUL_LEVELS + REC_ROW_LEVELS]
    heads = [slice(h * HG_DK, (h + 1) * HG_DK) for h in range(HG_HEADS)]

    def operands(rows_c):
        f = lb + (1.0 - lb) * jax.nn.sigmoid(fx_ref[0, rows_c, :])
        g = jnp.log2(f)
        kk = 1.0 - f
        g_hi = pltpu.bitcast(pltpu.bitcast(g, jnp.uint32) & jnp.uint32(0xFFFF0000), F32)
        g_pieces = jnp.concatenate([g_hi.astype(BF16), (g - g_hi).astype(BF16)], axis=0)
        ranges = _dot(emat_ref[...], g_pieces)
        b = ranges[0:c]

        q = q_ref[0, rows_c, :].astype(F32)
        q_state = (q * jnp.exp2(b)).astype(BF16)
        k_state = (kk * jnp.exp2(b[c - 1:c] - b)).astype(BF16)
        state_decay = jnp.exp2(b[c - 1:c])

        level_ops = [jnp.where((row & 1) != 0, q * f, kk).astype(BF16)]
        for li, w in enumerate(REC_MATMUL_LEVELS):
            x = jnp.exp2(ranges[(li + 1) * c:(li + 2) * c])
            level_ops.append((jnp.where((row & w) != 0, q, kk) * x).astype(BF16))
        for w in REC_ROW_LEVELS:
            pieces = []
            for blk in range(c // w):
                rows = slice(blk * w, (blk + 1) * w)
                mid = (blk // 2) * 2 * w + w - 1
                if blk % 2:
                    pieces.append(q[rows] * jnp.exp2(b[rows] - b[mid:mid + 1]))
                else:
                    pieces.append(kk[rows] * jnp.exp2(b[mid:mid + 1] - b[rows]))
            level_ops.append(jnp.concatenate(pieces, axis=0).astype(BF16))
        return q.astype(BF16), kk.astype(BF16), level_ops, q_state, k_state, state_decay

    def recur(rows_c, ops):
        q_bf, k_bf, level_ops, q_state, k_state, state_decay = ops
        att = []
        for sl in heads:
            a = jnp.where(diag_mask, _dot_nt(q_bf[:, sl], k_bf[:, sl]), 0.0)
            for mask, y in zip(level_masks, level_ops):
                a = jnp.where(mask, _dot_nt(y[:, sl], y[:, sl]), a)
            att.append(a.astype(BF16))
        for h, sl in enumerate(heads):
            o = (_dot(att[h], v_ref[0, rows_c, sl])
                 + _dot_nt(q_state[:, sl], st_ref[h].astype(BF16)))
            o_ref[0, rows_c, sl] = o.astype(o_ref.dtype)
        for h, sl in enumerate(heads):
            st_ref[h] = (st_ref[h] * state_decay[:, sl]
                         + _dot_tn(v_ref[0, rows_c, sl], k_state[:, sl]))

    chunk_rows = [slice(n * c, (n + 1) * c) for n in range(n_chunks)]
    ops = operands(chunk_rows[0])
    for n in range(n_chunks):
        ops_next = operands(chunk_rows[n + 1]) if n + 1 < n_chunks else None
        recur(chunk_rows[n], ops)
        ops = ops_next

    @pl.when(c_idx == pl.num_programs(1) - 1)
    def _():
        for h in range(HG_HEADS):
            sfin_ref[0, h] = st_ref[h].T


def _hgrn_rec(q, fx, v, lb_logits, s0, layer_j):
    b, t, d = q.shape
    c = min(t, REC_CHUNK * REC_CHUNKS_PER_STEP)
    blk = pl.BlockSpec((1, c, d), lambda i, j: (i, j, 0))
    st_blk = pl.BlockSpec((1, HG_HEADS, HG_DK, HG_DK), lambda i, j: (i, 0, 0, 0))
    emat = _range_matrices(REC_CHUNK)
    in_specs = [blk, blk, blk, _resident(lb_logits.shape), _resident(emat.shape)]
    args = [q, fx, v, lb_logits, emat]
    if s0 is not None:
        in_specs.append(st_blk)
        args.append(s0)
    return pl.pallas_call(
        functools.partial(_hgrn_rec_kernel, layer_j, s0 is not None),
        grid=(b, t // c),
        in_specs=in_specs,
        out_specs=[blk, st_blk],
        out_shape=[jax.ShapeDtypeStruct((b, t, d), BF16),
                   jax.ShapeDtypeStruct((b, HG_HEADS, HG_DK, HG_DK), F32)],
        scratch_shapes=[pltpu.VMEM((HG_HEADS, HG_DK, HG_DK), F32)],
        compiler_params=_params("parallel", "arbitrary"),
        name="hgrn_rec",
    )(*args)


def _masked_queries(q_refs):
    tq = q_refs[0].shape[1]
    lane = lax.broadcasted_iota(jnp.int32, (tq, LANES), 1)
    zero = jnp.zeros((tq, LANES), BF16)
    return [[jnp.where(lane < DA_HEAD_DIM, qr[0], zero), jnp.where(lane >= DA_HEAD_DIM, qr[0], zero)]
            for qr in q_refs]


def _lambda_scalar(lam_ref, lam_init):
    lv = lam_ref[...]
    return (jnp.exp(jnp.sum(lv[0:1] * lv[1:2], axis=1, keepdims=True))
            - jnp.exp(jnp.sum(lv[2:3] * lv[3:4], axis=1, keepdims=True)) + lam_init)


def _finish_transposed(lam_ref, subw_ref, lam_init, l_sc, acc_sc, o_ref):
    lam = _lambda_scalar(lam_ref, lam_init)
    for r in range(2):
        o = acc_sc[2 * r] / l_sc[2 * r] - lam * (acc_sc[2 * r + 1] / l_sc[2 * r + 1])
        ms = jnp.mean(o * o, axis=0, keepdims=True)
        o = o * lax.rsqrt(ms + EPS) * subw_ref[...] * (1.0 - lam_init)
        o_ref[0, :, r * LANES:(r + 1) * LANES] = o.T.astype(o_ref.dtype)


def _attn_rows_kernel(lam_init, n_q, slopes_ref, shift_ref, lam_ref, subw_ref, q0_ref, q1_ref,
                      k0_ref, k1_ref, v_ref, o_ref, bias_sc):
    j = pl.program_id(0)
    bi = pl.program_id(1)
    i = pl.program_id(2)
    t = q0_ref.shape[1]

    @pl.when((bi == 0) & (i == 0))
    def _():
        off = lax.broadcasted_iota(jnp.int32, (n_q * t, t), 0) - (n_q - 1) * t
        col = lax.broadcasted_iota(jnp.int32, (n_q * t, t), 1)
        dist = jnp.abs(col - off).astype(F32)
        visible = (off // MASK_CHUNK) <= (col // MASK_CHUNK)
        for r in range(2):
            bias_sc[:, r * t:(r + 1) * t] = jnp.where(
                visible, -slopes_ref[2 * j + r] * dist - shift_ref[0], NEG_INF)

    qm = _masked_queries((q0_ref, q1_ref))
    q_pair = [jnp.concatenate(qm[mp], axis=0) for mp in range(2)]
    k_refs = (k0_ref, k1_ref)
    lam = _lambda_scalar(lam_ref, lam_init)

    def attend(tile_idx):
        rows = (tile_idx + 1) * t
        bias_row0 = (n_q - 1 - tile_idx) * t
        s = [_dot_nt(k_refs[mp][0, 0:rows, :], q_pair[mp]) for mp in range(2)]
        for r in range(2):
            cols = slice(r * t, (r + 1) * t)
            bias = bias_sc[bias_row0:bias_row0 + rows, cols]
            p = [jnp.exp2(s[mp][:, cols] + bias) for mp in range(2)]
            inv0 = 1.0 / jnp.sum(p[0], axis=0, keepdims=True)
            inv1 = lam / jnp.sum(p[1], axis=0, keepdims=True)
            num = _dot_tn(v_ref[0, 0:rows, r * LANES:(r + 1) * LANES],
                          jnp.concatenate([pm.astype(BF16) for pm in p], axis=1))
            o = num[:, :t] * inv0 - num[:, t:] * inv1
            ms = jnp.mean(o * o, axis=0, keepdims=True)
            o = o * lax.rsqrt(ms + EPS) * subw_ref[...] * (1.0 - lam_init)
            o_ref[0, :, r * LANES:(r + 1) * LANES] = o.T.astype(o_ref.dtype)

    for tile_idx in range(n_q):
        pl.when(i == tile_idx)(functools.partial(attend, tile_idx))


def _attn_online_kernel(lam_init, slopes_ref, lam_ref, subw_ref, q0_ref, q1_ref, k0_ref, k1_ref,
                        v_ref, o_ref, m_sc, l_sc, acc_sc):
    j = pl.program_id(1)
    i = pl.program_id(2)
    t = q0_ref.shape[1]
    qm = _masked_queries((q0_ref, q1_ref))
    k_refs = (k0_ref, k1_ref)
    m_sc[...] = jnp.full(m_sc.shape, -jnp.inf, F32)
    l_sc[...] = jnp.zeros(l_sc.shape, F32)
    acc_sc[...] = jnp.zeros(acc_sc.shape, F32)

    def tile(kt_idx, diagonal):
        k_start = pl.multiple_of(kt_idx * t, t)
        row = lax.broadcasted_iota(jnp.int32, (t, t), 0)
        col = lax.broadcasted_iota(jnp.int32, (t, t), 1)
        dist = jnp.abs((i - kt_idx) * t + (col - row)).astype(F32)
        if diagonal:
            visible = (row // MASK_CHUNK) <= (col // MASK_CHUNK)
        for r in range(2):
            bias = -slopes_ref[2 * j + r] * dist
            vt = v_ref[0, pl.ds(k_start, t), r * LANES:(r + 1) * LANES]
            for mp in range(2):
                idx = 2 * r + mp
                kt = k_refs[mp][0, pl.ds(k_start, t), :]
                s = _dot_nt(kt, qm[mp][r]) + bias
                if diagonal:
                    s = jnp.where(visible, s, NEG_INF)
                m_prev = m_sc[idx]
                m_new = jnp.maximum(m_prev, jnp.max(s, axis=0, keepdims=True))
                alpha = jnp.exp2(m_prev - m_new)
                p = jnp.exp2(s - m_new)
                l_sc[idx] = alpha * l_sc[idx] + jnp.sum(p, axis=0, keepdims=True)
                acc_sc[idx] = alpha * acc_sc[idx] + _dot_tn(vt, p.astype(BF16))
                m_sc[idx] = m_new

    def below(kt_idx, carry):
        tile(kt_idx, False)
        return carry

    lax.fori_loop(0, i, below, 0)
    tile(i, True)
    _finish_transposed(lam_ref, subw_ref, lam_init, l_sc, acc_sc, o_ref)


def _prompt_attention(q, k, v, lam_vec, subln_w, slopes2, score_bound2, lam_init):
    b, t, d = q.shape
    tile = ATTN_TILE
    half = d // (2 * LANES)
    q_spec = lambda off: pl.BlockSpec((1, tile, LANES), lambda bi, j, i: (bi, i, j + off))
    k_spec = lambda off: pl.BlockSpec((1, t, LANES), lambda bi, j, i: (bi, 0, j + off))
    v_spec = pl.BlockSpec((1, t, 2 * LANES), lambda bi, j, i: (bi, 0, j))
    o_spec = pl.BlockSpec((1, tile, 2 * LANES), lambda bi, j, i: (bi, i, j))
    smem = pl.BlockSpec(memory_space=pltpu.SMEM)
    lam_spec = pl.BlockSpec(lam_vec.shape, lambda bi, j, i: (0, 0))
    subw_spec = pl.BlockSpec((2 * DA_HEAD_DIM, 1), lambda bi, j, i: (0, 0))
    data_specs = [q_spec(0), q_spec(half), k_spec(0), k_spec(half), v_spec]
    subw = subln_w.reshape(2 * DA_HEAD_DIM, 1)
    stat = pltpu.VMEM((4, 1, tile), F32)
    acc = pltpu.VMEM((4, 2 * DA_HEAD_DIM, tile), F32)
    common = dict(grid=(b, half, t // tile), out_specs=o_spec,
                  out_shape=jax.ShapeDtypeStruct((b, t, d), BF16),
                  compiler_params=_params("parallel", "parallel", "arbitrary"))

    def fixed(shift):
        reorder = lambda spec: pl.BlockSpec(spec.block_shape,
                                            lambda j, bi, i, f=spec.index_map: f(bi, j, i))
        return pl.pallas_call(
            functools.partial(_attn_rows_kernel, lam_init, t // tile),
            grid=(half, b, t // tile),
            in_specs=[smem, smem, reorder(lam_spec), reorder(subw_spec)]
            + [reorder(spec) for spec in data_specs],
            out_specs=reorder(o_spec),
            out_shape=jax.ShapeDtypeStruct((b, t, d), BF16),
            scratch_shapes=[pltpu.VMEM((t, 2 * tile), F32)],
            compiler_params=_params("arbitrary", "arbitrary", "arbitrary"),
            name="diff_attn_fixed",
        )(slopes2, shift, lam_vec, subw, q, q, k, k, v)

    def online(_):
        return pl.pallas_call(
            functools.partial(_attn_online_kernel, lam_init),
            in_specs=[smem, lam_spec, subw_spec] + data_specs,
            scratch_shapes=[stat, stat, acc],
            name="diff_attn_online", **common,
        )(slopes2, lam_vec, subw, q, q, k, k, v)

    return lax.cond(score_bound2[0] <= FIXED_SHIFT_MAX_SCORE * LOG2E, fixed, online, score_bound2)


def _sample_attn_kernel(lam_init, past_visible, slopes_ref, lam_ref, subw_ref, q_ref, kc_ref,
                        kn_ref, vc_ref, vn_ref, o_ref):
    tq = q_ref.shape[1]
    past = kc_ref.shape[3]
    half = DA_HEADS // 2
    lam = _lambda_scalar(lam_ref, lam_init)
    lane = lax.broadcasted_iota(jnp.int32, (tq, LANES), 1)
    zero = jnp.zeros((tq, LANES), BF16)

    row_c = lax.broadcasted_iota(jnp.int32, (tq, past), 0)
    col_c = lax.broadcasted_iota(jnp.int32, (tq, past), 1)
    dist_c = (past + row_c - col_c).astype(F32)
    row_n = lax.broadcasted_iota(jnp.int32, (tq, tq), 0)
    col_n = lax.broadcasted_iota(jnp.int32, (tq, tq), 1)
    dist_n = jnp.abs(row_n - col_n).astype(F32)
    if not past_visible:
        vis_c = (col_c // MASK_CHUNK) <= ((past + row_c) // MASK_CHUNK)
        vis_n = ((past + col_n) // MASK_CHUNK) <= ((past + row_n) // MASK_CHUNK)

    for pair in range(half):
        tiles = [slice((mp * half + pair) * LANES, (mp * half + pair + 1) * LANES) for mp in range(2)]
        kc = [kc_ref[mp, 2 * pair:2 * pair + 2].reshape(LANES, past).astype(BF16) for mp in range(2)]
        for r in range(2):
            head = 2 * pair + r
            slope = slopes_ref[head]
            keep = (lane >= DA_HEAD_DIM) if r else (lane < DA_HEAD_DIM)
            vc = vc_ref[pl.ds(head, past, stride=DA_HEADS), :].astype(BF16)
            vn = vn_ref[0, :, head * LANES:(head + 1) * LANES]
            outs = []
            for mp in range(2):
                qm = jnp.where(keep, q_ref[0, :, tiles[mp]], zero)
                s_c = _dot(qm, kc[mp]) - slope * dist_c
                s_n = _dot_nt(qm, kn_ref[0, :, tiles[mp]]) - slope * dist_n
                if not past_visible:
                    s_c = jnp.where(vis_c, s_c, NEG_INF)
                    s_n = jnp.where(vis_n, s_n, NEG_INF)
                m = jnp.maximum(jnp.max(s_c, axis=1, keepdims=True),
                                jnp.max(s_n, axis=1, keepdims=True))
                p_c = jnp.exp2(s_c - m)
                p_n = jnp.exp2(s_n - m)
                l = jnp.sum(p_c, axis=1, keepdims=True) + jnp.sum(p_n, axis=1, keepdims=True)
                acc = _dot(p_c.astype(BF16), vc) + _dot(p_n.astype(BF16), vn)
                outs.append(acc / l)
            o = outs[0] - lam * outs[1]
            ms = jnp.mean(o * o, axis=1, keepdims=True)
            o = o * lax.rsqrt(ms + EPS) * subw_ref[...] * (1.0 - lam_init)
            o_ref[0, :, head * LANES:(head + 1) * LANES] = o.astype(o_ref.dtype)


def _sample_attention(q, k_new, v_new, k_cache, v_cache, layer_j, lam_vec, subln_w, slopes2,
                      lam_init):
    b, t_q, d = q.shape
    past = k_cache.shape[5]
    past_visible = past % MASK_CHUNK == 0 and t_q <= MASK_CHUNK
    row = pl.BlockSpec((1, t_q, d), lambda bi: (bi, 0, 0))
    return pl.pallas_call(
        functools.partial(_sample_attn_kernel, lam_init, past_visible),
        grid=(b,),
        in_specs=[pl.BlockSpec(memory_space=pltpu.SMEM),
                  pl.BlockSpec(lam_vec.shape, lambda bi: (0, 0)),
                  pl.BlockSpec((1, 2 * DA_HEAD_DIM), lambda bi: (0, 0)),
                  row,
                  pl.BlockSpec((None, None, 2, DA_HEADS, DA_HEAD_DIM, past),
                               lambda bi: (layer_j, bi, 0, 0, 0, 0)),
                  row,
                  pl.BlockSpec((None, None, past * DA_HEADS, 2 * DA_HEAD_DIM),
                               lambda bi: (layer_j, bi, 0, 0)),
                  row],
        out_specs=row,
        out_shape=jax.ShapeDtypeStruct((b, t_q, d), BF16),
        compiler_params=_params("parallel"),
        name="diff_attn_sample",
    )(slopes2, lam_vec, subln_w.reshape(1, 2 * DA_HEAD_DIM), q, k_cache, k_new, v_cache, v_new)


def _row_tile(m):
    return 512 if m % 512 == 0 else 256


def _outproj_tile(m):
    return 1024 if m % 2048 == 0 else _row_tile(m)


def _hgrn_layer(x, s0, nw, w_in_bf, lb_logits, onw, w_out_bf, layer_j):
    b, t, d = x.shape
    x2 = x.reshape(b * t, d)
    q, fx, iv, z = _inproj_hgrn(x2, nw.reshape(1, d), w_in_bf, _outproj_tile(b * t))
    o, s_new = _hgrn_rec(q.reshape(b, t, d), fx.reshape(b, t, d), iv.reshape(b, t, d),
                         lb_logits, s0, layer_j)
    y = _outproj(o.reshape(b * t, d), z, x2, w_out_bf, onw.reshape(1, d), True,
                 _outproj_tile(b * t))
    return y.reshape(b, t, d), s_new


def _attn_layer(x, cache, nw, w_in_bf, qn_w, kn_w, lam_vec, subln_w, w_out_bf, layer_idx,
                layer_j, n_layers, k_stack, v_stack):
    b, t, d = x.shape
    x2 = x.reshape(b * t, d)
    groups = d // DA_HEAD_DIM
    gmat = jnp.kron(jnp.eye(MXU_COLS // DA_HEAD_DIM, dtype=F32),
                    jnp.full((DA_HEAD_DIM, DA_HEAD_DIM), 1.0 / DA_HEAD_DIM, F32)).astype(BF16)
    prompt = cache is None
    if t % 512 == 0:
        tm = 512
    else:
        tm = t if prompt else _row_tile(b * t)
    q, kb, vb, z, k_stack, v_stack = _inproj_attn(
        x2, t, nw.reshape(1, d), w_in_bf, jnp.tile(qn_w, groups).reshape(1, d),
        jnp.tile(kn_w, groups).reshape(1, d), gmat, layer_j, n_layers, k_stack, v_stack, prompt, tm)
    lam_init = 0.8 - 0.6 * math.exp(-0.3 * layer_idx)
    slopes2 = jnp.exp2(-8.0 * jnp.arange(1, DA_HEADS + 1, dtype=F32) / DA_HEADS) * LOG2E
    q, kb, vb = (a.reshape(b, t, d) for a in (q, kb, vb))
    if prompt:
        bound2 = (DA_HEAD_DIM ** 0.5 * LOG2E * 1.02) * jnp.max(jnp.abs(qn_w)) * jnp.max(jnp.abs(kn_w))
        o = _prompt_attention(q, kb, vb, lam_vec, subln_w, slopes2, bound2.reshape(1), lam_init)
    else:
        k_cache, v_cache = cache
        o = _sample_attention(q, kb, vb, k_cache, v_cache, layer_j, lam_vec, subln_w, slopes2,
                              lam_init)
    y = _outproj(o.reshape(b * t, d), z, x2, w_out_bf, jnp.ones((1, d), F32), False,
                 _outproj_tile(b * t))
    return y.reshape(b, t, d), k_stack, v_stack


def kernel(x_prompt, x_sample, cache_k, cache_v, state_hgrn, norm_w, hgrn_w_in, hgrn_lb_logits,
           hgrn_onorm_w, hgrn_w_out, attn_w_in, attn_q_norm, attn_k_norm, attn_lambda, attn_subln,
           attn_w_out):
    depth = norm_w.shape[0]
    n_attn = cache_k.shape[0]
    bp, tp, d = x_prompt.shape
    bs, ts, _ = x_sample.shape
    past = cache_k.shape[2]
    cache_k_t = jnp.transpose(cache_k, (0, 1, 3, 4, 5, 2))
    cache_v2 = cache_v.reshape(n_attn, bs, past * DA_HEADS, 2 * DA_HEAD_DIM)
    yp, ys = x_prompt, x_sample
    kp = vp = ks_ = vs_ = None
    sp, ss = [], []
    for l in range(depth):
        j = l // N_MIXERS
        if l % N_MIXERS == 0:
            w_in = hgrn_w_in[j].astype(BF16)
            w_out = hgrn_w_out[j].astype(BF16)
            yp, s_p = _hgrn_layer(yp, None, norm_w[l], w_in, hgrn_lb_logits, hgrn_onorm_w[j],
                                  w_out, j)
            ys, s_s = _hgrn_layer(ys, state_hgrn[j], norm_w[l], w_in, hgrn_lb_logits,
                                  hgrn_onorm_w[j], w_out, j)
            sp.append(s_p)
            ss.append(s_s)
        else:
            w_in = attn_w_in[j].astype(BF16)
            w_out = attn_w_out[j].astype(BF16)
            yp, kp, vp = _attn_layer(yp, None, norm_w[l], w_in, attn_q_norm[j], attn_k_norm[j],
                                     attn_lambda[j], attn_subln[j], w_out, l, j, n_attn, kp, vp)
            ys, ks_, vs_ = _attn_layer(ys, (cache_k_t, cache_v2), norm_w[l], w_in,
                                       attn_q_norm[j], attn_k_norm[j], attn_lambda[j],
                                       attn_subln[j], w_out, l, j, n_attn, ks_, vs_)
    new_k_prompt = jnp.transpose(kp.reshape(n_attn, bp, 2, DA_HEADS, DA_HEAD_DIM, tp),
                                 (0, 1, 5, 2, 3, 4))
    return (yp, ys, new_k_prompt,
            vp.reshape(n_attn, bp, tp, DA_HEADS, 2 * DA_HEAD_DIM),
            ks_.reshape(n_attn, bs, ts, 2, DA_HEADS, DA_HEAD_DIM),
            vs_.reshape(n_attn, bs, ts, DA_HEADS, 2 * DA_HEAD_DIM),
            jnp.stack(sp), jnp.stack(ss))
```
